```python
import math
import jax, jax.numpy as jnp
from jax import lax
import numpy as np

D_MODEL = 2048
BATCH = 2
SEQ = 4096
DEPTH = 4
DEC_BATCH = 8
DEC_SEQ = 8
PAST_LEN = 16384
PAGE_SIZE = 128

D_MIX = D_MODEL
HEAD_DIM_A = 64
D_A = (3 * D_MIX) // 8
H_A = D_A // HEAD_DIM_A
D_B = D_MIX // 4
G_B = 4
C_B = D_B // G_B
CHUNK_B = 128
D_C = D_MIX - D_A - D_B
DH_C = 128
H_C = D_C // DH_C
CHUNK_C = 128
CONV_W = 4
PATTERNS = ((128, 1), (512, 4), (2048, 16))
WIN_MAX = 2048
BLK_A = 128
N_BUCKETS = 32
MAX_DIST = 2048
EPS = 1e-6

_SIZES = (D_A, D_A, D_A, D_A, D_B, D_B, D_B, 2 * D_C, D_C, D_C, D_C, H_C, H_C)
D_IN = sum(_SIZES)
SPLIT_IDX = tuple(int(s) for s in np.cumsum(_SIZES)[:-1])

kernel_name = 'hybrid_dilated_gmlp_mlstm_decoder_step'


def _rmsnorm(x, g):
    xf = x.astype(jnp.float32)
    y = xf * lax.rsqrt(jnp.mean(xf * xf, axis=-1, keepdims=True) + EPS)
    return (y * g.astype(jnp.float32)).astype(x.dtype)


def _t5_bucket(dist):
    max_exact = N_BUCKETS // 2
    df = jnp.maximum(dist, 1).astype(jnp.float32)
    large = max_exact + (jnp.log(df / max_exact) / math.log(MAX_DIST / max_exact)
                         * (N_BUCKETS - max_exact)).astype(jnp.int32)
    return jnp.where(dist < max_exact, dist, jnp.minimum(large, N_BUCKETS - 1))


def _combine(outs, lses):
    w = jax.nn.softmax(jnp.stack(lses), axis=0)
    return jnp.einsum('pbsh,pbshd->bshd', w, jnp.stack(outs))


def _dilated_attn_prompt(q, k, v, rel_bias):
    B, S, H, Dh = q.shape
    outs, lses = [], []
    for win, dil in PATTERNS:
        n_back = win // dil
        blk = BLK_A
        nb = -(-S // (dil * blk))
        sp = nb * blk * dil
        padw = ((0, 0), (0, sp - S), (0, 0), (0, 0))
        qs = jnp.pad(q, padw).reshape(B, nb, blk, dil, H, Dh)
        ks = jnp.pad(k, padw).reshape(B, nb, blk, dil, H, Dh)
        vs = jnp.pad(v, padw).reshape(B, nb, blk, dil, H, Dh)
        prev = ((0, 0), (1, 0), (0, 0), (0, 0), (0, 0), (0, 0))
        kk = jnp.concatenate([jnp.pad(ks, prev)[:, :-1], ks], axis=2)
        vv = jnp.concatenate([jnp.pad(vs, prev)[:, :-1], vs], axis=2)
        qi = jnp.arange(blk)[:, None]
        ki = jnp.arange(2 * blk)[None, :]
        j = qi + blk - ki
        band = (j >= 0) & (j <= n_back)
        first = (jnp.arange(nb) == 0)[:, None, None] & (ki < blk)[None]
        mask = band[None] & ~first
        bias = rel_bias[_t5_bucket(jnp.clip(j, 0, n_back) * dil)].astype(jnp.float32).transpose(2, 0, 1)
        s = jnp.einsum('bnqrhd,bnkrhd->bnrhqk', qs, kk).astype(jnp.float32) + bias
        s = jnp.where(mask[None, :, None, None], s, -jnp.inf)
        lse = jax.nn.logsumexp(s, axis=-1)
        p = jnp.exp(s - lse[..., None])
        o = jnp.einsum('bnrhqk,bnkrhd->bnqrhd', p, vv.astype(jnp.float32))
        outs.append(o.reshape(B, sp, H, Dh)[:, :S])
        lses.append(lse.transpose(0, 1, 4, 2, 3).reshape(B, sp, H)[:, :S])
    return _combine(outs, lses)


def _dilated_attn_sample(q, k_all, v_all, rel_bias, n_past):
    T = q.shape[1]
    outs, lses = [], []
    for win, dil in PATTERNS:
        jj = jnp.arange(win // dil + 1)
        idx = n_past + jnp.arange(T)[:, None] - jj[None, :] * dil
        valid = idx >= 0
        idx = jnp.maximum(idx, 0)
        kg = k_all[:, idx]
        vg = v_all[:, idx]
        bias = rel_bias[_t5_bucket(jj * dil)].astype(jnp.float32).T
        s = jnp.einsum('bthd,btjhd->bthj', q, kg).astype(jnp.float32) + bias[None, None]
        s = jnp.where(valid[None, :, None, :], s, -jnp.inf)
        lse = jax.nn.logsumexp(s, axis=-1)
        p = jnp.exp(s - lse[..., None])
        outs.append(jnp.einsum('bthj,btjhd->bthd', p, vg.astype(jnp.float32)))
        lses.append(lse)
    return _combine(outs, lses)


def _sgu(u, vn, w_s, b_s):
    B, S, _ = u.shape
    L = min(CHUNK_B, S)
    nc = S // L
    w = jnp.tril(w_s[:, :L, :L])
    vg = vn.reshape(B, nc, L, G_B, C_B)
    mix = jnp.einsum('gts,bnsgc->bntgc', w, vg) + b_s[:, :L].T[None, None, :, :, None]
    return u * mix.reshape(B, S, D_B).astype(u.dtype)


def _mlstm(q, k, v, i_pre, f_pre, C0, n0, m0):
    B, S, H, D = q.shape
    L = min(CHUNK_C, S)
    nc = S // L
    f32 = jnp.float32
    q = q.astype(f32)
    k = k.astype(f32) * (D ** -0.5)
    v = v.astype(f32)
    ig = i_pre.astype(f32)
    logf = jax.nn.log_sigmoid(f_pre.astype(f32))

    def chunks(a):
        return a.reshape((B, nc, L) + a.shape[2:]).swapaxes(0, 1)

    causal = jnp.tril(jnp.ones((L, L), dtype=bool))

    def step(carry, inp):
        C, n, m = carry
        qc, kc, vc, ic, fc = inp
        bh = jnp.cumsum(fc, axis=1).transpose(0, 2, 1)
        ih = ic.transpose(0, 2, 1)
        dlog = jnp.where(causal, bh[:, :, :, None] - bh[:, :, None, :] + ih[:, :, None, :], -jnp.inf)
        inter = bh + m[:, :, None]
        mt = jnp.maximum(inter, jnp.max(dlog, axis=-1))
        a = jnp.exp(dlog - mt[..., None]) * jnp.einsum('bthd,bshd->bhts', qc, kc)
        w_inter = jnp.exp(inter - mt)
        num = (jnp.einsum('bhts,bshd->bthd', a, vc)
               + w_inter.transpose(0, 2, 1)[..., None] * jnp.einsum('bhkv,bthk->bthv', C, qc))
        den = jnp.sum(a, axis=-1) + w_inter * jnp.einsum('bhk,bthk->bht', n, qc)
        denom = jnp.maximum(jnp.abs(den), jnp.exp(-mt))
        h = num / denom.transpose(0, 2, 1)[..., None]
        bl = bh[:, :, -1]
        g = bl[:, :, None] - bh + ih
        m_new = jnp.maximum(bl + m, jnp.max(g, axis=-1))
        ws = jnp.exp(g - m_new[..., None])
        wc = jnp.exp(bl + m - m_new)
        C_new = wc[..., None, None] * C + jnp.einsum('bhs,bshk,bshv->bhkv', ws, kc, vc)
        n_new = wc[..., None] * n + jnp.einsum('bhs,bshk->bhk', ws, kc)
        return (C_new, n_new, m_new), h

    init = (C0.astype(f32), n0.astype(f32), m0.astype(f32))
    (Cf, nf, mf), hs = lax.scan(step, init, (chunks(q), chunks(k), chunks(v), chunks(ig), chunks(logf)))
    return hs.swapaxes(0, 1).reshape(B, S, H, D), Cf, nf, mf


def _layer(x, c, norm_g, ada_w, ada_b, w_in, qn_g, kn_g, sgu_g, sgu_w, sgu_b, conv_w, conv_b,
           f_bias, i_bias, hn_g, w_out, rel_bias,
           k_buf=None, v_buf=None, conv_buf=None, C0=None, n0=None, m0=None):
    B, S, _ = x.shape
    dt = x.dtype
    f32 = jnp.float32
    mod = jax.nn.silu(c.astype(f32)) @ ada_w.astype(f32) + ada_b.astype(f32)
    shift, scale, gate = jnp.split(mod, 3, axis=-1)
    h = (_rmsnorm(x, norm_g).astype(f32) * (1.0 + scale[:, None]) + shift[:, None]).astype(dt)
    proj = h @ w_in
    (q_a, k_a, v_a, z_a, u_b, v_b, z_b, qk_c, v_c, o_c, z_c, i_c, f_c) = jnp.split(proj, SPLIT_IDX, axis=-1)

    q = _rmsnorm(q_a.reshape(B, S, H_A, HEAD_DIM_A), qn_g) * (HEAD_DIM_A ** -0.5)
    k = _rmsnorm(k_a.reshape(B, S, H_A, HEAD_DIM_A), kn_g)
    v = v_a.reshape(B, S, H_A, HEAD_DIM_A)
    if k_buf is None:
        attn = _dilated_attn_prompt(q, k, v, rel_bias)
        wkeep = min(WIN_MAX, S)
        new_k, new_v = k[:, S - wkeep:], v[:, S - wkeep:]
    else:
        k_all = jnp.concatenate([k_buf.astype(k.dtype), k], axis=1)
        v_all = jnp.concatenate([v_buf.astype(v.dtype), v], axis=1)
        attn = _dilated_attn_sample(q, k_all, v_all, rel_bias, k_buf.shape[1])
        new_k, new_v = k, v
    y_a = attn.reshape(B, S, D_A).astype(dt) * jax.nn.silu(z_a)

    vn = _rmsnorm(v_b, sgu_g)
    y_b = _sgu(u_b, vn, sgu_w, sgu_b) * jax.nn.silu(z_b)

    if conv_buf is None:
        conv_buf = jnp.zeros((B, CONV_W - 1, 2 * D_C), dt)
    xp = jnp.concatenate([conv_buf.astype(qk_c.dtype), qk_c], axis=1)
    qk = conv_b
    for j in range(CONV_W):
        qk = qk + conv_w[j] * xp[:, j:j + S]
    q_c, k_c = jnp.split(jax.nn.silu(qk), 2, axis=-1)
    new_conv = xp[:, -(CONV_W - 1):]
    if C0 is None:
        C0 = jnp.zeros((B, H_C, DH_C, DH_C), f32)
        n0 = jnp.zeros((B, H_C, DH_C), f32)
        m0 = jnp.zeros((B, H_C), f32)
    hc, Cn, nn_, mn = _mlstm(q_c.reshape(B, S, H_C, DH_C), k_c.reshape(B, S, H_C, DH_C),
                             v_c.reshape(B, S, H_C, DH_C), i_c + i_bias, f_c + f_bias, C0, n0, m0)
    hc = _rmsnorm(hc, hn_g.reshape(H_C, DH_C))
    y_c = (hc.reshape(B, S, D_C) * jax.nn.sigmoid(o_c.astype(f32)) * jax.nn.silu(z_c.astype(f32))).astype(dt)

    y = jnp.concatenate([y_a, y_b, y_c], axis=-1) @ w_out
    x = x + (gate[:, None] * y.astype(f32)).astype(dt)
    return x, new_k, new_v, vn, new_conv, Cn, nn_, mn


def setup_inputs(seed: int = 0) -> dict:
    key = jax.random.key(seed)
    ks = jax.random.split(key, 26)
    f32 = jnp.float32

    def nrm(k, shape, s):
        return jax.random.normal(k, shape, f32) * s

    win_buf = min(WIN_MAX, PAST_LEN)
    ada_b = nrm(ks[13], (DEPTH, 3 * D_MODEL), 0.02).at[:, 2 * D_MODEL:].add(1.0)
    return {
        'x_prompt': nrm(ks[0], (BATCH, SEQ, D_MODEL), 1.0),
        'x_sample': nrm(ks[1], (DEC_BATCH, DEC_SEQ, D_MODEL), 1.0),
        'c_prompt': nrm(ks[2], (BATCH, D_MODEL), 1.0),
        'c_sample': nrm(ks[3], (DEC_BATCH, D_MODEL), 1.0),
        'cache_k_win': nrm(ks[4], (DEPTH, DEC_BATCH, win_buf, H_A, HEAD_DIM_A), 1.0),
        'cache_v_win': nrm(ks[5], (DEPTH, DEC_BATCH, win_buf, H_A, HEAD_DIM_A), 1.0),
        'state_conv': nrm(ks[6], (DEPTH, DEC_BATCH, CONV_W - 1, 2 * D_C), 1.0),
        'state_C': nrm(ks[7], (DEPTH, DEC_BATCH, H_C, DH_C, DH_C), 0.1),
        'state_n': nrm(ks[8], (DEPTH, DEC_BATCH, H_C, DH_C), 0.1),
        'state_m': nrm(ks[9], (DEPTH, DEC_BATCH, H_C), 1.0),
        'rel_bias': nrm(ks[10], (N_BUCKETS, H_A), 0.2),
        'norm_g': 1.0 + nrm(ks[11], (DEPTH, D_MODEL), 0.02),
        'ada_w': nrm(ks[12], (DEPTH, D_MODEL, 3 * D_MODEL), 0.1 * D_MODEL ** -0.5),
        'ada_b': ada_b,
        'w_in': nrm(ks[14], (DEPTH, D_MODEL, D_IN), D_MODEL ** -0.5),
        'qn_g': 1.0 + nrm(ks[15], (DEPTH, HEAD_DIM_A), 0.02),
        'kn_g': 1.0 + nrm(ks[16], (DEPTH, HEAD_DIM_A), 0.02),
        'sgu_g': 1.0 + nrm(ks[17], (DEPTH, D_B), 0.02),
        'sgu_w': nrm(ks[18], (DEPTH, G_B, CHUNK_B, CHUNK_B), CHUNK_B ** -0.5),
        'sgu_b': 1.0 + nrm(ks[19], (DEPTH, G_B, CHUNK_B), 0.02),
        'conv_w': nrm(ks[20], (DEPTH, CONV_W, 2 * D_C), CONV_W ** -0.5),
        'conv_b': nrm(ks[21], (DEPTH, 2 * D_C), 0.02),
        'f_bias': 3.0 + 3.0 * jax.random.uniform(ks[22], (DEPTH, H_C), f32),
        'i_bias': nrm(ks[23], (DEPTH, H_C), 0.1),
        'hn_g': 1.0 + nrm(ks[24], (DEPTH, D_C), 0.02),
        'w_out': nrm(ks[25], (DEPTH, D_MIX, D_MODEL), D_MIX ** -0.5),
    }


def reference(x_prompt, x_sample, c_prompt, c_sample, cache_k_win, cache_v_win, state_conv, state_C,
              state_n, state_m, rel_bias, norm_g, ada_w, ada_b, w_in, qn_g, kn_g, sgu_g, sgu_w, sgu_b,
              conv_w, conv_b, f_bias, i_bias, hn_g, w_out):
    xp, xs = x_prompt, x_sample
    p_k, p_v, p_conv, p_C, p_n, p_m = [], [], [], [], [], []
    s_k, s_v, s_sgu, s_conv, s_C, s_n, s_m = [], [], [], [], [], [], []
    for l in range(DEPTH):
        w = (norm_g[l], ada_w[l], ada_b[l], w_in[l], qn_g[l], kn_g[l], sgu_g[l], sgu_w[l], sgu_b[l],
             conv_w[l], conv_b[l], f_bias[l], i_bias[l], hn_g[l], w_out[l], rel_bias)
        xp, nk, nv, _, ncv, nC, nn_, nm = _layer(xp, c_prompt, *w)
        p_k.append(nk); p_v.append(nv); p_conv.append(ncv); p_C.append(nC); p_n.append(nn_); p_m.append(nm)
        xs, nk, nv, nsv, ncv, nC, nn_, nm = _layer(xs, c_sample, *w, cache_k_win[l], cache_v_win[l],
                                                  state_conv[l], state_C[l], state_n[l], state_m[l])
        s_k.append(nk); s_v.append(nv); s_sgu.append(nsv); s_conv.append(ncv)
        s_C.append(nC); s_n.append(nn_); s_m.append(nm)
    return (xp, xs,
            jnp.stack(p_k), jnp.stack(p_v), jnp.stack(p_conv), jnp.stack(p_C), jnp.stack(p_n), jnp.stack(p_m),
            jnp.stack(s_k), jnp.stack(s_v), jnp.stack(s_sgu), jnp.stack(s_conv), jnp.stack(s_C),
            jnp.stack(s_n), jnp.stack(s_m))
```

```python
import functools
import math

import numpy as np
import jax
import jax.numpy as jnp
from jax import lax
from jax.experimental import pallas as pl
from jax.experimental.pallas import tpu as pltpu

F32 = jnp.float32
BF16 = jnp.bfloat16

D_MODEL = 2048
HEAD_DIM_A = 64
D_A = 768
H_A = 12
D_B = 512
G_B = 4
C_B = 128
CHUNK_B = 128
D_C = 768
DH_C = 128
H_C = 6
CHUNK_C = 128
CONV_W = 4
PATTERNS = ((128, 1), (512, 4), (2048, 16))
N_PAT = len(PATTERNS)
WIN_MAX = 2048
BLK_A = 128
N_BUCKETS = 32
MAX_DIST = 2048
EPS = 1e-6
D_MAIN = 4 * D_A + 3 * D_B + 2 * D_C + 3 * D_C
D_IN = D_MAIN + 2 * H_C

LANES = 128
SUBLANES = 8
VMEM_LIMIT = 56 * 1024 * 1024

NEG = -1e30
PAD_A = BLK_A * PATTERNS[-1][1]
HPAIR = LANES // HEAD_DIM_A
SAMPLE_KEYS = 2176

COL_Q, COL_K, COL_V, COL_Z = 0, D_A // LANES, 2 * D_A // LANES, 3 * D_A // LANES
COL_UB, COL_VB, COL_ZB = 4 * D_A // D_B, 4 * D_A // D_B + 1, 4 * D_A // D_B + 2
COL_QK = (4 * D_A + 3 * D_B) // (2 * D_C)
COL_VC = (4 * D_A + 3 * D_B + 2 * D_C) // D_C
COL_OC, COL_ZC = COL_VC + 1, COL_VC + 2


def _cparams(sem):
    return pltpu.CompilerParams(dimension_semantics=sem, vmem_limit_bytes=VMEM_LIMIT)


def _silu(x):
    return x * jax.nn.sigmoid(x)


def _bdot(a, b):
    return jnp.dot(a.astype(BF16), b.astype(BF16), preferred_element_type=F32)


def _bdot_nt(a, b):
    return lax.dot_general(a.astype(BF16), b.astype(BF16), (((1,), (1,)), ((), ())),
                           preferred_element_type=F32)


def _bdot_tn(a, b):
    return lax.dot_general(a.astype(BF16), b.astype(BF16), (((0,), (0,)), ((), ())),
                           preferred_element_type=F32)


def _bucket_np(dist):
    max_exact = N_BUCKETS // 2
    df = np.maximum(dist, 1).astype(np.float32)
    large = max_exact + (np.log(df / np.float32(max_exact)) / np.float32(math.log(MAX_DIST / max_exact))
                         * np.float32(N_BUCKETS - max_exact)).astype(np.int32)
    return np.where(dist < max_exact, dist, np.minimum(large, N_BUCKETS - 1)).astype(np.int32)


def _prompt_bucket_table():
    qi = np.arange(BLK_A)[:, None]
    ki = np.arange(2 * BLK_A)[None, :]
    j = qi + BLK_A - ki
    out = []
    for win, dil in PATTERNS:
        n_back = win // dil
        band = (j >= 0) & (j <= n_back)
        b = _bucket_np(np.clip(j, 0, n_back) * dil)
        out.append(np.where(band, b, -1))
        out.append(np.where(band & (ki >= BLK_A), b, -1))
    return np.stack(out).reshape(N_PAT * 2 * BLK_A, 2 * BLK_A).astype(np.int32)


def _sample_bucket_table(n_past, t_new):
    c = np.arange(SAMPLE_KEYS)[None, :]
    t = np.arange(t_new)[:, None]
    delta = n_past + t - c
    out = []
    for win, dil in PATTERNS:
        valid = (c < n_past + t_new) & (delta >= 0) & (delta % dil == 0) & (delta // dil <= win // dil)
        out.append(np.where(valid, _bucket_np(np.maximum(delta, 0)), -1))
    return np.stack(out).reshape(N_PAT * t_new, SAMPLE_KEYS).astype(np.int32)


def _bias_kernel(rb_ref, idx_ref, out_ref):
    h = pl.program_id(0)
    idx = idx_ref[...]
    out = jnp.full(idx.shape, NEG, F32)
    for b in range(N_BUCKETS):
        out = jnp.where(idx == b, rb_ref[h, b], out)
    out_ref[...] = out


def _expand_bias(rel_bias_t, idx, row_block):
    rows, cols = idx.shape
    return pl.pallas_call(
        _bias_kernel,
        out_shape=jax.ShapeDtypeStruct((H_A, rows, cols), F32),
        grid=(H_A, rows // row_block),
        in_specs=[pl.BlockSpec(memory_space=pltpu.SMEM),
                  pl.BlockSpec((row_block, cols), lambda h, r: (r, 0))],
        out_specs=pl.BlockSpec((None, row_block, cols), lambda h, r: (h, r, 0)),
        compiler_params=_cparams(("arbitrary", "arbitrary")),
        name="bias_expand",
    )(rel_bias_t, idx)


def _ada_kernel(c_ref, w_ref, b_ref, o_ref):
    c = c_ref[...]
    a = _silu(c)
    w = w_ref[...]
    a_hi = a.astype(BF16)
    a_lo = (a - a_hi.astype(F32)).astype(BF16)
    w_hi = w.astype(BF16)
    w_lo = (w - w_hi.astype(F32)).astype(BF16)
    acc = jnp.dot(a_hi, w_hi, preferred_element_type=F32)
    acc += jnp.dot(a_hi, w_lo, preferred_element_type=F32)
    acc += jnp.dot(a_lo, w_hi, preferred_element_type=F32)
    o_ref[...] = acc + b_ref[...]


def _ada_mod(c_all, ada_w, ada_b):
    depth = ada_w.shape[0]
    rows = c_all.shape[0]
    tn = 768
    n = 3 * D_MODEL
    return pl.pallas_call(
        _ada_kernel,
        out_shape=jax.ShapeDtypeStruct((depth, rows, n), F32),
        grid=(depth, n // tn),
        in_specs=[pl.BlockSpec((rows, D_MODEL), lambda l, j: (0, 0)),
                  pl.BlockSpec((None, D_MODEL, tn), lambda l, j: (l, 0, j)),
                  pl.BlockSpec((None, 1, tn), lambda l, j: (l, 0, j))],
        out_specs=pl.BlockSpec((None, rows, tn), lambda l, j: (l, 0, j)),
        compiler_params=_cparams(("arbitrary", "arbitrary")),
        name="ada_mod",
    )(c_all, ada_w, ada_b.reshape(depth, 1, n))


def _inproj_kernel(x_ref, sc_ref, sh_ref, g_ref, w_ref, wg_ref, proj_ref, gates_ref, h_scr, *, row_chunk):
    @pl.when(pl.program_id(1) == 0)
    def _():
        def chunk(c, carry):
            rows = pl.ds(pl.multiple_of(c * row_chunk, row_chunk), row_chunk)
            x = x_ref[rows, :]
            ms = jnp.sum(x * x, axis=-1, keepdims=True) * (1.0 / D_MODEL)
            y = x * lax.rsqrt(ms + EPS) * g_ref[...]
            sc = sc_ref[...] if sc_ref.shape[0] == 1 else sc_ref[rows, :]
            sh = sh_ref[...] if sh_ref.shape[0] == 1 else sh_ref[rows, :]
            h_scr[rows, :] = (y * (1.0 + sc) + sh).astype(BF16)
            return carry
        lax.fori_loop(0, x_ref.shape[0] // row_chunk, chunk, 0)
        gates_ref[...] = jnp.dot(h_scr[...], wg_ref[...], preferred_element_type=F32)

    proj_ref[...] = jnp.dot(h_scr[...], w_ref[...], preferred_element_type=F32)


def _inproj(x2d, scale, shift, norm_g, w_in_bf, w_gate_bf, layer, tm):
    m = x2d.shape[0]
    groups = scale.shape[0]
    tiles_per_group = m // tm // groups
    tn = 768
    mod_spec = pl.BlockSpec((None, scale.shape[1], D_MODEL), lambda i, j: (i // tiles_per_group, 0, 0))
    return pl.pallas_call(
        functools.partial(_inproj_kernel, row_chunk=min(tm, 256)),
        out_shape=(jax.ShapeDtypeStruct((m, D_MAIN), F32), jax.ShapeDtypeStruct((m, LANES), F32)),
        grid=(m // tm, D_MAIN // tn),
        in_specs=[pl.BlockSpec((tm, D_MODEL), lambda i, j: (i, 0)),
                  mod_spec, mod_spec,
                  pl.BlockSpec((1, D_MODEL), lambda i, j: (0, 0)),
                  pl.BlockSpec((None, D_MODEL, tn), lambda i, j: (layer, 0, j)),
                  pl.BlockSpec((None, D_MODEL, LANES), lambda i, j: (layer, 0, 0))],
        out_specs=(pl.BlockSpec((tm, tn), lambda i, j: (i, j)),
                   pl.BlockSpec((tm, LANES), lambda i, j: (i, 0))),
        scratch_shapes=[pltpu.VMEM((tm, D_MODEL), BF16)],
        compiler_params=_cparams(("arbitrary", "arbitrary")),
        name="inproj",
    )(x2d, scale, shift, norm_g, w_in_bf, w_gate_bf)


def _outproj_kernel(ya_ref, yb_ref, yc_ref, x_ref, gate_ref, w_ref, o_ref):
    y = jnp.dot(ya_ref[...], w_ref[0:D_A, :], preferred_element_type=F32)
    y += jnp.dot(yb_ref[...], w_ref[D_A:D_A + D_B, :], preferred_element_type=F32)
    y += jnp.dot(yc_ref[...], w_ref[D_A + D_B:, :], preferred_element_type=F32)
    o_ref[...] = x_ref[...] + gate_ref[...] * y


def _outproj(ya, yb, yc, x2d, gate, w_out_bf, layer, tm):
    m = x2d.shape[0]
    groups = gate.shape[0]
    tiles_per_group = m // tm // groups
    return pl.pallas_call(
        _outproj_kernel,
        out_shape=jax.ShapeDtypeStruct((m, D_MODEL), F32),
        grid=(m // tm,),
        in_specs=[pl.BlockSpec((tm, D_A), lambda i: (i, 0)),
                  pl.BlockSpec((tm, D_B), lambda i: (i, 0)),
                  pl.BlockSpec((tm, D_C), lambda i: (i, 0)),
                  pl.BlockSpec((tm, D_MODEL), lambda i: (i, 0)),
                  pl.BlockSpec((None, gate.shape[1], D_MODEL), lambda i: (i // tiles_per_group, 0, 0)),
                  pl.BlockSpec((None, D_MODEL, D_MODEL), lambda i: (layer, 0, 0))],
        out_specs=pl.BlockSpec((tm, D_MODEL), lambda i: (i, 0)),
        compiler_params=_cparams(("arbitrary",)),
        name="outproj",
    )(ya, yb, yc, x2d, gate, w_out_bf)


def _head_norm(x, g, left):
    x2 = x * x
    s_left = jnp.sum(jnp.where(left, x2, 0.0), axis=-1, keepdims=True)
    s_right = jnp.sum(jnp.where(left, 0.0, x2), axis=-1, keepdims=True)
    ms = jnp.where(left, s_left, s_right) * (1.0 / HEAD_DIM_A)
    return x * lax.rsqrt(ms + EPS) * g


def _attn_prompt_kernel(q_ref, k_ref, v_ref, z_ref, gq_ref, gk_ref, bias_ref,
                        y_ref, pk_ref, pv_ref,
                        qn_s, kp_s, vp_s, m_s, l_s, acc_s, *, seq, keep):
    norm_rows = 512
    lane_n = lax.broadcasted_iota(jnp.int32, (norm_rows, LANES), 1)
    left_n = lane_n < HEAD_DIM_A

    kp_s[0:PAD_A, :] = jnp.zeros((PAD_A, LANES), F32)
    vp_s[0:PAD_A, :] = jnp.zeros((PAD_A, LANES), F32)

    def norm_chunk(c, carry):
        r0 = pl.multiple_of(c * norm_rows, norm_rows)
        rows = pl.ds(r0, norm_rows)
        qn_s[rows, :] = _head_norm(q_ref[rows, :], gq_ref[...], left_n) * (HEAD_DIM_A ** -0.5)
        kp_s[pl.ds(PAD_A + r0, norm_rows), :] = _head_norm(k_ref[rows, :], gk_ref[...], left_n)
        vp_s[pl.ds(PAD_A + r0, norm_rows), :] = v_ref[rows, :]
        m_s[rows, :] = jnp.full((norm_rows, LANES), NEG, F32)
        l_s[rows, :] = jnp.zeros((norm_rows, LANES), F32)
        acc_s[rows, :] = jnp.zeros((norm_rows, LANES), F32)
        return carry
    lax.fori_loop(0, seq // norm_rows, norm_chunk, 0)

    pk_ref[...] = kp_s[PAD_A + seq - keep:PAD_A + seq, :]
    pv_ref[...] = vp_s[PAD_A + seq - keep:PAD_A + seq, :]

    lane_b = lax.broadcasted_iota(jnp.int32, (BLK_A, LANES), 1)
    left_b = lane_b < HEAD_DIM_A

    for p, (win, dil) in enumerate(PATTERNS):
        blocks_per_residue = seq // (dil * BLK_A)

        def block(i, carry, p=p, dil=dil, blocks_per_residue=blocks_per_residue):
            r = i // blocks_per_residue
            n = i % blocks_per_residue
            q_start = r + n * (BLK_A * dil)
            k_start = PAD_A + q_start - BLK_A * dil
            if dil == 1:
                q_rows = pl.ds(pl.multiple_of(q_start, BLK_A), BLK_A)
                k_rows = pl.ds(pl.multiple_of(k_start, BLK_A), 2 * BLK_A)
            else:
                q_rows = pl.ds(q_start, BLK_A, stride=dil)
                k_rows = pl.ds(k_start, 2 * BLK_A, stride=dil)
            first = jnp.where(n == 0, 1, 0)
            q = qn_s[q_rows, :]
            k = kp_s[k_rows, :].astype(BF16)
            v = vp_s[k_rows, :].astype(BF16)
            ms, ls, os_ = [], [], []
            for h in range(HPAIR):
                sel = left_b if h == 0 else jnp.logical_not(left_b)
                qh = jnp.where(sel, q, 0.0)
                s = _bdot_nt(qh, k) + bias_ref[h, p, first]
                mh = jnp.max(s, axis=-1, keepdims=True)
                e = jnp.exp(s - mh)
                ms.append(mh)
                ls.append(jnp.sum(e, axis=-1, keepdims=True))
                os_.append(jnp.dot(e.astype(BF16), v, preferred_element_type=F32))
            m_p = jnp.where(left_b, ms[0], ms[1])
            l_p = jnp.where(left_b, ls[0], ls[1])
            o_p = jnp.where(left_b, os_[0], os_[1])
            m_old = m_s[q_rows, :]
            m_new = jnp.maximum(m_old, m_p)
            a_old = jnp.exp(m_old - m_new)
            a_new = jnp.exp(m_p - m_new)
            m_s[q_rows, :] = m_new
            l_s[q_rows, :] = a_old * l_s[q_rows, :] + a_new * l_p
            acc_s[q_rows, :] = a_old * acc_s[q_rows, :] + a_new * o_p
            return carry
        lax.fori_loop(0, seq // BLK_A, block, 0)

    def out_chunk(c, carry):
        rows = pl.ds(pl.multiple_of(c * norm_rows, norm_rows), norm_rows)
        y_ref[rows, :] = (acc_s[rows, :] / l_s[rows, :] * _silu(z_ref[rows, :])).astype(BF16)
        return carry
    lax.fori_loop(0, seq // norm_rows, out_chunk, 0)


def _attn_prompt(proj, gq2, gk2, bias_p):
    b, seq, _ = proj.shape
    keep = min(WIN_MAX, seq)
    assert seq % (BLK_A * PATTERNS[-1][1]) == 0
    col = lambda base: pl.BlockSpec((None, seq, LANES), lambda bi, hp: (bi, 0, base + hp))
    vec = pl.BlockSpec((1, LANES), lambda bi, hp: (0, 0))
    return pl.pallas_call(
        functools.partial(_attn_prompt_kernel, seq=seq, keep=keep),
        out_shape=(jax.ShapeDtypeStruct((b, seq, D_A), BF16),
                   jax.ShapeDtypeStruct((b, keep, D_A), F32),
                   jax.ShapeDtypeStruct((b, keep, D_A), F32)),
        grid=(b, H_A // HPAIR),
        in_specs=[col(COL_Q), col(COL_K), col(COL_V), col(COL_Z), vec, vec,
                  pl.BlockSpec((HPAIR, N_PAT, 2, BLK_A, 2 * BLK_A), lambda bi, hp: (hp, 0, 0, 0, 0))],
        out_specs=(pl.BlockSpec((None, seq, LANES), lambda bi, hp: (bi, 0, hp)),
                   pl.BlockSpec((None, keep, LANES), lambda bi, hp: (bi, 0, hp)),
                   pl.BlockSpec((None, keep, LANES), lambda bi, hp: (bi, 0, hp))),
        scratch_shapes=[pltpu.VMEM((seq, LANES), F32),
                        pltpu.VMEM((PAD_A + seq, LANES), F32),
                        pltpu.VMEM((PAD_A + seq, LANES), F32),
                        pltpu.VMEM((seq, LANES), F32),
                        pltpu.VMEM((seq, LANES), F32),
                        pltpu.VMEM((seq, LANES), F32)],
        compiler_params=_cparams(("arbitrary", "arbitrary")),
        name="attn_prompt",
    )(proj, proj, proj, proj, gq2, gk2, bias_p)


def _attn_sample_kernel(q_ref, k_ref, v_ref, z_ref, ck_ref, cv_ref, gq_ref, gk_ref, bias_ref,
                        y_ref, nk_ref, nv_ref, kall_s, vall_s, *, n_past, t_new):
    lane = lax.broadcasted_iota(jnp.int32, (t_new, LANES), 1)
    left = lane < HEAD_DIM_A
    kn = _head_norm(k_ref[...], gk_ref[...], left)
    q = _head_norm(q_ref[...], gq_ref[...], left) * (HEAD_DIM_A ** -0.5)
    v_new = v_ref[...]
    nk_ref[...] = kn
    nv_ref[...] = v_new
    tail = SAMPLE_KEYS - n_past
    kall_s[0:n_past, :] = ck_ref[...]
    vall_s[0:n_past, :] = cv_ref[...]
    kall_s[n_past:, :] = jnp.zeros((tail, LANES), F32)
    vall_s[n_past:, :] = jnp.zeros((tail, LANES), F32)
    kall_s[n_past:n_past + t_new, :] = kn
    vall_s[n_past:n_past + t_new, :] = v_new
    k_all = kall_s[...].astype(BF16)
    v_all = vall_s[...].astype(BF16)
    outs, dens = [], []
    for h in range(HPAIR):
        sel = left if h == 0 else jnp.logical_not(left)
        s = _bdot_nt(jnp.where(sel, q, 0.0), k_all)
        sp = [s + bias_ref[h, p] for p in range(N_PAT)]
        mp = [jnp.max(x, axis=-1, keepdims=True) for x in sp]
        m = functools.reduce(jnp.maximum, mp)
        e = functools.reduce(lambda a, b: a + b, [jnp.exp(x - m) for x in sp])
        dens.append(jnp.sum(e, axis=-1, keepdims=True))
        outs.append(jnp.dot(e.astype(BF16), v_all, preferred_element_type=F32))
    o = jnp.where(left, outs[0], outs[1]) / jnp.where(left, dens[0], dens[1])
    y_ref[...] = (o * _silu(z_ref[...])).astype(BF16)


def _attn_sample(proj_s, cache_k, cache_v, gq2, gk2, bias_s, layer):
    bd, t_new, _ = proj_s.shape
    n_past = cache_k.shape[2]
    col = lambda base: pl.BlockSpec((None, t_new, LANES), lambda bi, hp: (bi, 0, base + hp))
    vec = pl.BlockSpec((1, LANES), lambda bi, hp: (0, 0))
    cache = pl.BlockSpec((None, None, n_past, LANES), lambda bi, hp: (layer, bi, 0, hp))
    out = pl.BlockSpec((None, t_new, LANES), lambda bi, hp: (bi, 0, hp))
    return pl.pallas_call(
        functools.partial(_attn_sample_kernel, n_past=n_past, t_new=t_new),
        out_shape=(jax.ShapeDtypeStruct((bd, t_new, D_A), BF16),
                   jax.ShapeDtypeStruct((bd, t_new, D_A), F32),
                   jax.ShapeDtypeStruct((bd, t_new, D_A), F32)),
        grid=(bd, H_A // HPAIR),
        in_specs=[col(COL_Q), col(COL_K), col(COL_V), col(COL_Z), cache, cache, vec, vec,
                  pl.BlockSpec((HPAIR, N_PAT, t_new, SAMPLE_KEYS), lambda bi, hp: (hp, 0, 0, 0))],
        out_specs=(out, out, out),
        scratch_shapes=[pltpu.VMEM((SAMPLE_KEYS, LANES), F32), pltpu.VMEM((SAMPLE_KEYS, LANES), F32)],
        compiler_params=_cparams(("arbitrary", "arbitrary")),
        name="attn_sample",
    )(proj_s, proj_s, proj_s, proj_s, cache_k, cache_v, gq2, gk2, bias_s)


def _sgu_prompt_kernel(u_ref, v_ref, z_ref, g_ref, w_ref, bt_ref, y_ref, *, chunks):
    row = lax.broadcasted_iota(jnp.int32, (CHUNK_B, CHUNK_B), 0)
    colm = lax.broadcasted_iota(jnp.int32, (CHUNK_B, CHUNK_B), 1)
    tril = row >= colm
    ws = [jnp.where(tril, w_ref[g], 0.0).astype(BF16) for g in range(G_B)]
    for c in range(chunks):
        rows = slice(c * CHUNK_B, (c + 1) * CHUNK_B)
        v = v_ref[rows, :]
        ms = jnp.sum(v * v, axis=-1, keepdims=True) * (1.0 / D_B)
        vn = (v * lax.rsqrt(ms + EPS) * g_ref[...]).astype(BF16)
        for g in range(G_B):
            cols = slice(g * C_B, (g + 1) * C_B)
            mix = jnp.dot(ws[g], vn[:, cols], preferred_element_type=F32) + bt_ref[:, g:g + 1]
            y_ref[rows, cols] = (u_ref[rows, cols] * mix * _silu(z_ref[rows, cols])).astype(BF16)


def _sgu_prompt(proj, sgu_g, sgu_w, sgu_bt):
    b, seq, _ = proj.shape
    ts = 512
    col = lambda base: pl.BlockSpec((None, ts, D_B), lambda bi, i: (bi, i, base))
    return pl.pallas_call(
        functools.partial(_sgu_prompt_kernel, chunks=ts // CHUNK_B),
        out_shape=jax.ShapeDtypeStruct((b, seq, D_B), BF16),
        grid=(b, seq // ts),
        in_specs=[col(COL_UB), col(COL_VB), col(COL_ZB),
                  pl.BlockSpec((1, D_B), lambda bi, i: (0, 0)),
                  pl.BlockSpec((G_B, CHUNK_B, CHUNK_B), lambda bi, i: (0, 0, 0)),
                  pl.BlockSpec((CHUNK_B, G_B), lambda bi, i: (0, 0))],
        out_specs=pl.BlockSpec((None, ts, D_B), lambda bi, i: (bi, i, 0)),
        compiler_params=_cparams(("arbitrary", "arbitrary")),
        name="sgu_prompt",
    )(proj, proj, proj, sgu_g, sgu_w, sgu_bt)


def _sgu_sample_kernel(u_ref, v_ref, z_ref, g_ref, wexp_ref, bexp_ref, y_ref, vn_ref, *, t_new):
    v = v_ref[...]
    ms = jnp.sum(v * v, axis=-1, keepdims=True) * (1.0 / D_B)
    vn = v * lax.rsqrt(ms + EPS) * g_ref[...]
    vn_ref[...] = vn
    mix = bexp_ref[...]
    for s in range(t_new):
        mix = mix + wexp_ref[s] * vn[s:s + 1, :]
    y_ref[...] = (u_ref[...] * mix * _silu(z_ref[...])).astype(BF16)


def _sgu_sample(proj_s, sgu_g, wexp, bexp):
    bd, t_new, _ = proj_s.shape
    col = lambda base: pl.BlockSpec((None, t_new, D_B), lambda bi: (bi, 0, base))
    return pl.pallas_call(
        functools.partial(_sgu_sample_kernel, t_new=t_new),
        out_shape=(jax.ShapeDtypeStruct((bd, t_new, D_B), BF16),
                   jax.ShapeDtypeStruct((bd, t_new, D_B), F32)),
        grid=(bd,),
        in_specs=[col(COL_UB), col(COL_VB), col(COL_ZB),
                  pl.BlockSpec((1, D_B), lambda bi: (0, 0)),
                  pl.BlockSpec((t_new, t_new, D_B), lambda bi: (0, 0, 0)),
                  pl.BlockSpec((t_new, D_B), lambda bi: (0, 0))],
        out_specs=(pl.BlockSpec((None, t_new, D_B), lambda bi: (bi, 0, 0)),
                   pl.BlockSpec((None, t_new, D_B), lambda bi: (bi, 0, 0))),
        compiler_params=_cparams(("arbitrary",)),
        name="sgu_sample",
    )(proj_s, proj_s, proj_s, sgu_g, wexp, bexp)


def _log_sigmoid(x):
    return jnp.minimum(x, 0.0) - jnp.log1p(jnp.exp(-jnp.abs(x)))


def _mlstm_kernel(qk_ref, v_ref, o_ref, z_ref, gt_ref, cbuf_ref, c0_ref, n0_ref, m0_ref,
                  cw_ref, cb_ref, gb_ref, hn_ref,
                  y_ref, conv_out_ref, c_out_ref, n_out_ref, m_out_ref,
                  xp_s, vp_s, gp_s, c_s, n_s, m_s, *, t_rows):
    L = CHUNK_C
    ci = pl.program_id(1)
    last = pl.num_programs(1) - 1
    base = SUBLANES
    carry_rows = CONV_W - 1

    @pl.when(ci == 0)
    def _():
        xp_s[...] = jnp.zeros(xp_s.shape, F32)
        vp_s[...] = jnp.zeros(vp_s.shape, F32)
        gp_s[...] = jnp.zeros(gp_s.shape, F32)
        xp_s[base - carry_rows:base, :] = cbuf_ref[...]
        c_s[...] = c0_ref[...]
        n_s[...] = n0_ref[...]
        m_s[...] = m0_ref[...]

    xp_s[base:base + t_rows, :] = qk_ref[...]
    vp_s[0:t_rows, :] = v_ref[...]
    gp_s[0:t_rows, :] = gt_ref[...]

    acc = cb_ref[...] + cw_ref[CONV_W - 1:CONV_W, :] * xp_s[base:base + L, :]
    for j in range(CONV_W - 1):
        off = base - carry_rows + j
        acc = acc + cw_ref[j:j + 1, :] * xp_s[off:off + L, :]
    qk = _silu(acc)
    new_carry = xp_s[base + t_rows - carry_rows:base + t_rows, :]
    xp_s[base - carry_rows:base, :] = new_carry

    gates = gp_s[...] + gb_ref[...]
    row_l = lax.broadcasted_iota(jnp.int32, (L, LANES), 0)
    lane_l = lax.broadcasted_iota(jnp.int32, (L, LANES), 1)
    real = row_l < t_rows
    is_f = (lane_l >= H_C) & (lane_l < 2 * H_C)
    logf = jnp.where(real & is_f, _log_sigmoid(gates), 0.0)
    ig = jnp.where(real, gates, NEG)
    tri = (row_l >= lane_l).astype(BF16)
    f1 = logf.astype(BF16)
    r1 = logf - f1.astype(F32)
    f2 = r1.astype(BF16)
    f3 = (r1 - f2.astype(F32)).astype(BF16)
    cum = (jnp.dot(tri, f1, preferred_element_type=F32) + jnp.dot(tri, f2, preferred_element_type=F32)
           + jnp.dot(tri, f3, preferred_element_type=F32))
    cum_t = cum.T
    ig_t = ig.T
    causal = row_l >= lane_l
    v_all = vp_s[...]
    o_all = o_ref[...]
    z_all = z_ref[...]

    for h in range(H_C):
        cols = slice(h * DH_C, (h + 1) * DH_C)
        q = qk[:, cols]
        k = qk[:, D_C + h * DH_C:D_C + (h + 1) * DH_C] * (DH_C ** -0.5)
        v = v_all[:, cols]
        bh_col = cum[:, H_C + h:H_C + h + 1]
        ih_col = ig[:, h:h + 1]
        bh_row = cum_t[H_C + h:H_C + h + 1, :]
        ih_row = ig_t[h:h + 1, :]
        m_prev = m_s[h:h + 1, 0:1]
        c_prev = c_s[h]
        n_prev = n_s[h:h + 1, :]

        dlog = jnp.where(causal, bh_col - bh_row + ih_row, NEG)
        inter = bh_col + m_prev
        mt = jnp.maximum(inter, jnp.max(dlog, axis=-1, keepdims=True))
        a = jnp.exp(dlog - mt) * _bdot_nt(q, k)
        w_inter = jnp.exp(inter - mt)
        num = _bdot(a, v) + w_inter * _bdot(q, c_prev)
        den = jnp.sum(a, axis=-1, keepdims=True) + w_inter * jnp.sum(q * n_prev, axis=-1, keepdims=True)
        denom = jnp.maximum(jnp.abs(den), jnp.exp(-mt))
        hh = num / denom

        bl = cum[L - 1:L, H_C + h:H_C + h + 1]
        g_col = bl - bh_col + ih_col
        m_new = jnp.maximum(bl + m_prev, jnp.max(g_col, axis=0, keepdims=True))
        ws_col = jnp.exp(g_col - m_new)
        wc = jnp.exp(bl + m_prev - m_new)
        kw = k * ws_col
        c_s[h] = wc * c_prev + _bdot_tn(kw, v)
        n_s[h:h + 1, :] = wc * n_prev + jnp.sum(kw, axis=0, keepdims=True)
        m_s[h:h + 1, :] = jnp.broadcast_to(m_new, (1, LANES))

        ms = jnp.sum(hh * hh, axis=-1, keepdims=True) * (1.0 / DH_C)
        hn = hh * lax.rsqrt(ms + EPS) * hn_ref[:, cols]
        yv = hn[0:t_rows, :] * jax.nn.sigmoid(o_all[:, cols]) * _silu(z_all[:, cols])
        y_ref[:, cols] = yv.astype(BF16)

    @pl.when(ci == last)
    def _():
        conv_out_ref[...] = new_carry
        c_out_ref[...] = c_s[...]
        n_out_ref[...] = n_s[...]
        m_out_ref[...] = m_s[...]


def _mlstm(proj, gates, conv_buf, c0, n0, m0, conv_w, conv_b, gate_bias, hn_g, t_rows):
    b, seq, _ = proj.shape
    nchunks = seq // t_rows
    hp = SUBLANES
    col = lambda base, width: pl.BlockSpec((None, t_rows, width), lambda bi, ci: (bi, ci, base))
    const2 = lambda shape: pl.BlockSpec(shape, lambda bi, ci: (0, 0))
    per_b3 = lambda shape: pl.BlockSpec((None,) + shape, lambda bi, ci: (bi, 0, 0))
    per_b4 = lambda shape: pl.BlockSpec((None,) + shape, lambda bi, ci: (bi, 0, 0, 0))
    return pl.pallas_call(
        functools.partial(_mlstm_kernel, t_rows=t_rows),
        out_shape=(jax.ShapeDtypeStruct((b, seq, D_C), BF16),
                   jax.ShapeDtypeStruct((b, CONV_W - 1, 2 * D_C), F32),
                   jax.ShapeDtypeStruct((b, H_C, DH_C, DH_C), F32),
                   jax.ShapeDtypeStruct((b, hp, DH_C), F32),
                   jax.ShapeDtypeStruct((b, hp, LANES), F32)),
        grid=(b, nchunks),
        in_specs=[col(COL_QK, 2 * D_C), col(COL_VC, D_C), col(COL_OC, D_C), col(COL_ZC, D_C),
                  pl.BlockSpec((None, t_rows, LANES), lambda bi, ci: (bi, ci, 0)),
                  per_b3((CONV_W - 1, 2 * D_C)), per_b4((H_C, DH_C, DH_C)), per_b3((hp, DH_C)),
                  per_b3((hp, LANES)),
                  const2((CONV_W, 2 * D_C)), const2((1, 2 * D_C)), const2((1, LANES)), const2((1, D_C))],
        out_specs=(pl.BlockSpec((None, t_rows, D_C), lambda bi, ci: (bi, ci, 0)),
                   per_b3((CONV_W - 1, 2 * D_C)), per_b4((H_C, DH_C, DH_C)), per_b3((hp, DH_C)),
                   per_b3((hp, LANES))),
        scratch_shapes=[pltpu.VMEM((SUBLANES + CHUNK_C, 2 * D_C), F32),
                        pltpu.VMEM((CHUNK_C, D_C), F32),
                        pltpu.VMEM((CHUNK_C, LANES), F32),
                        pltpu.VMEM((H_C, DH_C, DH_C), F32),
                        pltpu.VMEM((hp, DH_C), F32),
                        pltpu.VMEM((hp, LANES), F32)],
        compiler_params=_cparams(("arbitrary", "arbitrary")),
        name="mlstm",
    )(proj, proj, proj, proj, gates, conv_buf, c0, n0, m0, conv_w, conv_b, gate_bias, hn_g)


def _pad_heads(a):
    pad = [(0, 0)] * a.ndim
    pad[1] = (0, SUBLANES - a.shape[1])
    return jnp.pad(a, pad)


def _mixers(proj, gates, lw, attn_fn, sgu_fn, conv_buf, c0, n0, m0, t_rows):
    ya, nk, nv = attn_fn(proj)
    sgu_out = sgu_fn(proj)
    m0b = jnp.broadcast_to(_pad_heads(m0)[:, :, None], (m0.shape[0], SUBLANES, LANES))
    yc, nconv, c_new, n_new, m_new = _mlstm(proj, gates, conv_buf, c0, _pad_heads(n0), m0b,
                                            lw["conv_w"], lw["conv_b"], lw["gate_bias"], lw["hn_g"], t_rows)
    return ya, sgu_out, yc, nk, nv, nconv, c_new, n_new[:, :H_C], m_new[:, :H_C, 0]


def kernel(x_prompt, x_sample, c_prompt, c_sample, cache_k_win, cache_v_win, state_conv, state_C, state_n, state_m, rel_bias, norm_g, ada_w, ada_b, w_in, qn_g, kn_g, sgu_g, sgu_w, sgu_b, conv_w, conv_b, f_bias, i_bias, hn_g, w_out):
    depth = w_in.shape[0]
    bp, seq, _ = x_prompt.shape
    bd, t_new, _ = x_sample.shape
    n_past = cache_k_win.shape[2]
    assert t_new <= SUBLANES and n_past + t_new <= SAMPLE_KEYS and seq % 1024 == 0

    w_in_bf = w_in.astype(BF16)
    w_gate_bf = jnp.pad(w_in[:, :, D_MAIN:], ((0, 0), (0, 0), (0, LANES - 2 * H_C))).astype(BF16)
    w_out_bf = w_out.astype(BF16)

    rel_t = rel_bias.T
    bias_p = _expand_bias(rel_t, jnp.asarray(_prompt_bucket_table()), 64)
    bias_p = bias_p.reshape(H_A, N_PAT, 2, BLK_A, 2 * BLK_A)
    bias_s = _expand_bias(rel_t, jnp.asarray(_sample_bucket_table(n_past, t_new)), N_PAT * t_new)
    bias_s = bias_s.reshape(H_A, N_PAT, t_new, SAMPLE_KEYS)

    n_c = bp + bd
    c_all = jnp.pad(jnp.concatenate([c_prompt, c_sample], axis=0), ((0, 2 * SUBLANES - n_c), (0, 0)))
    mod = _ada_mod(c_all, ada_w, ada_b)

    cache_k = cache_k_win.reshape(depth, bd, n_past, D_A)
    cache_v = cache_v_win.reshape(depth, bd, n_past, D_A)
    zeros_conv = jnp.zeros((bp, CONV_W - 1, 2 * D_C), F32)
    zeros_c = jnp.zeros((bp, H_C, DH_C, DH_C), F32)
    zeros_n = jnp.zeros((bp, H_C, DH_C), F32)
    zeros_m = jnp.zeros((bp, H_C), F32)
    tril8 = jnp.tril(jnp.ones((t_new, t_new), F32))

    xp = x_prompt.reshape(bp * seq, D_MODEL)
    xs = x_sample.reshape(bd * t_new, D_MODEL)
    outs = [[] for _ in range(13)]
    for l in range(depth):
        shift, scale, gate = (mod[l, :, i * D_MODEL:(i + 1) * D_MODEL] for i in range(3))
        p_mod = [a[:bp, None, :] for a in (scale, shift, gate)]
        s_mod = [jnp.repeat(a[bp:n_c], t_new, axis=0)[None] for a in (scale, shift, gate)]
        lw = {
            "conv_w": conv_w[l], "conv_b": conv_b[l][None],
            "gate_bias": jnp.pad(jnp.concatenate([i_bias[l], f_bias[l]]), (0, LANES - 2 * H_C))[None],
            "hn_g": hn_g[l][None],
        }
        gq2 = jnp.tile(qn_g[l], HPAIR)[None]
        gk2 = jnp.tile(kn_g[l], HPAIR)[None]
        ng = norm_g[l][None]
        sg = sgu_g[l][None]

        proj, gates = _inproj(xp, p_mod[0], p_mod[1], ng, w_in_bf, w_gate_bf, l, 1024)
        proj = proj.reshape(bp, seq, D_MAIN)
        gates = gates.reshape(bp, seq, LANES)
        ya, yb, yc, nk, nv, ncv, nc_, nn_, nm = _mixers(
            proj, gates, lw,
            lambda pr: _attn_prompt(pr, gq2, gk2, bias_p),
            lambda pr: _sgu_prompt(pr, sg, sgu_w[l], sgu_b[l].T),
            zeros_conv, zeros_c, zeros_n, zeros_m, CHUNK_C)
        xp = _outproj(ya.reshape(bp * seq, D_A), yb.reshape(bp * seq, D_B), yc.reshape(bp * seq, D_C),
                      xp, p_mod[2], w_out_bf, l, 512)
        keep = nk.shape[1]
        for i, a in enumerate((nk.reshape(bp, keep, H_A, HEAD_DIM_A), nv.reshape(bp, keep, H_A, HEAD_DIM_A),
                               ncv, nc_, nn_, nm)):
            outs[i].append(a)

        proj_s, gates_s = _inproj(xs, s_mod[0], s_mod[1], ng, w_in_bf, w_gate_bf, l, bd * t_new)
        proj_s = proj_s.reshape(bd, t_new, D_MAIN)
        gates_s = gates_s.reshape(bd, t_new, LANES)
        w8 = sgu_w[l][:, :t_new, :t_new] * tril8
        wexp = jnp.repeat(jnp.transpose(w8, (2, 1, 0)), C_B, axis=2)
        bexp = jnp.repeat(sgu_b[l][:, :t_new].T, C_B, axis=1)
        ya, sgu_out, yc, nk, nv, ncv, nc_, nn_, nm = _mixers(
            proj_s, gates_s, lw,
            lambda pr: _attn_sample(pr, cache_k, cache_v, gq2, gk2, bias_s, l),
            lambda pr: _sgu_sample(pr, sg, wexp, bexp),
            state_conv[l], state_C[l], state_n[l], state_m[l], t_new)
        yb, vn = sgu_out
        xs = _outproj(ya.reshape(bd * t_new, D_A), yb.reshape(bd * t_new, D_B), yc.reshape(bd * t_new, D_C),
                      xs, s_mod[2], w_out_bf, l, bd * t_new)
        for i, a in enumerate((nk.reshape(bd, t_new, H_A, HEAD_DIM_A), nv.reshape(bd, t_new, H_A, HEAD_DIM_A),
                               vn, ncv, nc_, nn_, nm)):
            outs[6 + i].append(a)

    stacked = [jnp.stack(o) for o in outs]
    return (xp.reshape(bp, seq, D_MODEL), xs.reshape(bd, t_new, D_MODEL), *stacked)
```

```python
import functools
import math

import numpy as np
import jax
import jax.numpy as jnp
from jax import lax
from jax.experimental import pallas as pl
from jax.experimental.pallas import tpu as pltpu

F32 = jnp.float32
BF16 = jnp.bfloat16

D_MODEL = 2048
HEAD_DIM_A = 64
D_A = 768
H_A = 12
D_B = 512
G_B = 4
C_B = 128
CHUNK_B = 128
D_C = 768
DH_C = 128
H_C = 6
CHUNK_C = 128
CONV_W = 4
PATTERNS = ((128, 1), (512, 4), (2048, 16))
N_PAT = len(PATTERNS)
WIN_MAX = 2048
BLK_A = 128
N_BUCKETS = 32
MAX_DIST = 2048
EPS = 1e-6
D_MAIN = 4 * D_A + 3 * D_B + 2 * D_C + 3 * D_C
D_IN = D_MAIN + 2 * H_C

LANES = 128
SUBLANES = 8
VMEM_LIMIT = 56 * 1024 * 1024

NEG = -1e30
LOG2E = 1.4426950408889634
PAD_A = BLK_A * PATTERNS[-1][1]
HPAIR = LANES // HEAD_DIM_A
SAMPLE_KEYS = 2176

COL_Q, COL_K, COL_V, COL_Z = 0, D_A // LANES, 2 * D_A // LANES, 3 * D_A // LANES
COL_UB, COL_VB, COL_ZB = 4 * D_A // D_B, 4 * D_A // D_B + 1, 4 * D_A // D_B + 2
COL_QK = (4 * D_A + 3 * D_B) // (2 * D_C)
COL_VC = (4 * D_A + 3 * D_B + 2 * D_C) // D_C
COL_OC, COL_ZC = COL_VC + 1, COL_VC + 2


def _cparams(sem):
    return pltpu.CompilerParams(dimension_semantics=sem, vmem_limit_bytes=VMEM_LIMIT)


def _silu(x):
    return x * jax.nn.sigmoid(x)


def _bdot(a, b):
    return jnp.dot(a.astype(BF16), b.astype(BF16), preferred_element_type=F32)


def _bdot_nt(a, b):
    return lax.dot_general(a.astype(BF16), b.astype(BF16), (((1,), (1,)), ((), ())),
                           preferred_element_type=F32)


def _bdot_tn(a, b):
    return lax.dot_general(a.astype(BF16), b.astype(BF16), (((0,), (0,)), ((), ())),
                           preferred_element_type=F32)


def _bucket_np(dist):
    max_exact = N_BUCKETS // 2
    df = np.maximum(dist, 1).astype(np.float32)
    large = max_exact + (np.log(df / np.float32(max_exact)) / np.float32(math.log(MAX_DIST / max_exact))
                         * np.float32(N_BUCKETS - max_exact)).astype(np.int32)
    return np.where(dist < max_exact, dist, np.minimum(large, N_BUCKETS - 1)).astype(np.int32)


def _prompt_bucket_table():
    qi = np.arange(BLK_A)[:, None]
    ki = np.arange(2 * BLK_A)[None, :]
    j = qi + BLK_A - ki
    out = []
    for win, dil in PATTERNS:
        n_back = win // dil
        band = (j >= 0) & (j <= n_back)
        b = _bucket_np(np.clip(j, 0, n_back) * dil)
        out.append(np.where(band, b, -1))
        out.append(np.where(band & (ki >= BLK_A), b, -1))
    return np.stack(out).reshape(N_PAT * 2 * BLK_A, 2 * BLK_A).astype(np.int32)


def _sample_bucket_table(n_past, t_new):
    c = np.arange(SAMPLE_KEYS)[None, :]
    t = np.arange(t_new)[:, None]
    delta = n_past + t - c
    out = []
    for win, dil in PATTERNS:
        valid = (c < n_past + t_new) & (delta >= 0) & (delta % dil == 0) & (delta // dil <= win // dil)
        out.append(np.where(valid, _bucket_np(np.maximum(delta, 0)), -1))
    return np.stack(out).reshape(N_PAT * t_new, SAMPLE_KEYS).astype(np.int32)


def _bias_kernel(rb_ref, idx_ref, out_ref, *, scale):
    h = pl.program_id(0)
    idx = idx_ref[...]
    out = jnp.full(idx.shape, NEG, F32)
    for b in range(N_BUCKETS):
        out = jnp.where(idx == b, rb_ref[h, b] * scale, out)
    out_ref[...] = out


def _expand_bias(rel_bias_t, idx, row_block, scale):
    rows, cols = idx.shape
    return pl.pallas_call(
        functools.partial(_bias_kernel, scale=scale),
        out_shape=jax.ShapeDtypeStruct((H_A, rows, cols), F32),
        grid=(H_A, rows // row_block),
        in_specs=[pl.BlockSpec(memory_space=pltpu.SMEM),
                  pl.BlockSpec((row_block, cols), lambda h, r: (r, 0))],
        out_specs=pl.BlockSpec((None, row_block, cols), lambda h, r: (h, r, 0)),
        compiler_params=_cparams(("arbitrary", "arbitrary")),
        name="bias_expand",
    )(rel_bias_t, idx)


def _ada_kernel(c_ref, w_ref, b_ref, o_ref):
    c = c_ref[...]
    a = _silu(c)
    w = w_ref[...]
    a_hi = a.astype(BF16)
    a_lo = (a - a_hi.astype(F32)).astype(BF16)
    w_hi = w.astype(BF16)
    w_lo = (w - w_hi.astype(F32)).astype(BF16)
    acc = jnp.dot(a_hi, w_hi, preferred_element_type=F32)
    acc += jnp.dot(a_hi, w_lo, preferred_element_type=F32)
    acc += jnp.dot(a_lo, w_hi, preferred_element_type=F32)
    o_ref[...] = acc + b_ref[...]


def _ada_mod(c_all, ada_w, ada_b):
    depth = ada_w.shape[0]
    rows = c_all.shape[0]
    tn = 768
    n = 3 * D_MODEL
    return pl.pallas_call(
        _ada_kernel,
        out_shape=jax.ShapeDtypeStruct((depth, rows, n), F32),
        grid=(depth, n // tn),
        in_specs=[pl.BlockSpec((rows, D_MODEL), lambda l, j: (0, 0)),
                  pl.BlockSpec((None, D_MODEL, tn), lambda l, j: (l, 0, j)),
                  pl.BlockSpec((None, 1, tn), lambda l, j: (l, 0, j))],
        out_specs=pl.BlockSpec((None, rows, tn), lambda l, j: (l, 0, j)),
        compiler_params=_cparams(("arbitrary", "arbitrary")),
        name="ada_mod",
    )(c_all, ada_w, ada_b.reshape(depth, 1, n))


def _inproj_kernel(x_ref, sc_ref, sh_ref, g_ref, w_ref, wg_ref, proj_ref, gates_ref, h_scr, *, row_chunk):
    @pl.when(pl.program_id(1) == 0)
    def _():
        def chunk(c, carry):
            rows = pl.ds(pl.multiple_of(c * row_chunk, row_chunk), row_chunk)
            x = x_ref[rows, :]
            ms = jnp.sum(x * x, axis=-1, keepdims=True) * (1.0 / D_MODEL)
            y = x * lax.rsqrt(ms + EPS) * g_ref[...]
            sc = sc_ref[...] if sc_ref.shape[0] == 1 else sc_ref[rows, :]
            sh = sh_ref[...] if sh_ref.shape[0] == 1 else sh_ref[rows, :]
            h_scr[rows, :] = (y * (1.0 + sc) + sh).astype(BF16)
            return carry
        lax.fori_loop(0, x_ref.shape[0] // row_chunk, chunk, 0)
        gates_ref[...] = jnp.dot(h_scr[...], wg_ref[...], preferred_element_type=F32)

    proj_ref[...] = jnp.dot(h_scr[...], w_ref[...].astype(BF16), preferred_element_type=F32)


def _inproj(x2d, scale, shift, norm_g, w_in, w_gate_bf, layer, tm):
    m = x2d.shape[0]
    groups = scale.shape[0]
    tiles_per_group = m // tm // groups
    tn = 768
    mod_spec = pl.BlockSpec((None, scale.shape[1], D_MODEL), lambda i, j: (i // tiles_per_group, 0, 0))
    return pl.pallas_call(
        functools.partial(_inproj_kernel, row_chunk=min(tm, 256)),
        out_shape=(jax.ShapeDtypeStruct((m, D_MAIN), F32), jax.ShapeDtypeStruct((m, LANES), F32)),
        grid=(m // tm, D_MAIN // tn),
        in_specs=[pl.BlockSpec((tm, D_MODEL), lambda i, j: (i, 0)),
                  mod_spec, mod_spec,
                  pl.BlockSpec((1, D_MODEL), lambda i, j: (0, 0)),
                  pl.BlockSpec((None, D_MODEL, tn), lambda i, j: (layer, 0, j)),
                  pl.BlockSpec((None, D_MODEL, LANES), lambda i, j: (layer, 0, 0))],
        out_specs=(pl.BlockSpec((tm, tn), lambda i, j: (i, j)),
                   pl.BlockSpec((tm, LANES), lambda i, j: (i, 0))),
        scratch_shapes=[pltpu.VMEM((tm, D_MODEL), BF16)],
        compiler_params=_cparams(("arbitrary", "arbitrary")),
        name="inproj",
    )(x2d, scale, shift, norm_g, w_in, w_gate_bf)


def _outproj_kernel(ya_ref, yb_ref, yc_ref, x_ref, gate_ref, w_ref, o_ref):
    y = jnp.dot(ya_ref[...], w_ref[0:D_A, :], preferred_element_type=F32)
    y += jnp.dot(yb_ref[...], w_ref[D_A:D_A + D_B, :], preferred_element_type=F32)
    y += jnp.dot(yc_ref[...], w_ref[D_A + D_B:, :], preferred_element_type=F32)
    o_ref[...] = x_ref[...] + gate_ref[...] * y


def _outproj(ya, yb, yc, x2d, gate, w_out_bf, layer, tm):
    m = x2d.shape[0]
    groups = gate.shape[0]
    tiles_per_group = m // tm // groups
    return pl.pallas_call(
        _outproj_kernel,
        out_shape=jax.ShapeDtypeStruct((m, D_MODEL), F32),
        grid=(m // tm,),
        in_specs=[pl.BlockSpec((tm, D_A), lambda i: (i, 0)),
                  pl.BlockSpec((tm, D_B), lambda i: (i, 0)),
                  pl.BlockSpec((tm, D_C), lambda i: (i, 0)),
                  pl.BlockSpec((tm, D_MODEL), lambda i: (i, 0)),
                  pl.BlockSpec((None, gate.shape[1], D_MODEL), lambda i: (i // tiles_per_group, 0, 0)),
                  pl.BlockSpec((None, D_MODEL, D_MODEL), lambda i: (layer, 0, 0))],
        out_specs=pl.BlockSpec((tm, D_MODEL), lambda i: (i, 0)),
        compiler_params=_cparams(("arbitrary",)),
        name="outproj",
    )(ya, yb, yc, x2d, gate, w_out_bf)


def _head_norm(x, g, left):
    x2 = x * x
    s_left = jnp.sum(jnp.where(left, x2, 0.0), axis=-1, keepdims=True)
    s_right = jnp.sum(jnp.where(left, 0.0, x2), axis=-1, keepdims=True)
    ms = jnp.where(left, s_left, s_right) * (1.0 / HEAD_DIM_A)
    return x * lax.rsqrt(ms + EPS) * g


def _attn_prompt_kernel(q_ref, k_ref, v_ref, z_ref, gq_ref, gk_ref, bias_ref,
                        y_ref, pk_ref, pv_ref,
                        qn_s, kp_s, vp_s, ones_s, o_s, l_s, m_s, *, seq, keep, unroll):
    norm_rows = 512
    lane_n = lax.broadcasted_iota(jnp.int32, (norm_rows, LANES), 1)
    left_n = lane_n < HEAD_DIM_A

    kp_s[0:PAD_A, :] = jnp.zeros((PAD_A, LANES), F32)
    vp_s[0:PAD_A, :] = jnp.zeros((PAD_A, LANES), F32)
    row_o = lax.broadcasted_iota(jnp.int32, ones_s.shape, 0)
    lane_o = lax.broadcasted_iota(jnp.int32, ones_s.shape, 1)
    ones_s[...] = jnp.where((row_o < 2 * BLK_A) == (lane_o < HEAD_DIM_A), 1.0, 0.0).astype(BF16)

    def norm_chunk(c, carry):
        r0 = pl.multiple_of(c * norm_rows, norm_rows)
        rows = pl.ds(r0, norm_rows)
        qn_s[rows, :] = _head_norm(q_ref[rows, :], gq_ref[...], left_n) * (HEAD_DIM_A ** -0.5 * LOG2E)
        kp_s[pl.ds(PAD_A + r0, norm_rows), :] = _head_norm(k_ref[rows, :], gk_ref[...], left_n)
        vp_s[pl.ds(PAD_A + r0, norm_rows), :] = v_ref[rows, :]
        return carry
    lax.fori_loop(0, seq // norm_rows, norm_chunk, 0)

    pk_ref[...] = kp_s[PAD_A + seq - keep:PAD_A + seq, :]
    pv_ref[...] = vp_s[PAD_A + seq - keep:PAD_A + seq, :]

    left_b = lax.broadcasted_iota(jnp.int32, (BLK_A, LANES), 1) < HEAD_DIM_A
    left_k = lax.broadcasted_iota(jnp.int32, (2 * BLK_A, LANES), 1) < HEAD_DIM_A

    for p, (win, dil) in enumerate(PATTERNS):
        blocks_per_residue = seq // (dil * BLK_A)

        def block(i, carry, p=p, dil=dil, blocks_per_residue=blocks_per_residue):
            r = i // blocks_per_residue
            n = i % blocks_per_residue
            q_start = r + n * (BLK_A * dil)
            k_start = PAD_A + q_start - BLK_A * dil
            if dil == 1:
                q_rows = pl.ds(pl.multiple_of(q_start, BLK_A), BLK_A)
                k_rows = pl.ds(pl.multiple_of(k_start, BLK_A), 2 * BLK_A)
            else:
                q_rows = pl.ds(q_start, BLK_A, stride=dil)
                k_rows = pl.ds(k_start, 2 * BLK_A, stride=dil)
            first = jnp.where(n == 0, 1, 0)
            q = qn_s[q_rows, :]
            k = kp_s[k_rows, :].astype(BF16)
            v = vp_s[k_rows, :]
            v2 = jnp.concatenate([jnp.where(left_k, v, 0.0), jnp.where(left_k, 0.0, v)], axis=0).astype(BF16)
            rhs = jnp.concatenate([v2, ones_s[...]], axis=1)
            es, ms = [], []
            for h in range(HPAIR):
                qh = jnp.where(left_b, q, 0.0) if h == 0 else jnp.where(left_b, 0.0, q)
                s = _bdot_nt(qh, k) + bias_ref[h, p, first]
                mh = jnp.max(s, axis=-1, keepdims=True)
                es.append(jnp.exp2(s - mh).astype(BF16))
                ms.append(mh)
            res = jnp.dot(jnp.concatenate(es, axis=1), rhs, preferred_element_type=F32)
            o_s[p, q_rows, :] = res[:, :LANES]
            l_s[p, q_rows, :] = res[:, LANES:]
            m_s[p, q_rows, :] = jnp.where(left_b, ms[0], ms[1])
            return carry
        lax.fori_loop(0, seq // BLK_A, block, 0, unroll=unroll)

    def out_chunk(c, carry):
        rows = pl.ds(pl.multiple_of(c * norm_rows, norm_rows), norm_rows)
        m = [m_s[p, rows, :] for p in range(N_PAT)]
        top = functools.reduce(jnp.maximum, m)
        w = [jnp.exp2(x - top) for x in m]
        num = functools.reduce(lambda a, b: a + b, [w[p] * o_s[p, rows, :] for p in range(N_PAT)])
        den = functools.reduce(lambda a, b: a + b, [w[p] * l_s[p, rows, :] for p in range(N_PAT)])
        y_ref[rows, :] = (num / den * _silu(z_ref[rows, :])).astype(BF16)
        return carry
    lax.fori_loop(0, seq // norm_rows, out_chunk, 0)


def _attn_prompt(proj, gq2, gk2, bias_p):
    b, seq, _ = proj.shape
    keep = min(WIN_MAX, seq)
    assert seq % (BLK_A * PATTERNS[-1][1]) == 0
    col = lambda base: pl.BlockSpec((None, seq, LANES), lambda bi, hp: (bi, 0, base + hp))
    vec = pl.BlockSpec((1, LANES), lambda bi, hp: (0, 0))
    return pl.pallas_call(
        functools.partial(_attn_prompt_kernel, seq=seq, keep=keep, unroll=8),
        out_shape=(jax.ShapeDtypeStruct((b, seq, D_A), BF16),
                   jax.ShapeDtypeStruct((b, keep, D_A), F32),
                   jax.ShapeDtypeStruct((b, keep, D_A), F32)),
        grid=(b, H_A // HPAIR),
        in_specs=[col(COL_Q), col(COL_K), col(COL_V), col(COL_Z), vec, vec,
                  pl.BlockSpec((HPAIR, N_PAT, 2, BLK_A, 2 * BLK_A), lambda bi, hp: (hp, 0, 0, 0, 0))],
        out_specs=(pl.BlockSpec((None, seq, LANES), lambda bi, hp: (bi, 0, hp)),
                   pl.BlockSpec((None, keep, LANES), lambda bi, hp: (bi, 0, hp)),
                   pl.BlockSpec((None, keep, LANES), lambda bi, hp: (bi, 0, hp))),
        scratch_shapes=[pltpu.VMEM((seq, LANES), F32),
                        pltpu.VMEM((PAD_A + seq, LANES), F32),
                        pltpu.VMEM((PAD_A + seq, LANES), F32),
                        pltpu.VMEM((2 * HPAIR * BLK_A, LANES), BF16),
                        pltpu.VMEM((N_PAT, seq, LANES), F32),
                        pltpu.VMEM((N_PAT, seq, LANES), F32),
                        pltpu.VMEM((N_PAT, seq, LANES), F32)],
        compiler_params=_cparams(("arbitrary", "arbitrary")),
        name="attn_prompt",
    )(proj, proj, proj, proj, gq2, gk2, bias_p)


def _attn_sample_kernel(q_ref, k_ref, v_ref, z_ref, ck_ref, cv_ref, gq_ref, gk_ref, bias_ref,
                        y_ref, nk_ref, nv_ref, kall_s, vall_s, *, n_past, t_new):
    lane = lax.broadcasted_iota(jnp.int32, (t_new, LANES), 1)
    left = lane < HEAD_DIM_A
    kn = _head_norm(k_ref[...], gk_ref[...], left)
    q = _head_norm(q_ref[...], gq_ref[...], left) * (HEAD_DIM_A ** -0.5)
    v_new = v_ref[...]
    nk_ref[...] = kn
    nv_ref[...] = v_new
    tail = SAMPLE_KEYS - n_past
    kall_s[0:n_past, :] = ck_ref[...]
    vall_s[0:n_past, :] = cv_ref[...]
    kall_s[n_past:, :] = jnp.zeros((tail, LANES), F32)
    vall_s[n_past:, :] = jnp.zeros((tail, LANES), F32)
    kall_s[n_past:n_past + t_new, :] = kn
    vall_s[n_past:n_past + t_new, :] = v_new
    k_all = kall_s[...].astype(BF16)
    v_all = vall_s[...].astype(BF16)
    outs, dens = [], []
    for h in range(HPAIR):
        sel = left if h == 0 else jnp.logical_not(left)
        s = _bdot_nt(jnp.where(sel, q, 0.0), k_all)
        sp = [s + bias_ref[h, p] for p in range(N_PAT)]
        mp = [jnp.max(x, axis=-1, keepdims=True) for x in sp]
        m = functools.reduce(jnp.maximum, mp)
        e = functools.reduce(lambda a, b: a + b, [jnp.exp(x - m) for x in sp])
        dens.append(jnp.sum(e, axis=-1, keepdims=True))
        outs.append(jnp.dot(e.astype(BF16), v_all, preferred_element_type=F32))
    o = jnp.where(left, outs[0], outs[1]) / jnp.where(left, dens[0], dens[1])
    y_ref[...] = (o * _silu(z_ref[...])).astype(BF16)


def _attn_sample(proj_s, cache_k, cache_v, gq2, gk2, bias_s, layer):
    bd, t_new, _ = proj_s.shape
    n_past = cache_k.shape[2]
    col = lambda base: pl.BlockSpec((None, t_new, LANES), lambda bi, hp: (bi, 0, base + hp))
    vec = pl.BlockSpec((1, LANES), lambda bi, hp: (0, 0))
    cache = pl.BlockSpec((None, None, n_past, LANES), lambda bi, hp: (layer, bi, 0, hp))
    out = pl.BlockSpec((None, t_new, LANES), lambda bi, hp: (bi, 0, hp))
    return pl.pallas_call(
        functools.partial(_attn_sample_kernel, n_past=n_past, t_new=t_new),
        out_shape=(jax.ShapeDtypeStruct((bd, t_new, D_A), BF16),
                   jax.ShapeDtypeStruct((bd, t_new, D_A), F32),
                   jax.ShapeDtypeStruct((bd, t_new, D_A), F32)),
        grid=(bd, H_A // HPAIR),
        in_specs=[col(COL_Q), col(COL_K), col(COL_V), col(COL_Z), cache, cache, vec, vec,
                  pl.BlockSpec((HPAIR, N_PAT, t_new, SAMPLE_KEYS), lambda bi, hp: (hp, 0, 0, 0))],
        out_specs=(out, out, out),
        scratch_shapes=[pltpu.VMEM((SAMPLE_KEYS, LANES), F32), pltpu.VMEM((SAMPLE_KEYS, LANES), F32)],
        compiler_params=_cparams(("arbitrary", "arbitrary")),
        name="attn_sample",
    )(proj_s, proj_s, proj_s, proj_s, cache_k, cache_v, gq2, gk2, bias_s)


def _sgu_prompt_kernel(u_ref, v_ref, z_ref, g_ref, w_ref, bt_ref, y_ref, *, chunks):
    row = lax.broadcasted_iota(jnp.int32, (CHUNK_B, CHUNK_B), 0)
    colm = lax.broadcasted_iota(jnp.int32, (CHUNK_B, CHUNK_B), 1)
    tril = row >= colm
    ws = [jnp.where(tril, w_ref[g], 0.0).astype(BF16) for g in range(G_B)]
    for c in range(chunks):
        rows = slice(c * CHUNK_B, (c + 1) * CHUNK_B)
        v = v_ref[rows, :]
        ms = jnp.sum(v * v, axis=-1, keepdims=True) * (1.0 / D_B)
        vn = (v * lax.rsqrt(ms + EPS) * g_ref[...]).astype(BF16)
        for g in range(G_B):
            cols = slice(g * C_B, (g + 1) * C_B)
            mix = jnp.dot(ws[g], vn[:, cols], preferred_element_type=F32) + bt_ref[:, g:g + 1]
            y_ref[rows, cols] = (u_ref[rows, cols] * mix * _silu(z_ref[rows, cols])).astype(BF16)


def _sgu_prompt(proj, sgu_g, sgu_w, sgu_bt):
    b, seq, _ = proj.shape
    ts = 512
    col = lambda base: pl.BlockSpec((None, ts, D_B), lambda bi, i: (bi, i, base))
    return pl.pallas_call(
        functools.partial(_sgu_prompt_kernel, chunks=ts // CHUNK_B),
        out_shape=jax.ShapeDtypeStruct((b, seq, D_B), BF16),
        grid=(b, seq // ts),
        in_specs=[col(COL_UB), col(COL_VB), col(COL_ZB),
                  pl.BlockSpec((1, D_B), lambda bi, i: (0, 0)),
                  pl.BlockSpec((G_B, CHUNK_B, CHUNK_B), lambda bi, i: (0, 0, 0)),
                  pl.BlockSpec((CHUNK_B, G_B), lambda bi, i: (0, 0))],
        out_specs=pl.BlockSpec((None, ts, D_B), lambda bi, i: (bi, i, 0)),
        compiler_params=_cparams(("arbitrary", "arbitrary")),
        name="sgu_prompt",
    )(proj, proj, proj, sgu_g, sgu_w, sgu_bt)


def _sgu_sample_kernel(u_ref, v_ref, z_ref, g_ref, wexp_ref, bexp_ref, y_ref, vn_ref, *, t_new):
    v = v_ref[...]
    ms = jnp.sum(v * v, axis=-1, keepdims=True) * (1.0 / D_B)
    vn = v * lax.rsqrt(ms + EPS) * g_ref[...]
    vn_ref[...] = vn
    mix = bexp_ref[...]
    for s in range(t_new):
        mix = mix + wexp_ref[s] * vn[s:s + 1, :]
    y_ref[...] = (u_ref[...] * mix * _silu(z_ref[...])).astype(BF16)


def _sgu_sample(proj_s, sgu_g, wexp, bexp):
    bd, t_new, _ = proj_s.shape
    col = lambda base: pl.BlockSpec((None, t_new, D_B), lambda bi: (bi, 0, base))
    return pl.pallas_call(
        functools.partial(_sgu_sample_kernel, t_new=t_new),
        out_shape=(jax.ShapeDtypeStruct((bd, t_new, D_B), BF16),
                   jax.ShapeDtypeStruct((bd, t_new, D_B), F32)),
        grid=(bd,),
        in_specs=[col(COL_UB), col(COL_VB), col(COL_ZB),
                  pl.BlockSpec((1, D_B), lambda bi: (0, 0)),
                  pl.BlockSpec((t_new, t_new, D_B), lambda bi: (0, 0, 0)),
                  pl.BlockSpec((t_new, D_B), lambda bi: (0, 0))],
        out_specs=(pl.BlockSpec((None, t_new, D_B), lambda bi: (bi, 0, 0)),
                   pl.BlockSpec((None, t_new, D_B), lambda bi: (bi, 0, 0))),
        compiler_params=_cparams(("arbitrary",)),
        name="sgu_sample",
    )(proj_s, proj_s, proj_s, sgu_g, wexp, bexp)


def _log_sigmoid(x):
    return jnp.minimum(x, 0.0) - jnp.log1p(jnp.exp(-jnp.abs(x)))


def _mlstm_kernel(qk_ref, v_ref, o_ref, z_ref, gt_ref, cbuf_ref, c0_ref, n0_ref, m0_ref,
                  cw_ref, cb_ref, gb_ref, hn_ref,
                  y_ref, conv_out_ref, c_out_ref, n_out_ref, m_out_ref,
                  xp_s, vp_s, gp_s, c_s, n_s, m_s, *, t_rows):
    L = CHUNK_C
    ci = pl.program_id(1)
    last = pl.num_programs(1) - 1
    base = SUBLANES
    carry_rows = CONV_W - 1

    @pl.when(ci == 0)
    def _():
        xp_s[...] = jnp.zeros(xp_s.shape, F32)
        vp_s[...] = jnp.zeros(vp_s.shape, F32)
        gp_s[...] = jnp.zeros(gp_s.shape, F32)
        xp_s[base - carry_rows:base, :] = cbuf_ref[...]
        c_s[...] = c0_ref[...]
        n_s[...] = n0_ref[...]
        m_s[...] = m0_ref[...]

    xp_s[base:base + t_rows, :] = qk_ref[...]
    vp_s[0:t_rows, :] = v_ref[...]
    gp_s[0:t_rows, :] = gt_ref[...]

    acc = cb_ref[...] + cw_ref[CONV_W - 1:CONV_W, :] * xp_s[base:base + L, :]
    for j in range(CONV_W - 1):
        off = base - carry_rows + j
        acc = acc + cw_ref[j:j + 1, :] * xp_s[off:off + L, :]
    qk = _silu(acc)
    new_carry = xp_s[base + t_rows - carry_rows:base + t_rows, :]
    xp_s[base - carry_rows:base, :] = new_carry

    gates = gp_s[...] + gb_ref[...]
    row_l = lax.broadcasted_iota(jnp.int32, (L, LANES), 0)
    lane_l = lax.broadcasted_iota(jnp.int32, (L, LANES), 1)
    real = row_l < t_rows
    is_f = (lane_l >= H_C) & (lane_l < 2 * H_C)
    logf = jnp.where(real & is_f, _log_sigmoid(gates), 0.0)
    ig = jnp.where(real, gates, NEG)
    tri = (row_l >= lane_l).astype(BF16)
    f1 = logf.astype(BF16)
    r1 = logf - f1.astype(F32)
    f2 = r1.astype(BF16)
    f3 = (r1 - f2.astype(F32)).astype(BF16)
    cum = (jnp.dot(tri, f1, preferred_element_type=F32) + jnp.dot(tri, f2, preferred_element_type=F32)
           + jnp.dot(tri, f3, preferred_element_type=F32))
    cum_t = cum.T
    ig_t = ig.T
    causal = row_l >= lane_l
    v_all = vp_s[...]
    o_all = o_ref[...]
    z_all = z_ref[...]

    for h in range(H_C):
        cols = slice(h * DH_C, (h + 1) * DH_C)
        q = qk[:, cols]
        k = qk[:, D_C + h * DH_C:D_C + (h + 1) * DH_C] * (DH_C ** -0.5)
        v = v_all[:, cols]
        bh_col = cum[:, H_C + h:H_C + h + 1]
        ih_col = ig[:, h:h + 1]
        bh_row = cum_t[H_C + h:H_C + h + 1, :]
        ih_row = ig_t[h:h + 1, :]
        m_prev = m_s[h:h + 1, 0:1]
        c_prev = c_s[h]
        n_prev = n_s[h:h + 1, :]

        dlog = jnp.where(causal, bh_col - bh_row + ih_row, NEG)
        inter = bh_col + m_prev
        mt = jnp.maximum(inter, jnp.max(dlog, axis=-1, keepdims=True))
        a = jnp.exp(dlog - mt) * _bdot_nt(q, k)
        w_inter = jnp.exp(inter - mt)
        num = _bdot(a, v) + w_inter * _bdot(q, c_prev)
        den = jnp.sum(a, axis=-1, keepdims=True) + w_inter * jnp.sum(q * n_prev, axis=-1, keepdims=True)
        denom = jnp.maximum(jnp.abs(den), jnp.exp(-mt))
        hh = num / denom

        bl = cum[L - 1:L, H_C + h:H_C + h + 1]
        g_col = bl - bh_col + ih_col
        m_new = jnp.maximum(bl + m_prev, jnp.max(g_col, axis=0, keepdims=True))
        ws_col = jnp.exp(g_col - m_new)
        wc = jnp.exp(bl + m_prev - m_new)
        kw = k * ws_col
        c_s[h] = wc * c_prev + _bdot_tn(kw, v)
        n_s[h:h + 1, :] = wc * n_prev + jnp.sum(kw, axis=0, keepdims=True)
        m_s[h:h + 1, :] = jnp.broadcast_to(m_new, (1, LANES))

        ms = jnp.sum(hh * hh, axis=-1, keepdims=True) * (1.0 / DH_C)
        hn = hh * lax.rsqrt(ms + EPS) * hn_ref[:, cols]
        yv = hn[0:t_rows, :] * jax.nn.sigmoid(o_all[:, cols]) * _silu(z_all[:, cols])
        y_ref[:, cols] = yv.astype(BF16)

    @pl.when(ci == last)
    def _():
        conv_out_ref[...] = new_carry
        c_out_ref[...] = c_s[...]
        n_out_ref[...] = n_s[...]
        m_out_ref[...] = m_s[...]


def _mlstm(proj, gates, conv_buf, c0, n0, m0, conv_w, conv_b, gate_bias, hn_g, t_rows):
    b, seq, _ = proj.shape
    nchunks = seq // t_rows
    hp = SUBLANES
    col = lambda base, width: pl.BlockSpec((None, t_rows, width), lambda bi, ci: (bi, ci, base))
    const2 = lambda shape: pl.BlockSpec(shape, lambda bi, ci: (0, 0))
    per_b3 = lambda shape: pl.BlockSpec((None,) + shape, lambda bi, ci: (bi, 0, 0))
    per_b4 = lambda shape: pl.BlockSpec((None,) + shape, lambda bi, ci: (bi, 0, 0, 0))
    return pl.pallas_call(
        functools.partial(_mlstm_kernel, t_rows=t_rows),
        out_shape=(jax.ShapeDtypeStruct((b, seq, D_C), BF16),
                   jax.ShapeDtypeStruct((b, CONV_W - 1, 2 * D_C), F32),
                   jax.ShapeDtypeStruct((b, H_C, DH_C, DH_C), F32),
                   jax.ShapeDtypeStruct((b, hp, DH_C), F32),
                   jax.ShapeDtypeStruct((b, hp, LANES), F32)),
        grid=(b, nchunks),
        in_specs=[col(COL_QK, 2 * D_C), col(COL_VC, D_C), col(COL_OC, D_C), col(COL_ZC, D_C),
                  pl.BlockSpec((None, t_rows, LANES), lambda bi, ci: (bi, ci, 0)),
                  per_b3((CONV_W - 1, 2 * D_C)), per_b4((H_C, DH_C, DH_C)), per_b3((hp, DH_C)),
                  per_b3((hp, LANES)),
                  const2((CONV_W, 2 * D_C)), const2((1, 2 * D_C)), const2((1, LANES)), const2((1, D_C))],
        out_specs=(pl.BlockSpec((None, t_rows, D_C), lambda bi, ci: (bi, ci, 0)),
                   per_b3((CONV_W - 1, 2 * D_C)), per_b4((H_C, DH_C, DH_C)), per_b3((hp, DH_C)),
                   per_b3((hp, LANES))),
        scratch_shapes=[pltpu.VMEM((SUBLANES + CHUNK_C, 2 * D_C), F32),
                        pltpu.VMEM((CHUNK_C, D_C), F32),
                        pltpu.VMEM((CHUNK_C, LANES), F32),
                        pltpu.VMEM((H_C, DH_C, DH_C), F32),
                        pltpu.VMEM((hp, DH_C), F32),
                        pltpu.VMEM((hp, LANES), F32)],
        compiler_params=_cparams(("arbitrary", "arbitrary")),
        name="mlstm",
    )(proj, proj, proj, proj, gates, conv_buf, c0, n0, m0, conv_w, conv_b, gate_bias, hn_g)


def _pad_heads(a):
    pad = [(0, 0)] * a.ndim
    pad[1] = (0, SUBLANES - a.shape[1])
    return jnp.pad(a, pad)


def _mixers(proj, gates, lw, attn_fn, sgu_fn, conv_buf, c0, n0, m0, t_rows):
    ya, nk, nv = attn_fn(proj)
    sgu_out = sgu_fn(proj)
    m0b = jnp.broadcast_to(_pad_heads(m0)[:, :, None], (m0.shape[0], SUBLANES, LANES))
    yc, nconv, c_new, n_new, m_new = _mlstm(proj, gates, conv_buf, c0, _pad_heads(n0), m0b,
                                            lw["conv_w"], lw["conv_b"], lw["gate_bias"], lw["hn_g"], t_rows)
    return ya, sgu_out, yc, nk, nv, nconv, c_new, n_new[:, :H_C], m_new[:, :H_C, 0]


def kernel(x_prompt, x_sample, c_prompt, c_sample, cache_k_win, cache_v_win, state_conv, state_C, state_n, state_m, rel_bias, norm_g, ada_w, ada_b, w_in, qn_g, kn_g, sgu_g, sgu_w, sgu_b, conv_w, conv_b, f_bias, i_bias, hn_g, w_out):
    depth = w_in.shape[0]
    bp, seq, _ = x_prompt.shape
    bd, t_new, _ = x_sample.shape
    n_past = cache_k_win.shape[2]
    assert t_new <= SUBLANES and n_past + t_new <= SAMPLE_KEYS and seq % 1024 == 0

    w_gate_bf = jnp.pad(w_in[:, :, D_MAIN:], ((0, 0), (0, 0), (0, LANES - 2 * H_C))).astype(BF16)
    w_out_bf = w_out.astype(BF16)

    rel_t = rel_bias.T
    bias_p = _expand_bias(rel_t, jnp.asarray(_prompt_bucket_table()), 64, LOG2E)
    bias_p = bias_p.reshape(H_A, N_PAT, 2, BLK_A, 2 * BLK_A)
    bias_s = _expand_bias(rel_t, jnp.asarray(_sample_bucket_table(n_past, t_new)), N_PAT * t_new, 1.0)
    bias_s = bias_s.reshape(H_A, N_PAT, t_new, SAMPLE_KEYS)

    n_c = bp + bd
    c_all = jnp.pad(jnp.concatenate([c_prompt, c_sample], axis=0), ((0, 2 * SUBLANES - n_c), (0, 0)))
    mod = _ada_mod(c_all, ada_w, ada_b)

    cache_k = cache_k_win.reshape(depth, bd, n_past, D_A)
    cache_v = cache_v_win.reshape(depth, bd, n_past, D_A)
    zeros_conv = jnp.zeros((bp, CONV_W - 1, 2 * D_C), F32)
    zeros_c = jnp.zeros((bp, H_C, DH_C, DH_C), F32)
    zeros_n = jnp.zeros((bp, H_C, DH_C), F32)
    zeros_m = jnp.zeros((bp, H_C), F32)
    tril8 = jnp.tril(jnp.ones((t_new, t_new), F32))

    xp = x_prompt.reshape(bp * seq, D_MODEL)
    xs = x_sample.reshape(bd * t_new, D_MODEL)
    outs = [[] for _ in range(13)]
    for l in range(depth):
        shift, scale, gate = (mod[l, :, i * D_MODEL:(i + 1) * D_MODEL] for i in range(3))
        p_mod = [a[:bp, None, :] for a in (scale, shift, gate)]
        s_mod = [jnp.repeat(a[bp:n_c], t_new, axis=0)[None] for a in (scale, shift, gate)]
        lw = {
            "conv_w": conv_w[l], "conv_b": conv_b[l][None],
            "gate_bias": jnp.pad(jnp.concatenate([i_bias[l], f_bias[l]]), (0, LANES - 2 * H_C))[None],
            "hn_g": hn_g[l][None],
        }
        gq2 = jnp.tile(qn_g[l], HPAIR)[None]
        gk2 = jnp.tile(kn_g[l], HPAIR)[None]
        ng = norm_g[l][None]
        sg = sgu_g[l][None]

        proj, gates = _inproj(xp, p_mod[0], p_mod[1], ng, w_in, w_gate_bf, l, 1024)
        proj = proj.reshape(bp, seq, D_MAIN)
        gates = gates.reshape(bp, seq, LANES)
        ya, yb, yc, nk, nv, ncv, nc_, nn_, nm = _mixers(
            proj, gates, lw,
            lambda pr: _attn_prompt(pr, gq2, gk2, bias_p),
            lambda pr: _sgu_prompt(pr, sg, sgu_w[l], sgu_b[l].T),
            zeros_conv, zeros_c, zeros_n, zeros_m, CHUNK_C)
        xp = _outproj(ya.reshape(bp * seq, D_A), yb.reshape(bp * seq, D_B), yc.reshape(bp * seq, D_C),
                      xp, p_mod[2], w_out_bf, l, 512)
        keep = nk.shape[1]
        for i, a in enumerate((nk.reshape(bp, keep, H_A, HEAD_DIM_A), nv.reshape(bp, keep, H_A, HEAD_DIM_A),
                               ncv, nc_, nn_, nm)):
            outs[i].append(a)

        proj_s, gates_s = _inproj(xs, s_mod[0], s_mod[1], ng, w_in, w_gate_bf, l, bd * t_new)
        proj_s = proj_s.reshape(bd, t_new, D_MAIN)
        gates_s = gates_s.reshape(bd, t_new, LANES)
        w8 = sgu_w[l][:, :t_new, :t_new] * tril8
        wexp = jnp.repeat(jnp.transpose(w8, (2, 1, 0)), C_B, axis=2)
        bexp = jnp.repeat(sgu_b[l][:, :t_new].T, C_B, axis=1)
        ya, sgu_out, yc, nk, nv, ncv, nc_, nn_, nm = _mixers(
            proj_s, gates_s, lw,
            lambda pr: _attn_sample(pr, cache_k, cache_v, gq2, gk2, bias_s, l),
            lambda pr: _sgu_sample(pr, sg, wexp, bexp),
            state_conv[l], state_C[l], state_n[l], state_m[l], t_new)
        yb, vn = sgu_out
        xs = _outproj(ya.reshape(bd * t_new, D_A), yb.reshape(bd * t_new, D_B), yc.reshape(bd * t_new, D_C),
                      xs, s_mod[2], w_out_bf, l, bd * t_new)
        for i, a in enumerate((nk.reshape(bd, t_new, H_A, HEAD_DIM_A), nv.reshape(bd, t_new, H_A, HEAD_DIM_A),
                               vn, ncv, nc_, nn_, nm)):
            outs[6 + i].append(a)

    stacked = [jnp.stack(o) for o in outs]
    return (xp.reshape(bp, seq, D_MODEL), xs.reshape(bd, t_new, D_MODEL), *stacked)
```

```python
import functools
import math

import numpy as np
import jax
import jax.numpy as jnp
from jax import lax
from jax.experimental import pallas as pl
from jax.experimental.pallas import tpu as pltpu

F32 = jnp.float32
BF16 = jnp.bfloat16

D_MODEL = 2048
HEAD_DIM_A = 64
D_A = 768
H_A = 12
D_B = 512
G_B = 4
C_B = 128
CHUNK_B = 128
D_C = 768
DH_C = 128
H_C = 6
CHUNK_C = 128
CONV_W = 4
PATTERNS = ((128, 1), (512, 4), (2048, 16))
N_PAT = len(PATTERNS)
WIN_MAX = 2048
BLK_A = 128
N_BUCKETS = 32
MAX_DIST = 2048
EPS = 1e-6
D_MAIN = 4 * D_A + 3 * D_B + 2 * D_C + 3 * D_C
D_IN = D_MAIN + 2 * H_C

LANES = 128
SUBLANES = 8
VMEM_LIMIT = 56 * 1024 * 1024

NEG = -1e30
LOG2E = 1.4426950408889634
PAD_A = BLK_A * PATTERNS[-1][1]
HPAIR = LANES // HEAD_DIM_A

COL_Q, COL_K, COL_V, COL_Z = 0, D_A // LANES, 2 * D_A // LANES, 3 * D_A // LANES
COL_UB, COL_VB, COL_ZB = 4 * D_A // D_B, 4 * D_A // D_B + 1, 4 * D_A // D_B + 2
COL_QK = (4 * D_A + 3 * D_B) // (2 * D_C)
COL_VC = (4 * D_A + 3 * D_B + 2 * D_C) // D_C
COL_OC, COL_ZC = COL_VC + 1, COL_VC + 2


def _cparams(sem):
    return pltpu.CompilerParams(dimension_semantics=sem, vmem_limit_bytes=VMEM_LIMIT)


def _silu(x):
    return x * jax.nn.sigmoid(x)


def _bdot(a, b):
    return jnp.dot(a.astype(BF16), b.astype(BF16), preferred_element_type=F32)


def _bdot_nt(a, b):
    return lax.dot_general(a.astype(BF16), b.astype(BF16), (((1,), (1,)), ((), ())),
                           preferred_element_type=F32)


def _bdot_tn(a, b):
    return lax.dot_general(a.astype(BF16), b.astype(BF16), (((0,), (0,)), ((), ())),
                           preferred_element_type=F32)


def _bucket_np(dist):
    max_exact = N_BUCKETS // 2
    df = np.maximum(dist, 1).astype(np.float32)
    large = max_exact + (np.log(df / np.float32(max_exact)) / np.float32(math.log(MAX_DIST / max_exact))
                         * np.float32(N_BUCKETS - max_exact)).astype(np.int32)
    return np.where(dist < max_exact, dist, np.minimum(large, N_BUCKETS - 1)).astype(np.int32)


def _prompt_bucket_table():
    qi = np.arange(BLK_A)[:, None]
    ki = np.arange(2 * BLK_A)[None, :]
    j = qi + BLK_A - ki
    out = []
    for win, dil in PATTERNS:
        n_back = win // dil
        band = (j >= 0) & (j <= n_back)
        b = _bucket_np(np.clip(j, 0, n_back) * dil)
        out.append(np.where(band, b, -1))
        out.append(np.where(band & (ki >= BLK_A), b, -1))
    return np.stack(out).reshape(N_PAT * 2 * BLK_A, 2 * BLK_A).astype(np.int32)


def _sample_bucket_table(n_past, t_new):
    c = np.arange(n_past + LANES)[None, :]
    t = np.arange(t_new)[:, None]
    delta = n_past + t - c
    out = []
    for win, dil in PATTERNS:
        valid = (c < n_past + t_new) & (delta >= 0) & (delta % dil == 0) & (delta // dil <= win // dil)
        out.append(np.where(valid, _bucket_np(np.maximum(delta, 0)), -1))
    return np.stack(out).reshape(N_PAT * t_new, n_past + LANES).astype(np.int32)


def _bias_kernel(rb_ref, idx_ref, out_ref, *, scale, row_chunk):
    h = pl.program_id(0)

    def chunk(c, carry):
        rows = pl.ds(pl.multiple_of(c * row_chunk, row_chunk), row_chunk)
        idx = idx_ref[rows, :]
        out = jnp.full(idx.shape, NEG, F32)
        for b in range(N_BUCKETS):
            out = jnp.where(idx == b, rb_ref[h, b] * scale, out)
        out_ref[rows, :] = out
        return carry
    lax.fori_loop(0, idx_ref.shape[0] // row_chunk, chunk, 0)


def _expand_bias(rel_bias_t, idx, row_chunk, scale):
    rows, cols = idx.shape
    return pl.pallas_call(
        functools.partial(_bias_kernel, scale=scale, row_chunk=row_chunk),
        out_shape=jax.ShapeDtypeStruct((H_A, rows, cols), F32),
        grid=(H_A,),
        in_specs=[pl.BlockSpec(memory_space=pltpu.SMEM),
                  pl.BlockSpec((rows, cols), lambda h: (0, 0))],
        out_specs=pl.BlockSpec((None, rows, cols), lambda h: (h, 0, 0)),
        compiler_params=_cparams(("arbitrary",)),
        name="bias_expand",
    )(rel_bias_t, idx)


def _ada_kernel(c_ref, w_ref, b_ref, o_ref):
    c = c_ref[...]
    a = _silu(c)
    w = w_ref[...]
    a_hi = a.astype(BF16)
    a_lo = (a - a_hi.astype(F32)).astype(BF16)
    w_hi = w.astype(BF16)
    w_lo = (w - w_hi.astype(F32)).astype(BF16)
    acc = jnp.dot(a_hi, w_hi, preferred_element_type=F32)
    acc += jnp.dot(a_hi, w_lo, preferred_element_type=F32)
    acc += jnp.dot(a_lo, w_hi, preferred_element_type=F32)
    o_ref[...] = acc + b_ref[...]


def _ada_mod(c_all, ada_w, ada_b):
    depth = ada_w.shape[0]
    rows = c_all.shape[0]
    tn = 768
    n = 3 * D_MODEL
    return pl.pallas_call(
        _ada_kernel,
        out_shape=jax.ShapeDtypeStruct((depth, rows, n), F32),
        grid=(depth, n // tn),
        in_specs=[pl.BlockSpec((rows, D_MODEL), lambda l, j: (0, 0)),
                  pl.BlockSpec((None, D_MODEL, tn), lambda l, j: (l, 0, j)),
                  pl.BlockSpec((None, 1, tn), lambda l, j: (l, 0, j))],
        out_specs=pl.BlockSpec((None, rows, tn), lambda l, j: (l, 0, j)),
        compiler_params=_cparams(("arbitrary", "arbitrary")),
        name="ada_mod",
    )(c_all, ada_w, ada_b.reshape(depth, 1, n))


def _inproj_kernel(x_ref, sc_ref, sh_ref, g_ref, w_ref, wg_ref, proj_ref, gates_ref, h_scr, *, row_chunk):
    @pl.when(pl.program_id(1) == 0)
    def _():
        def chunk(c, carry):
            rows = pl.ds(pl.multiple_of(c * row_chunk, row_chunk), row_chunk)
            x = x_ref[rows, :]
            ms = jnp.sum(x * x, axis=-1, keepdims=True) * (1.0 / D_MODEL)
            y = x * lax.rsqrt(ms + EPS) * g_ref[...]
            sc = sc_ref[...] if sc_ref.shape[0] == 1 else sc_ref[rows, :]
            sh = sh_ref[...] if sh_ref.shape[0] == 1 else sh_ref[rows, :]
            h_scr[rows, :] = (y * (1.0 + sc) + sh).astype(BF16)
            return carry
        lax.fori_loop(0, x_ref.shape[0] // row_chunk, chunk, 0)
        gates_ref[...] = jnp.dot(h_scr[...], wg_ref[...], preferred_element_type=F32)

    proj_ref[...] = jnp.dot(h_scr[...], w_ref[...].astype(BF16), preferred_element_type=F32)


def _inproj(x2d, scale, shift, norm_g, w_in, w_gate_bf, layer, tm):
    m = x2d.shape[0]
    groups = scale.shape[0]
    tiles_per_group = m // tm // groups
    tn = 768
    mod_spec = pl.BlockSpec((None, scale.shape[1], D_MODEL), lambda i, j: (i // tiles_per_group, 0, 0))
    return pl.pallas_call(
        functools.partial(_inproj_kernel, row_chunk=min(tm, 256)),
        out_shape=(jax.ShapeDtypeStruct((m, D_MAIN), F32), jax.ShapeDtypeStruct((m, LANES), F32)),
        grid=(m // tm, D_MAIN // tn),
        in_specs=[pl.BlockSpec((tm, D_MODEL), lambda i, j: (i, 0)),
                  mod_spec, mod_spec,
                  pl.BlockSpec((1, D_MODEL), lambda i, j: (0, 0)),
                  pl.BlockSpec((None, D_MODEL, tn), lambda i, j: (layer, 0, j)),
                  pl.BlockSpec((None, D_MODEL, LANES), lambda i, j: (layer, 0, 0))],
        out_specs=(pl.BlockSpec((tm, tn), lambda i, j: (i, j)),
                   pl.BlockSpec((tm, LANES), lambda i, j: (i, 0))),
        scratch_shapes=[pltpu.VMEM((tm, D_MODEL), BF16)],
        compiler_params=_cparams(("arbitrary", "arbitrary")),
        name="inproj",
    )(x2d, scale, shift, norm_g, w_in, w_gate_bf)


def _outproj_kernel(ya_ref, yb_ref, yc_ref, x_ref, gate_ref, w_ref, o_ref):
    y = jnp.dot(ya_ref[...], w_ref[0:D_A, :], preferred_element_type=F32)
    y += jnp.dot(yb_ref[...], w_ref[D_A:D_A + D_B, :], preferred_element_type=F32)
    y += jnp.dot(yc_ref[...], w_ref[D_A + D_B:, :], preferred_element_type=F32)
    o_ref[...] = x_ref[...] + gate_ref[...] * y


def _outproj(ya, yb, yc, x2d, gate, w_out_bf, layer, tm):
    m = x2d.shape[0]
    groups = gate.shape[0]
    tiles_per_group = m // tm // groups
    return pl.pallas_call(
        _outproj_kernel,
        out_shape=jax.ShapeDtypeStruct((m, D_MODEL), F32),
        grid=(m // tm,),
        in_specs=[pl.BlockSpec((tm, D_A), lambda i: (i, 0)),
                  pl.BlockSpec((tm, D_B), lambda i: (i, 0)),
                  pl.BlockSpec((tm, D_C), lambda i: (i, 0)),
                  pl.BlockSpec((tm, D_MODEL), lambda i: (i, 0)),
                  pl.BlockSpec((None, gate.shape[1], D_MODEL), lambda i: (i // tiles_per_group, 0, 0)),
                  pl.BlockSpec((None, D_MODEL, D_MODEL), lambda i: (layer, 0, 0))],
        out_specs=pl.BlockSpec((tm, D_MODEL), lambda i: (i, 0)),
        compiler_params=_cparams(("arbitrary",)),
        name="outproj",
    )(ya, yb, yc, x2d, gate, w_out_bf)


def _head_norm(x, g, left):
    x2 = x * x
    s_left = jnp.sum(jnp.where(left, x2, 0.0), axis=-1, keepdims=True)
    s_right = jnp.sum(jnp.where(left, 0.0, x2), axis=-1, keepdims=True)
    ms = jnp.where(left, s_left, s_right) * (1.0 / HEAD_DIM_A)
    return x * lax.rsqrt(ms + EPS) * g


def _attn_prompt_kernel(q_ref, k_ref, v_ref, z_ref, gq_ref, gk_ref, bias_ref,
                        y_ref, pk_ref, pv_ref,
                        qn_s, kp_s, vp_s, ones_s, o_s, l_s, m_s, *, seq, keep, unroll):
    norm_rows = 512
    lane_n = lax.broadcasted_iota(jnp.int32, (norm_rows, LANES), 1)
    left_n = lane_n < HEAD_DIM_A

    kp_s[0:PAD_A, :] = jnp.zeros((PAD_A, LANES), F32)
    vp_s[0:PAD_A, :] = jnp.zeros((PAD_A, LANES), F32)
    row_o = lax.broadcasted_iota(jnp.int32, ones_s.shape, 0)
    lane_o = lax.broadcasted_iota(jnp.int32, ones_s.shape, 1)
    ones_s[...] = jnp.where((row_o < 2 * BLK_A) == (lane_o < HEAD_DIM_A), 1.0, 0.0).astype(BF16)

    def norm_chunk(c, carry):
        r0 = pl.multiple_of(c * norm_rows, norm_rows)
        rows = pl.ds(r0, norm_rows)
        qn_s[rows, :] = _head_norm(q_ref[rows, :], gq_ref[...], left_n) * (HEAD_DIM_A ** -0.5 * LOG2E)
        kp_s[pl.ds(PAD_A + r0, norm_rows), :] = _head_norm(k_ref[rows, :], gk_ref[...], left_n)
        vp_s[pl.ds(PAD_A + r0, norm_rows), :] = v_ref[rows, :]
        return carry
    lax.fori_loop(0, seq // norm_rows, norm_chunk, 0)

    for c in range(keep // LANES):
        src = slice(PAD_A + seq - keep + c * LANES, PAD_A + seq - keep + (c + 1) * LANES)
        pk_ref[:, c * LANES:(c + 1) * LANES] = kp_s[src, :].T
        pv_ref[:, c * LANES:(c + 1) * LANES] = vp_s[src, :].T

    left_b = lax.broadcasted_iota(jnp.int32, (BLK_A, LANES), 1) < HEAD_DIM_A
    left_k = lax.broadcasted_iota(jnp.int32, (2 * BLK_A, LANES), 1) < HEAD_DIM_A

    for p, (win, dil) in enumerate(PATTERNS):
        blocks_per_residue = seq // (dil * BLK_A)

        def block(i, carry, p=p, dil=dil, blocks_per_residue=blocks_per_residue):
            r = i // blocks_per_residue
            n = i % blocks_per_residue
            q_start = r + n * (BLK_A * dil)
            k_start = PAD_A + q_start - BLK_A * dil
            if dil == 1:
                q_rows = pl.ds(pl.multiple_of(q_start, BLK_A), BLK_A)
                k_rows = pl.ds(pl.multiple_of(k_start, BLK_A), 2 * BLK_A)
            else:
                q_rows = pl.ds(q_start, BLK_A, stride=dil)
                k_rows = pl.ds(k_start, 2 * BLK_A, stride=dil)
            first = jnp.where(n == 0, 1, 0)
            q = qn_s[q_rows, :]
            k = kp_s[k_rows, :].astype(BF16)
            v = vp_s[k_rows, :]
            v2 = jnp.concatenate([jnp.where(left_k, v, 0.0), jnp.where(left_k, 0.0, v)], axis=0).astype(BF16)
            rhs = jnp.concatenate([v2, ones_s[...]], axis=1)
            es, ms = [], []
            for h in range(HPAIR):
                qh = jnp.where(left_b, q, 0.0) if h == 0 else jnp.where(left_b, 0.0, q)
                s = _bdot_nt(qh, k) + bias_ref[h, p, first]
                mh = jnp.max(s, axis=-1, keepdims=True)
                es.append(jnp.exp2(s - mh).astype(BF16))
                ms.append(mh)
            res = jnp.dot(jnp.concatenate(es, axis=1), rhs, preferred_element_type=F32)
            o_s[p, q_rows, :] = res[:, :LANES]
            l_s[p, q_rows, :] = res[:, LANES:]
            m_s[p, q_rows, :] = jnp.where(left_b, ms[0], ms[1])
            return carry
        lax.fori_loop(0, seq // BLK_A, block, 0, unroll=unroll)

    def out_chunk(c, carry):
        rows = pl.ds(pl.multiple_of(c * norm_rows, norm_rows), norm_rows)
        m = [m_s[p, rows, :] for p in range(N_PAT)]
        top = functools.reduce(jnp.maximum, m)
        w = [jnp.exp2(x - top) for x in m]
        num = functools.reduce(lambda a, b: a + b, [w[p] * o_s[p, rows, :] for p in range(N_PAT)])
        den = functools.reduce(lambda a, b: a + b, [w[p] * l_s[p, rows, :] for p in range(N_PAT)])
        y_ref[rows, :] = (num / den * _silu(z_ref[rows, :])).astype(BF16)
        return carry
    lax.fori_loop(0, seq // norm_rows, out_chunk, 0)


def _attn_prompt(proj, gq2, gk2, bias_p):
    b, seq, _ = proj.shape
    keep = min(WIN_MAX, seq)
    assert seq % (BLK_A * PATTERNS[-1][1]) == 0
    col = lambda base: pl.BlockSpec((None, seq, LANES), lambda bi, hp: (bi, 0, base + hp))
    vec = pl.BlockSpec((1, LANES), lambda bi, hp: (0, 0))
    return pl.pallas_call(
        functools.partial(_attn_prompt_kernel, seq=seq, keep=keep, unroll=8),
        out_shape=(jax.ShapeDtypeStruct((b, seq, D_A), BF16),
                   jax.ShapeDtypeStruct((b, D_A, keep), F32),
                   jax.ShapeDtypeStruct((b, D_A, keep), F32)),
        grid=(b, H_A // HPAIR),
        in_specs=[col(COL_Q), col(COL_K), col(COL_V), col(COL_Z), vec, vec,
                  pl.BlockSpec((HPAIR, N_PAT, 2, BLK_A, 2 * BLK_A), lambda bi, hp: (hp, 0, 0, 0, 0))],
        out_specs=(pl.BlockSpec((None, seq, LANES), lambda bi, hp: (bi, 0, hp)),
                   pl.BlockSpec((None, LANES, keep), lambda bi, hp: (bi, hp, 0)),
                   pl.BlockSpec((None, LANES, keep), lambda bi, hp: (bi, hp, 0))),
        scratch_shapes=[pltpu.VMEM((seq, LANES), F32),
                        pltpu.VMEM((PAD_A + seq, LANES), F32),
                        pltpu.VMEM((PAD_A + seq, LANES), F32),
                        pltpu.VMEM((2 * HPAIR * BLK_A, LANES), BF16),
                        pltpu.VMEM((N_PAT, seq, LANES), F32),
                        pltpu.VMEM((N_PAT, seq, LANES), F32),
                        pltpu.VMEM((N_PAT, seq, LANES), F32)],
        compiler_params=_cparams(("arbitrary", "arbitrary")),
        name="attn_prompt",
    )(proj, proj, proj, proj, gq2, gk2, bias_p)


def _attn_sample_kernel(q_ref, k_ref, v_ref, z_ref, ck_ref, cv_ref, gq_ref, gk_ref, bias_ref,
                        y_ref, nk_ref, nv_ref, *, n_past, t_new):
    q_all = q_ref[...]
    k_all = k_ref[...]
    v_all = v_ref[...]
    z_all = z_ref[...]
    nv_ref[...] = v_all
    pad_rows = jnp.zeros((LANES - t_new, HEAD_DIM_A), F32)

    def norm(x, g):
        ms = jnp.sum(x * x, axis=-1, keepdims=True) * (1.0 / HEAD_DIM_A)
        return x * lax.rsqrt(ms + EPS) * g

    for h in range(H_A):
        cols = slice(h * HEAD_DIM_A, (h + 1) * HEAD_DIM_A)
        qn = (norm(q_all[:, cols], gq_ref[:, cols]) * (HEAD_DIM_A ** -0.5)).astype(BF16)
        kn = norm(k_all[:, cols], gk_ref[:, cols])
        nk_ref[:, cols] = kn
        k_new = jnp.concatenate([kn, pad_rows], axis=0).astype(BF16)
        v_new = jnp.concatenate([v_all[:, cols], pad_rows], axis=0).astype(BF16)
        k_t = ck_ref[h].astype(BF16)
        v_t = cv_ref[h].astype(BF16)
        s_c = jnp.dot(qn, k_t, preferred_element_type=F32)
        s_n = _bdot_nt(qn, k_new)
        sc = [s_c + bias_ref[h, p, :, 0:n_past] for p in range(N_PAT)]
        sn = [s_n + bias_ref[h, p, :, n_past:] for p in range(N_PAT)]
        m = functools.reduce(jnp.maximum, [jnp.max(x, axis=-1, keepdims=True) for x in sc + sn])
        e_c = functools.reduce(lambda a, b: a + b, [jnp.exp(x - m) for x in sc])
        e_n = functools.reduce(lambda a, b: a + b, [jnp.exp(x - m) for x in sn])
        den = jnp.sum(e_c, axis=-1, keepdims=True) + jnp.sum(e_n, axis=-1, keepdims=True)
        o = _bdot_nt(e_c, v_t) + jnp.dot(e_n.astype(BF16), v_new, preferred_element_type=F32)
        y_ref[:, cols] = (o / den * _silu(z_all[:, cols])).astype(BF16)


def _attn_sample(proj_s, cache_kt, cache_vt, gq, gk, bias_s, layer):
    bd, t_new, _ = proj_s.shape
    n_past = cache_kt.shape[-1]
    col = lambda base: pl.BlockSpec((None, t_new, D_A), lambda bi: (bi, 0, base))
    vec = pl.BlockSpec((1, D_A), lambda bi: (0, 0))
    cache = pl.BlockSpec((None, None, H_A, HEAD_DIM_A, n_past), lambda bi: (layer, bi, 0, 0, 0))
    out = pl.BlockSpec((None, t_new, D_A), lambda bi: (bi, 0, 0))
    return pl.pallas_call(
        functools.partial(_attn_sample_kernel, n_past=n_past, t_new=t_new),
        out_shape=(jax.ShapeDtypeStruct((bd, t_new, D_A), BF16),
                   jax.ShapeDtypeStruct((bd, t_new, D_A), F32),
                   jax.ShapeDtypeStruct((bd, t_new, D_A), F32)),
        grid=(bd,),
        in_specs=[col(0), col(1), col(2), col(3), cache, cache, vec, vec,
                  pl.BlockSpec((H_A, N_PAT, t_new, n_past + LANES), lambda bi: (0, 0, 0, 0))],
        out_specs=(out, out, out),
        compiler_params=_cparams(("arbitrary",)),
        name="attn_sample",
    )(proj_s, proj_s, proj_s, proj_s, cache_kt, cache_vt, gq, gk, bias_s)


def _sgu_prompt_kernel(u_ref, v_ref, z_ref, g_ref, w_ref, bt_ref, y_ref, *, chunks):
    row = lax.broadcasted_iota(jnp.int32, (CHUNK_B, CHUNK_B), 0)
    colm = lax.broadcasted_iota(jnp.int32, (CHUNK_B, CHUNK_B), 1)
    tril = row >= colm
    ws = [jnp.where(tril, w_ref[g], 0.0).astype(BF16) for g in range(G_B)]
    for c in range(chunks):
        rows = slice(c * CHUNK_B, (c + 1) * CHUNK_B)
        v = v_ref[rows, :]
        ms = jnp.sum(v * v, axis=-1, keepdims=True) * (1.0 / D_B)
        vn = (v * lax.rsqrt(ms + EPS) * g_ref[...]).astype(BF16)
        for g in range(G_B):
            cols = slice(g * C_B, (g + 1) * C_B)
            mix = jnp.dot(ws[g], vn[:, cols], preferred_element_type=F32) + bt_ref[:, g:g + 1]
            y_ref[rows, cols] = (u_ref[rows, cols] * mix * _silu(z_ref[rows, cols])).astype(BF16)


def _sgu_prompt(proj, sgu_g, sgu_w, sgu_bt):
    b, seq, _ = proj.shape
    ts = 512
    col = lambda base: pl.BlockSpec((None, ts, D_B), lambda bi, i: (bi, i, base))
    return pl.pallas_call(
        functools.partial(_sgu_prompt_kernel, chunks=ts // CHUNK_B),
        out_shape=jax.ShapeDtypeStruct((b, seq, D_B), BF16),
        grid=(b, seq // ts),
        in_specs=[col(COL_UB), col(COL_VB), col(COL_ZB),
                  pl.BlockSpec((1, D_B), lambda bi, i: (0, 0)),
                  pl.BlockSpec((G_B, CHUNK_B, CHUNK_B), lambda bi, i: (0, 0, 0)),
                  pl.BlockSpec((CHUNK_B, G_B), lambda bi, i: (0, 0))],
        out_specs=pl.BlockSpec((None, ts, D_B), lambda bi, i: (bi, i, 0)),
        compiler_params=_cparams(("arbitrary", "arbitrary")),
        name="sgu_prompt",
    )(proj, proj, proj, sgu_g, sgu_w, sgu_bt)


def _sgu_sample_kernel(u_ref, v_ref, z_ref, g_ref, wexp_ref, bexp_ref, y_ref, vn_ref, *, t_new):
    v = v_ref[...]
    ms = jnp.sum(v * v, axis=-1, keepdims=True) * (1.0 / D_B)
    vn = v * lax.rsqrt(ms + EPS) * g_ref[...]
    vn_ref[...] = vn
    mix = bexp_ref[...]
    for s in range(t_new):
        mix = mix + wexp_ref[s] * vn[s:s + 1, :]
    y_ref[...] = (u_ref[...] * mix * _silu(z_ref[...])).astype(BF16)


def _sgu_sample(proj_s, sgu_g, wexp, bexp):
    bd, t_new, _ = proj_s.shape
    col = lambda base: pl.BlockSpec((None, t_new, D_B), lambda bi: (bi, 0, base))
    return pl.pallas_call(
        functools.partial(_sgu_sample_kernel, t_new=t_new),
        out_shape=(jax.ShapeDtypeStruct((bd, t_new, D_B), BF16),
                   jax.ShapeDtypeStruct((bd, t_new, D_B), F32)),
        grid=(bd,),
        in_specs=[col(COL_UB), col(COL_VB), col(COL_ZB),
                  pl.BlockSpec((1, D_B), lambda bi: (0, 0)),
                  pl.BlockSpec((t_new, t_new, D_B), lambda bi: (0, 0, 0)),
                  pl.BlockSpec((t_new, D_B), lambda bi: (0, 0))],
        out_specs=(pl.BlockSpec((None, t_new, D_B), lambda bi: (bi, 0, 0)),
                   pl.BlockSpec((None, t_new, D_B), lambda bi: (bi, 0, 0))),
        compiler_params=_cparams(("arbitrary",)),
        name="sgu_sample",
    )(proj_s, proj_s, proj_s, sgu_g, wexp, bexp)


def _log_sigmoid(x):
    return jnp.minimum(x, 0.0) - jnp.log1p(jnp.exp(-jnp.abs(x)))


def _mlstm_kernel(qk_ref, v_ref, o_ref, z_ref, gt_ref, cbuf_ref, c0_ref, n0_ref, m0_ref,
                  cw_ref, cb_ref, gb_ref, hn_ref,
                  y_ref, conv_out_ref, c_out_ref, n_out_ref, m_out_ref,
                  xp_s, vp_s, gp_s, caug_s, m_s, *, t_rows):
    L = CHUNK_C
    ci = pl.program_id(1)
    last = pl.num_programs(1) - 1
    base = SUBLANES
    carry_rows = CONV_W - 1
    row_l = lax.broadcasted_iota(jnp.int32, (L, LANES), 0)
    lane_l = lax.broadcasted_iota(jnp.int32, (L, LANES), 1)
    zero_rows = jnp.zeros((L - SUBLANES, LANES), F32)

    def to_cols(x8):
        return jnp.concatenate([x8, zero_rows], axis=0).T

    @pl.when(ci == 0)
    def _():
        xp_s[...] = jnp.zeros(xp_s.shape, F32)
        vp_s[...] = jnp.zeros(vp_s.shape, F32)
        gp_s[...] = jnp.zeros(gp_s.shape, F32)
        xp_s[base - carry_rows:base, :] = cbuf_ref[...]
        n0_cols = to_cols(n0_ref[...])
        for h in range(H_C):
            caug_s[h, :, 0:DH_C] = c0_ref[h]
            caug_s[h, :, DH_C:] = jnp.broadcast_to(n0_cols[:, h:h + 1], (DH_C, DH_C))
        m_s[...] = m0_ref[...]

    xp_s[base:base + t_rows, :] = qk_ref[...]
    vp_s[0:t_rows, :] = v_ref[...]
    gp_s[0:t_rows, :] = gt_ref[...]

    acc = cb_ref[...] + cw_ref[CONV_W - 1:CONV_W, :] * xp_s[base:base + L, :]
    for j in range(CONV_W - 1):
        off = base - carry_rows + j
        acc = acc + cw_ref[j:j + 1, :] * xp_s[off:off + L, :]
    qk = _silu(acc)
    new_carry = xp_s[base + t_rows - carry_rows:base + t_rows, :]
    xp_s[base - carry_rows:base, :] = new_carry

    gates_t = (gp_s[...] + gb_ref[...]).T
    lane_t = lax.broadcasted_iota(jnp.int32, (SUBLANES, L), 1)
    real_t = lane_t < t_rows
    i_t = jnp.where(real_t, gates_t[0:SUBLANES, :], NEG)
    logf_t = jnp.where(real_t, _log_sigmoid(gates_t[SUBLANES:2 * SUBLANES, :]), 0.0)
    upper = (row_l <= lane_l).astype(BF16)
    f1 = logf_t.astype(BF16)
    r1 = logf_t - f1.astype(F32)
    f2 = r1.astype(BF16)
    f3 = (r1 - f2.astype(F32)).astype(BF16)
    b_t = (jnp.dot(f1, upper, preferred_element_type=F32) + jnp.dot(f2, upper, preferred_element_type=F32)
           + jnp.dot(f3, upper, preferred_element_type=F32))
    m_prev = m_s[...]
    c_t = i_t - b_t
    pm = c_t
    shift = 1
    while shift < L:
        pm = jnp.maximum(pm, jnp.where(lane_t >= shift, pltpu.roll(pm, shift, axis=1), NEG))
        shift *= 2
    d_t = -jnp.maximum(m_prev, pm)
    wi_t = jnp.exp(m_prev + d_t)
    em_t = jnp.exp(-(b_t - d_t))
    bl = jnp.broadcast_to(b_t[:, L - 1:L], (SUBLANES, L))
    g_t = bl - b_t + i_t
    m_new = jnp.maximum(bl + m_prev, jnp.broadcast_to(jnp.max(g_t, axis=1, keepdims=True), (SUBLANES, L)))
    ws_t = jnp.exp(g_t - m_new)
    wc = jnp.exp(bl + m_prev - m_new)
    m_s[...] = m_new
    stats = jnp.concatenate([d_t, wi_t, em_t, ws_t, zero_rows[0:L - 4 * SUBLANES]], axis=0).T

    causal = row_l >= lane_l
    ones_blk = jnp.ones((L, DH_C), BF16)
    v_all = vp_s[...]
    o_all = o_ref[...]
    z_all = z_ref[...]

    for h in range(H_C):
        cols = slice(h * DH_C, (h + 1) * DH_C)
        q = qk[:, cols].astype(BF16)
        k = qk[:, D_C + h * DH_C:D_C + (h + 1) * DH_C] * (DH_C ** -0.5)
        v_aug = jnp.concatenate([v_all[:, cols].astype(BF16), ones_blk], axis=1)
        d_col = stats[:, h:h + 1]
        wi_col = stats[:, SUBLANES + h:SUBLANES + h + 1]
        em_col = stats[:, 2 * SUBLANES + h:2 * SUBLANES + h + 1]
        ws_col = stats[:, 3 * SUBLANES + h:3 * SUBLANES + h + 1]
        c_prev = caug_s[h]

        a = jnp.exp(jnp.where(causal, d_col + c_t[h:h + 1, :], NEG)) * _bdot_nt(q, k)
        r1_ = jnp.dot(a.astype(BF16), v_aug, preferred_element_type=F32)
        r2_ = jnp.dot(q, c_prev.astype(BF16), preferred_element_type=F32)
        num = r1_[:, 0:DH_C] + wi_col * r2_[:, 0:DH_C]
        den = r1_[:, DH_C:] + wi_col * r2_[:, DH_C:]
        hh = num / jnp.maximum(jnp.abs(den), em_col)

        wc_row = jnp.concatenate([wc[h:h + 1, :], wc[h:h + 1, :]], axis=1)
        caug_s[h] = wc_row * c_prev + _bdot_tn(k * ws_col, v_aug)

        ms = jnp.sum(hh * hh, axis=-1, keepdims=True) * (1.0 / DH_C)
        hn = hh * lax.rsqrt(ms + EPS) * hn_ref[:, cols]
        yv = hn[0:t_rows, :] * jax.nn.sigmoid(o_all[:, cols]) * _silu(z_all[:, cols])
        y_ref[:, cols] = yv.astype(BF16)

    @pl.when(ci == last)
    def _():
        conv_out_ref[...] = new_carry
        n_cols = jnp.zeros((DH_C, LANES), F32)
        for h in range(H_C):
            c_out_ref[h] = caug_s[h, :, 0:DH_C]
            n_cols = jnp.where(lane_l == h, caug_s[h, :, DH_C:], n_cols)
        n_out_ref[...] = n_cols.T[0:SUBLANES, :]
        m_out_ref[...] = m_s[...]


def _mlstm(proj, gates, conv_buf, c0, n0, m0, conv_w, conv_b, gate_bias, hn_g, t_rows):
    b, seq, _ = proj.shape
    nchunks = seq // t_rows
    hp = SUBLANES
    col = lambda base, width: pl.BlockSpec((None, t_rows, width), lambda bi, ci: (bi, ci, base))
    const2 = lambda shape: pl.BlockSpec(shape, lambda bi, ci: (0, 0))
    per_b3 = lambda shape: pl.BlockSpec((None,) + shape, lambda bi, ci: (bi, 0, 0))
    per_b4 = lambda shape: pl.BlockSpec((None,) + shape, lambda bi, ci: (bi, 0, 0, 0))
    return pl.pallas_call(
        functools.partial(_mlstm_kernel, t_rows=t_rows),
        out_shape=(jax.ShapeDtypeStruct((b, seq, D_C), BF16),
                   jax.ShapeDtypeStruct((b, CONV_W - 1, 2 * D_C), F32),
                   jax.ShapeDtypeStruct((b, H_C, DH_C, DH_C), F32),
                   jax.ShapeDtypeStruct((b, hp, DH_C), F32),
                   jax.ShapeDtypeStruct((b, hp, LANES), F32)),
        grid=(b, nchunks),
        in_specs=[col(COL_QK, 2 * D_C), col(COL_VC, D_C), col(COL_OC, D_C), col(COL_ZC, D_C),
                  pl.BlockSpec((None, t_rows, LANES), lambda bi, ci: (bi, ci, 0)),
                  per_b3((CONV_W - 1, 2 * D_C)), per_b4((H_C, DH_C, DH_C)), per_b3((hp, DH_C)),
                  per_b3((hp, LANES)),
                  const2((CONV_W, 2 * D_C)), const2((1, 2 * D_C)), const2((1, LANES)), const2((1, D_C))],
        out_specs=(pl.BlockSpec((None, t_rows, D_C), lambda bi, ci: (bi, ci, 0)),
                   per_b3((CONV_W - 1, 2 * D_C)), per_b4((H_C, DH_C, DH_C)), per_b3((hp, DH_C)),
                   per_b3((hp, LANES))),
        scratch_shapes=[pltpu.VMEM((SUBLANES + CHUNK_C, 2 * D_C), F32),
                        pltpu.VMEM((CHUNK_C, D_C), F32),
                        pltpu.VMEM((CHUNK_C, LANES), F32),
                        pltpu.VMEM((H_C, DH_C, 2 * DH_C), F32),
                        pltpu.VMEM((hp, LANES), F32)],
        compiler_params=_cparams(("arbitrary", "arbitrary")),
        name="mlstm",
    )(proj, proj, proj, proj, gates, conv_buf, c0, n0, m0, conv_w, conv_b, gate_bias, hn_g)


def _gate_lanes(i_part, f_part):
    lead = i_part.shape[:-1]
    gap = jnp.zeros(lead + (SUBLANES - H_C,), i_part.dtype)
    tail = jnp.zeros(lead + (LANES - 2 * SUBLANES,), i_part.dtype)
    return jnp.concatenate([i_part, gap, f_part, gap, tail], axis=-1)


def _pad_heads(a):
    pad = [(0, 0)] * a.ndim
    pad[1] = (0, SUBLANES - a.shape[1])
    return jnp.pad(a, pad)


def _mixers(proj, gates, lw, attn_fn, sgu_fn, conv_buf, c0, n0, m0, t_rows):
    ya, nk, nv = attn_fn(proj)
    sgu_out = sgu_fn(proj)
    m0b = jnp.broadcast_to(_pad_heads(m0)[:, :, None], (m0.shape[0], SUBLANES, LANES))
    yc, nconv, c_new, n_new, m_new = _mlstm(proj, gates, conv_buf, c0, _pad_heads(n0), m0b,
                                            lw["conv_w"], lw["conv_b"], lw["gate_bias"], lw["hn_g"], t_rows)
    return ya, sgu_out, yc, nk, nv, nconv, c_new, n_new[:, :H_C], m_new[:, :H_C, 0]


def kernel(x_prompt, x_sample, c_prompt, c_sample, cache_k_win, cache_v_win, state_conv, state_C, state_n, state_m, rel_bias, norm_g, ada_w, ada_b, w_in, qn_g, kn_g, sgu_g, sgu_w, sgu_b, conv_w, conv_b, f_bias, i_bias, hn_g, w_out):
    depth = w_in.shape[0]
    bp, seq, _ = x_prompt.shape
    bd, t_new, _ = x_sample.shape
    n_past = cache_k_win.shape[2]
    assert t_new == SUBLANES and n_past % LANES == 0 and seq % 1024 == 0

    w_gate_bf = _gate_lanes(w_in[:, :, D_MAIN:D_MAIN + H_C], w_in[:, :, D_MAIN + H_C:]).astype(BF16)
    w_out_bf = w_out.astype(BF16)

    rel_t = rel_bias.T
    bias_p = _expand_bias(rel_t, jnp.asarray(_prompt_bucket_table()), 64, LOG2E)
    bias_p = bias_p.reshape(H_A, N_PAT, 2, BLK_A, 2 * BLK_A)
    bias_s = _expand_bias(rel_t, jnp.asarray(_sample_bucket_table(n_past, t_new)), SUBLANES, 1.0)
    bias_s = bias_s.reshape(H_A, N_PAT, t_new, n_past + LANES)

    n_c = bp + bd
    c_all = jnp.pad(jnp.concatenate([c_prompt, c_sample], axis=0), ((0, 2 * SUBLANES - n_c), (0, 0)))
    mod = _ada_mod(c_all, ada_w, ada_b)

    cache_kt = jnp.transpose(cache_k_win, (0, 1, 3, 4, 2))
    cache_vt = jnp.transpose(cache_v_win, (0, 1, 3, 4, 2))
    gq_all = [jnp.tile(qn_g[l], H_A)[None] for l in range(depth)]
    gk_all = [jnp.tile(kn_g[l], H_A)[None] for l in range(depth)]
    zeros_conv = jnp.zeros((bp, CONV_W - 1, 2 * D_C), F32)
    zeros_c = jnp.zeros((bp, H_C, DH_C, DH_C), F32)
    zeros_n = jnp.zeros((bp, H_C, DH_C), F32)
    zeros_m = jnp.zeros((bp, H_C), F32)
    tril8 = jnp.tril(jnp.ones((t_new, t_new), F32))

    xp = x_prompt.reshape(bp * seq, D_MODEL)
    xs = x_sample.reshape(bd * t_new, D_MODEL)
    outs = [[] for _ in range(13)]
    for l in range(depth):
        shift, scale, gate = (mod[l, :, i * D_MODEL:(i + 1) * D_MODEL] for i in range(3))
        p_mod = [a[:bp, None, :] for a in (scale, shift, gate)]
        s_mod = [jnp.repeat(a[bp:n_c], t_new, axis=0)[None] for a in (scale, shift, gate)]
        lw = {
            "conv_w": conv_w[l], "conv_b": conv_b[l][None],
            "gate_bias": _gate_lanes(i_bias[l], f_bias[l])[None],
            "hn_g": hn_g[l][None],
        }
        gq2 = jnp.tile(qn_g[l], HPAIR)[None]
        gk2 = jnp.tile(kn_g[l], HPAIR)[None]
        ng = norm_g[l][None]
        sg = sgu_g[l][None]

        proj, gates = _inproj(xp, p_mod[0], p_mod[1], ng, w_in, w_gate_bf, l, 1024)
        proj = proj.reshape(bp, seq, D_MAIN)
        gates = gates.reshape(bp, seq, LANES)
        ya, yb, yc, nk, nv, ncv, nc_, nn_, nm = _mixers(
            proj, gates, lw,
            lambda pr: _attn_prompt(pr, gq2, gk2, bias_p),
            lambda pr: _sgu_prompt(pr, sg, sgu_w[l], sgu_b[l].T),
            zeros_conv, zeros_c, zeros_n, zeros_m, CHUNK_C)
        xp = _outproj(ya.reshape(bp * seq, D_A), yb.reshape(bp * seq, D_B), yc.reshape(bp * seq, D_C),
                      xp, p_mod[2], w_out_bf, l, 512)
        keep = nk.shape[2]
        for i, a in enumerate((nk.reshape(bp, H_A, HEAD_DIM_A, keep), nv.reshape(bp, H_A, HEAD_DIM_A, keep),
                               ncv, nc_, nn_, nm)):
            outs[i].append(a)

        proj_s, gates_s = _inproj(xs, s_mod[0], s_mod[1], ng, w_in, w_gate_bf, l, bd * t_new)
        proj_s = proj_s.reshape(bd, t_new, D_MAIN)
        gates_s = gates_s.reshape(bd, t_new, LANES)
        w8 = sgu_w[l][:, :t_new, :t_new] * tril8
        wexp = jnp.repeat(jnp.transpose(w8, (2, 1, 0)), C_B, axis=2)
        bexp = jnp.repeat(sgu_b[l][:, :t_new].T, C_B, axis=1)
        ya, sgu_out, yc, nk, nv, ncv, nc_, nn_, nm = _mixers(
            proj_s, gates_s, lw,
            lambda pr: _attn_sample(pr, cache_kt, cache_vt, gq_all[l], gk_all[l], bias_s, l),
            lambda pr: _sgu_sample(pr, sg, wexp, bexp),
            state_conv[l], state_C[l], state_n[l], state_m[l], t_new)
        yb, vn = sgu_out
        xs = _outproj(ya.reshape(bd * t_new, D_A), yb.reshape(bd * t_new, D_B), yc.reshape(bd * t_new, D_C),
                      xs, s_mod[2], w_out_bf, l, bd * t_new)
        for i, a in enumerate((nk.reshape(bd, t_new, H_A, HEAD_DIM_A), nv.reshape(bd, t_new, H_A, HEAD_DIM_A),
                               vn, ncv, nc_, nn_, nm)):
            outs[6 + i].append(a)

    stacked = [jnp.stack(o) for o in outs]
    for i in (0, 1):
        stacked[i] = jnp.transpose(stacked[i], (0, 1, 4, 2, 3))
    return (xp.reshape(bp, seq, D_MODEL), xs.reshape(bd, t_new, D_MODEL), *stacked)
```

```python
import functools
import math

import numpy as np
import jax
import jax.numpy as jnp
from jax import lax
from jax.experimental import pallas as pl
from jax.experimental.pallas import tpu as pltpu

F32 = jnp.float32
BF16 = jnp.bfloat16

D_MODEL = 2048
HEAD_DIM_A = 64
D_A = 768
H_A = 12
D_B = 512
G_B = 4
C_B = 128
CHUNK_B = 128
D_C = 768
DH_C = 128
H_C = 6
CHUNK_C = 128
CONV_W = 4
PATTERNS = ((128, 1), (512, 4), (2048, 16))
N_PAT = len(PATTERNS)
WIN_MAX = 2048
BLK_A = 128
N_BUCKETS = 32
MAX_DIST = 2048
EPS = 1e-6
D_MAIN = 4 * D_A + 3 * D_B + 2 * D_C + 3 * D_C
D_IN = D_MAIN + 2 * H_C

LANES = 128
SUBLANES = 8
VMEM_LIMIT = 56 * 1024 * 1024

NEG = -1e30
LOG2E = 1.4426950408889634
PAD_A = BLK_A * PATTERNS[-1][1]
HPAIR = LANES // HEAD_DIM_A

COL_Q, COL_K, COL_V, COL_Z = 0, D_A // LANES, 2 * D_A // LANES, 3 * D_A // LANES
COL_UB, COL_VB, COL_ZB = 4 * D_A // D_B, 4 * D_A // D_B + 1, 4 * D_A // D_B + 2
COL_QK = (4 * D_A + 3 * D_B) // (2 * D_C)
COL_VC = (4 * D_A + 3 * D_B + 2 * D_C) // D_C
COL_OC, COL_ZC = COL_VC + 1, COL_VC + 2


def _cparams(sem):
    return pltpu.CompilerParams(dimension_semantics=sem, vmem_limit_bytes=VMEM_LIMIT)


def _silu(x):
    return x * jax.nn.sigmoid(x)


def _bdot(a, b):
    return jnp.dot(a.astype(BF16), b.astype(BF16), preferred_element_type=F32)


def _bdot_nt(a, b):
    return lax.dot_general(a.astype(BF16), b.astype(BF16), (((1,), (1,)), ((), ())),
                           preferred_element_type=F32)


def _bdot_tn(a, b):
    return lax.dot_general(a.astype(BF16), b.astype(BF16), (((0,), (0,)), ((), ())),
                           preferred_element_type=F32)


def _bucket_np(dist):
    max_exact = N_BUCKETS // 2
    df = np.maximum(dist, 1).astype(np.float32)
    large = max_exact + (np.log(df / np.float32(max_exact)) / np.float32(math.log(MAX_DIST / max_exact))
                         * np.float32(N_BUCKETS - max_exact)).astype(np.int32)
    return np.where(dist < max_exact, dist, np.minimum(large, N_BUCKETS - 1)).astype(np.int32)


def _prompt_bucket_table():
    qi = np.arange(BLK_A)[:, None]
    ki = np.arange(2 * BLK_A)[None, :]
    j = qi + BLK_A - ki
    out = []
    for win, dil in PATTERNS:
        n_back = win // dil
        band = (j >= 0) & (j <= n_back)
        b = _bucket_np(np.clip(j, 0, n_back) * dil)
        out.append(np.where(band, b, -1))
        out.append(np.where(band & (ki >= BLK_A), b, -1))
    return np.stack(out).reshape(N_PAT * 2 * BLK_A, 2 * BLK_A).astype(np.int32)


def _sample_bucket_table(n_past, t_new):
    c = np.arange(n_past + LANES)[None, :]
    t = np.arange(t_new)[:, None]
    delta = n_past + t - c
    out = []
    for win, dil in PATTERNS:
        valid = (c < n_past + t_new) & (delta >= 0) & (delta % dil == 0) & (delta // dil <= win // dil)
        out.append(np.where(valid, _bucket_np(np.maximum(delta, 0)), -1))
    return np.stack(out).reshape(N_PAT * t_new, n_past + LANES).astype(np.int32)


def _bias_kernel(rb_ref, idx_ref, out_ref, *, scale, row_chunk):
    h = pl.program_id(0)

    def chunk(c, carry):
        rows = pl.ds(pl.multiple_of(c * row_chunk, row_chunk), row_chunk)
        idx = idx_ref[rows, :]
        out = jnp.full(idx.shape, NEG, F32)
        for b in range(N_BUCKETS):
            out = jnp.where(idx == b, rb_ref[h, b] * scale, out)
        out_ref[rows, :] = out
        return carry
    lax.fori_loop(0, idx_ref.shape[0] // row_chunk, chunk, 0)


def _expand_bias(rel_bias_t, idx, row_chunk, scale):
    rows, cols = idx.shape
    return pl.pallas_call(
        functools.partial(_bias_kernel, scale=scale, row_chunk=row_chunk),
        out_shape=jax.ShapeDtypeStruct((H_A, rows, cols), F32),
        grid=(H_A,),
        in_specs=[pl.BlockSpec(memory_space=pltpu.SMEM),
                  pl.BlockSpec((rows, cols), lambda h: (0, 0))],
        out_specs=pl.BlockSpec((None, rows, cols), lambda h: (h, 0, 0)),
        compiler_params=_cparams(("arbitrary",)),
        name="bias_expand",
    )(rel_bias_t, idx)


def _ada_kernel(c_ref, w_ref, b_ref, o_ref):
    c = c_ref[...]
    a = _silu(c)
    w = w_ref[...]
    a_hi = a.astype(BF16)
    a_lo = (a - a_hi.astype(F32)).astype(BF16)
    w_hi = w.astype(BF16)
    w_lo = (w - w_hi.astype(F32)).astype(BF16)
    acc = jnp.dot(a_hi, w_hi, preferred_element_type=F32)
    acc += jnp.dot(a_hi, w_lo, preferred_element_type=F32)
    acc += jnp.dot(a_lo, w_hi, preferred_element_type=F32)
    o_ref[...] = acc + b_ref[...]


def _ada_mod(c_all, ada_w, ada_b):
    depth = ada_w.shape[0]
    rows = c_all.shape[0]
    tn = 768
    n = 3 * D_MODEL
    return pl.pallas_call(
        _ada_kernel,
        out_shape=jax.ShapeDtypeStruct((depth, rows, n), F32),
        grid=(depth, n // tn),
        in_specs=[pl.BlockSpec((rows, D_MODEL), lambda l, j: (0, 0)),
                  pl.BlockSpec((None, D_MODEL, tn), lambda l, j: (l, 0, j)),
                  pl.BlockSpec((None, 1, tn), lambda l, j: (l, 0, j))],
        out_specs=pl.BlockSpec((None, rows, tn), lambda l, j: (l, 0, j)),
        compiler_params=_cparams(("arbitrary", "arbitrary")),
        name="ada_mod",
    )(c_all, ada_w, ada_b.reshape(depth, 1, n))


def _inproj_kernel(x_ref, sc_ref, sh_ref, g_ref, w_ref, wg_ref, proj_ref, gates_ref, h_scr, *, row_chunk):
    @pl.when(pl.program_id(1) == 0)
    def _():
        def chunk(c, carry):
            rows = pl.ds(pl.multiple_of(c * row_chunk, row_chunk), row_chunk)
            x = x_ref[rows, :]
            ms = jnp.sum(x * x, axis=-1, keepdims=True) * (1.0 / D_MODEL)
            y = x * lax.rsqrt(ms + EPS) * g_ref[...]
            sc = sc_ref[...] if sc_ref.shape[0] == 1 else sc_ref[rows, :]
            sh = sh_ref[...] if sh_ref.shape[0] == 1 else sh_ref[rows, :]
            h_scr[rows, :] = (y * (1.0 + sc) + sh).astype(BF16)
            return carry
        lax.fori_loop(0, x_ref.shape[0] // row_chunk, chunk, 0)
        gates_ref[...] = jnp.dot(h_scr[...], wg_ref[...].astype(BF16), preferred_element_type=F32)

    proj_ref[...] = lax.dot_general(h_scr[...], w_ref[...], (((1,), (1,)), ((), ())),
                                    preferred_element_type=F32)


def _unpack_kernel(x_ref, o_ref, *, depth, k_tiles, tn):
    for l in range(depth):
        for kt in range(k_tiles):
            rows = pl.ds(kt * depth + l, tn, stride=k_tiles * depth)
            o_ref[l, :, kt * LANES:(kt + 1) * LANES] = x_ref[rows, :].astype(BF16)


def _unpack_w_in(w_in):
    depth, d, d_in = w_in.shape
    k_tiles = d // LANES
    tn = LANES
    flat = w_in.reshape(depth, k_tiles, LANES, d_in).transpose(3, 1, 0, 2).reshape(d_in * k_tiles * depth, LANES)
    return pl.pallas_call(
        functools.partial(_unpack_kernel, depth=depth, k_tiles=k_tiles, tn=tn),
        out_shape=jax.ShapeDtypeStruct((depth, D_MAIN, d), BF16),
        grid=(D_MAIN // tn,),
        in_specs=[pl.BlockSpec((tn * k_tiles * depth, LANES), lambda i: (i, 0))],
        out_specs=pl.BlockSpec((depth, tn, d), lambda i: (0, i, 0)),
        compiler_params=_cparams(("arbitrary",)),
        name="unpack_w_in",
    )(flat)


def _inproj(x2d, scale, shift, norm_g, w_in_t, w_gate, layer, tm):
    m = x2d.shape[0]
    groups = scale.shape[0]
    tiles_per_group = m // tm // groups
    tn = 768
    mod_spec = pl.BlockSpec((None, scale.shape[1], D_MODEL), lambda i, j: (i // tiles_per_group, 0, 0))
    return pl.pallas_call(
        functools.partial(_inproj_kernel, row_chunk=min(tm, 256)),
        out_shape=(jax.ShapeDtypeStruct((m, D_MAIN), F32), jax.ShapeDtypeStruct((m, LANES), F32)),
        grid=(m // tm, D_MAIN // tn),
        in_specs=[pl.BlockSpec((tm, D_MODEL), lambda i, j: (i, 0)),
                  mod_spec, mod_spec,
                  pl.BlockSpec((1, D_MODEL), lambda i, j: (0, 0)),
                  pl.BlockSpec((None, tn, D_MODEL), lambda i, j: (layer, j, 0)),
                  pl.BlockSpec((None, D_MODEL, LANES), lambda i, j: (layer, 0, 0))],
        out_specs=(pl.BlockSpec((tm, tn), lambda i, j: (i, j)),
                   pl.BlockSpec((tm, LANES), lambda i, j: (i, 0))),
        scratch_shapes=[pltpu.VMEM((tm, D_MODEL), BF16)],
        compiler_params=_cparams(("arbitrary", "arbitrary")),
        name="inproj",
    )(x2d, scale, shift, norm_g, w_in_t, w_gate)


def _outproj_kernel(ya_ref, yb_ref, yc_ref, x_ref, gate_ref, w_ref, o_ref):
    y = jnp.dot(ya_ref[...], w_ref[0:D_A, :], preferred_element_type=F32)
    y += jnp.dot(yb_ref[...], w_ref[D_A:D_A + D_B, :], preferred_element_type=F32)
    y += jnp.dot(yc_ref[...], w_ref[D_A + D_B:, :], preferred_element_type=F32)
    o_ref[...] = x_ref[...] + gate_ref[...] * y


def _outproj(ya, yb, yc, x2d, gate, w_out_bf, layer, tm):
    m = x2d.shape[0]
    groups = gate.shape[0]
    tiles_per_group = m // tm // groups
    return pl.pallas_call(
        _outproj_kernel,
        out_shape=jax.ShapeDtypeStruct((m, D_MODEL), F32),
        grid=(m // tm,),
        in_specs=[pl.BlockSpec((tm, D_A), lambda i: (i, 0)),
                  pl.BlockSpec((tm, D_B), lambda i: (i, 0)),
                  pl.BlockSpec((tm, D_C), lambda i: (i, 0)),
                  pl.BlockSpec((tm, D_MODEL), lambda i: (i, 0)),
                  pl.BlockSpec((None, gate.shape[1], D_MODEL), lambda i: (i // tiles_per_group, 0, 0)),
                  pl.BlockSpec((None, D_MODEL, D_MODEL), lambda i: (layer, 0, 0))],
        out_specs=pl.BlockSpec((tm, D_MODEL), lambda i: (i, 0)),
        compiler_params=_cparams(("arbitrary",)),
        name="outproj",
    )(ya, yb, yc, x2d, gate, w_out_bf)


def _head_norm(x, g, left):
    x2 = x * x
    s_left = jnp.sum(jnp.where(left, x2, 0.0), axis=-1, keepdims=True)
    s_right = jnp.sum(jnp.where(left, 0.0, x2), axis=-1, keepdims=True)
    ms = jnp.where(left, s_left, s_right) * (1.0 / HEAD_DIM_A)
    return x * lax.rsqrt(ms + EPS) * g


def _attn_prompt_kernel(q_ref, k_ref, v_ref, z_ref, gq_ref, gk_ref, bias_ref,
                        y_ref, pk_ref, pv_ref,
                        qn_s, kp_s, vp_s, ones_s, o_s, l_s, m_s, *, seq, keep, unroll):
    norm_rows = 512
    lane_n = lax.broadcasted_iota(jnp.int32, (norm_rows, LANES), 1)
    left_n = lane_n < HEAD_DIM_A

    kp_s[0:PAD_A, :] = jnp.zeros((PAD_A, LANES), F32)
    vp_s[0:PAD_A, :] = jnp.zeros((PAD_A, LANES), F32)
    row_o = lax.broadcasted_iota(jnp.int32, ones_s.shape, 0)
    lane_o = lax.broadcasted_iota(jnp.int32, ones_s.shape, 1)
    ones_s[...] = jnp.where((row_o < 2 * BLK_A) == (lane_o < HEAD_DIM_A), 1.0, 0.0).astype(BF16)

    def norm_chunk(c, carry):
        r0 = pl.multiple_of(c * norm_rows, norm_rows)
        rows = pl.ds(r0, norm_rows)
        qn_s[rows, :] = _head_norm(q_ref[rows, :], gq_ref[...], left_n) * (HEAD_DIM_A ** -0.5 * LOG2E)
        kp_s[pl.ds(PAD_A + r0, norm_rows), :] = _head_norm(k_ref[rows, :], gk_ref[...], left_n)
        vp_s[pl.ds(PAD_A + r0, norm_rows), :] = v_ref[rows, :]
        return carry
    lax.fori_loop(0, seq // norm_rows, norm_chunk, 0)

    for c in range(keep // LANES):
        src = slice(PAD_A + seq - keep + c * LANES, PAD_A + seq - keep + (c + 1) * LANES)
        pk_ref[:, c * LANES:(c + 1) * LANES] = kp_s[src, :].T
        pv_ref[:, c * LANES:(c + 1) * LANES] = vp_s[src, :].T

    left_b = lax.broadcasted_iota(jnp.int32, (BLK_A, LANES), 1) < HEAD_DIM_A
    left_k = lax.broadcasted_iota(jnp.int32, (2 * BLK_A, LANES), 1) < HEAD_DIM_A

    for p, (win, dil) in enumerate(PATTERNS):
        blocks_per_residue = seq // (dil * BLK_A)

        def block(i, carry, p=p, dil=dil, blocks_per_residue=blocks_per_residue):
            r = i // blocks_per_residue
            n = i % blocks_per_residue
            q_start = r + n * (BLK_A * dil)
            k_start = PAD_A + q_start - BLK_A * dil
            if dil == 1:
                q_rows = pl.ds(pl.multiple_of(q_start, BLK_A), BLK_A)
                k_rows = pl.ds(pl.multiple_of(k_start, BLK_A), 2 * BLK_A)
            else:
                q_rows = pl.ds(q_start, BLK_A, stride=dil)
                k_rows = pl.ds(k_start, 2 * BLK_A, stride=dil)
            first = jnp.where(n == 0, 1, 0)
            q = qn_s[q_rows, :]
            k = kp_s[k_rows, :].astype(BF16)
            v = vp_s[k_rows, :]
            v2 = jnp.concatenate([jnp.where(left_k, v, 0.0), jnp.where(left_k, 0.0, v)], axis=0).astype(BF16)
            rhs = jnp.concatenate([v2, ones_s[...]], axis=1)
            es, ms = [], []
            for h in range(HPAIR):
                qh = jnp.where(left_b, q, 0.0) if h == 0 else jnp.where(left_b, 0.0, q)
                s = _bdot_nt(qh, k) + bias_ref[h, p, first]
                mh = jnp.max(s, axis=-1, keepdims=True)
                es.append(jnp.exp2(s - mh).astype(BF16))
                ms.append(mh)
            res = jnp.dot(jnp.concatenate(es, axis=1), rhs, preferred_element_type=F32)
            o_s[p, q_rows, :] = res[:, :LANES]
            l_s[p, q_rows, :] = res[:, LANES:]
            m_s[p, q_rows, :] = jnp.where(left_b, ms[0], ms[1])
            return carry
        lax.fori_loop(0, seq // BLK_A, block, 0, unroll=unroll)

    def out_chunk(c, carry):
        rows = pl.ds(pl.multiple_of(c * norm_rows, norm_rows), norm_rows)
        m = [m_s[p, rows, :] for p in range(N_PAT)]
        top = functools.reduce(jnp.maximum, m)
        w = [jnp.exp2(x - top) for x in m]
        num = functools.reduce(lambda a, b: a + b, [w[p] * o_s[p, rows, :] for p in range(N_PAT)])
        den = functools.reduce(lambda a, b: a + b, [w[p] * l_s[p, rows, :] for p in range(N_PAT)])
        y_ref[rows, :] = (num / den * _silu(z_ref[rows, :])).astype(BF16)
        return carry
    lax.fori_loop(0, seq // norm_rows, out_chunk, 0)


def _attn_prompt(proj, gq2, gk2, bias_p):
    b, seq, _ = proj.shape
    keep = min(WIN_MAX, seq)
    assert seq % (BLK_A * PATTERNS[-1][1]) == 0
    col = lambda base: pl.BlockSpec((None, seq, LANES), lambda bi, hp: (bi, 0, base + hp))
    vec = pl.BlockSpec((1, LANES), lambda bi, hp: (0, 0))
    return pl.pallas_call(
        functools.partial(_attn_prompt_kernel, seq=seq, keep=keep, unroll=8),
        out_shape=(jax.ShapeDtypeStruct((b, seq, D_A), BF16),
                   jax.ShapeDtypeStruct((b, D_A, keep), F32),
                   jax.ShapeDtypeStruct((b, D_A, keep), F32)),
        grid=(b, H_A // HPAIR),
        in_specs=[col(COL_Q), col(COL_K), col(COL_V), col(COL_Z), vec, vec,
                  pl.BlockSpec((HPAIR, N_PAT, 2, BLK_A, 2 * BLK_A), lambda bi, hp: (hp, 0, 0, 0, 0))],
        out_specs=(pl.BlockSpec((None, seq, LANES), lambda bi, hp: (bi, 0, hp)),
                   pl.BlockSpec((None, LANES, keep), lambda bi, hp: (bi, hp, 0)),
                   pl.BlockSpec((None, LANES, keep), lambda bi, hp: (bi, hp, 0))),
        scratch_shapes=[pltpu.VMEM((seq, LANES), F32),
                        pltpu.VMEM((PAD_A + seq, LANES), F32),
                        pltpu.VMEM((PAD_A + seq, LANES), F32),
                        pltpu.VMEM((2 * HPAIR * BLK_A, LANES), BF16),
                        pltpu.VMEM((N_PAT, seq, LANES), F32),
                        pltpu.VMEM((N_PAT, seq, LANES), F32),
                        pltpu.VMEM((N_PAT, seq, LANES), F32)],
        compiler_params=_cparams(("arbitrary", "arbitrary")),
        name="attn_prompt",
    )(proj, proj, proj, proj, gq2, gk2, bias_p)


def _attn_sample_kernel(q_ref, k_ref, v_ref, z_ref, ck_ref, cv_ref, gq_ref, gk_ref, bias_ref,
                        y_ref, nk_ref, nv_ref, *, n_past, t_new):
    q_all = q_ref[...]
    k_all = k_ref[...]
    v_all = v_ref[...]
    z_all = z_ref[...]
    nv_ref[...] = v_all
    pad_rows = jnp.zeros((LANES - t_new, HEAD_DIM_A), F32)

    def norm(x, g):
        ms = jnp.sum(x * x, axis=-1, keepdims=True) * (1.0 / HEAD_DIM_A)
        return x * lax.rsqrt(ms + EPS) * g

    for h in range(H_A):
        cols = slice(h * HEAD_DIM_A, (h + 1) * HEAD_DIM_A)
        qn = (norm(q_all[:, cols], gq_ref[:, cols]) * (HEAD_DIM_A ** -0.5)).astype(BF16)
        kn = norm(k_all[:, cols], gk_ref[:, cols])
        nk_ref[:, cols] = kn
        k_new = jnp.concatenate([kn, pad_rows], axis=0).astype(BF16)
        v_new = jnp.concatenate([v_all[:, cols], pad_rows], axis=0).astype(BF16)
        k_t = ck_ref[h].astype(BF16)
        v_t = cv_ref[h].astype(BF16)
        s_c = jnp.dot(qn, k_t, preferred_element_type=F32)
        s_n = _bdot_nt(qn, k_new)
        sc = [s_c + bias_ref[h, p, :, 0:n_past] for p in range(N_PAT)]
        sn = [s_n + bias_ref[h, p, :, n_past:] for p in range(N_PAT)]
        m = functools.reduce(jnp.maximum, [jnp.max(x, axis=-1, keepdims=True) for x in sc + sn])
        e_c = functools.reduce(lambda a, b: a + b, [jnp.exp(x - m) for x in sc])
        e_n = functools.reduce(lambda a, b: a + b, [jnp.exp(x - m) for x in sn])
        den = jnp.sum(e_c, axis=-1, keepdims=True) + jnp.sum(e_n, axis=-1, keepdims=True)
        o = _bdot_nt(e_c, v_t) + jnp.dot(e_n.astype(BF16), v_new, preferred_element_type=F32)
        y_ref[:, cols] = (o / den * _silu(z_all[:, cols])).astype(BF16)


def _attn_sample(proj_s, cache_kt, cache_vt, gq, gk, bias_s, layer):
    bd, t_new, _ = proj_s.shape
    n_past = cache_kt.shape[-1]
    col = lambda base: pl.BlockSpec((None, t_new, D_A), lambda bi: (bi, 0, base))
    vec = pl.BlockSpec((1, D_A), lambda bi: (0, 0))
    cache = pl.BlockSpec((None, None, H_A, HEAD_DIM_A, n_past), lambda bi: (layer, bi, 0, 0, 0))
    out = pl.BlockSpec((None, t_new, D_A), lambda bi: (bi, 0, 0))
    return pl.pallas_call(
        functools.partial(_attn_sample_kernel, n_past=n_past, t_new=t_new),
        out_shape=(jax.ShapeDtypeStruct((bd, t_new, D_A), BF16),
                   jax.ShapeDtypeStruct((bd, t_new, D_A), F32),
                   jax.ShapeDtypeStruct((bd, t_new, D_A), F32)),
        grid=(bd,),
        in_specs=[col(0), col(1), col(2), col(3), cache, cache, vec, vec,
                  pl.BlockSpec((H_A, N_PAT, t_new, n_past + LANES), lambda bi: (0, 0, 0, 0))],
        out_specs=(out, out, out),
        compiler_params=_cparams(("arbitrary",)),
        name="attn_sample",
    )(proj_s, proj_s, proj_s, proj_s, cache_kt, cache_vt, gq, gk, bias_s)


def _sgu_prompt_kernel(u_ref, v_ref, z_ref, g_ref, w_ref, bt_ref, y_ref, *, chunks):
    row = lax.broadcasted_iota(jnp.int32, (CHUNK_B, CHUNK_B), 0)
    colm = lax.broadcasted_iota(jnp.int32, (CHUNK_B, CHUNK_B), 1)
    tril = row >= colm
    ws = [jnp.where(tril, w_ref[g], 0.0).astype(BF16) for g in range(G_B)]
    for c in range(chunks):
        rows = slice(c * CHUNK_B, (c + 1) * CHUNK_B)
        v = v_ref[rows, :]
        ms = jnp.sum(v * v, axis=-1, keepdims=True) * (1.0 / D_B)
        vn = (v * lax.rsqrt(ms + EPS) * g_ref[...]).astype(BF16)
        for g in range(G_B):
            cols = slice(g * C_B, (g + 1) * C_B)
            mix = jnp.dot(ws[g], vn[:, cols], preferred_element_type=F32) + bt_ref[:, g:g + 1]
            y_ref[rows, cols] = (u_ref[rows, cols] * mix * _silu(z_ref[rows, cols])).astype(BF16)


def _sgu_prompt(proj, sgu_g, sgu_w, sgu_bt):
    b, seq, _ = proj.shape
    ts = 512
    col = lambda base: pl.BlockSpec((None, ts, D_B), lambda bi, i: (bi, i, base))
    return pl.pallas_call(
        functools.partial(_sgu_prompt_kernel, chunks=ts // CHUNK_B),
        out_shape=jax.ShapeDtypeStruct((b, seq, D_B), BF16),
        grid=(b, seq // ts),
        in_specs=[col(COL_UB), col(COL_VB), col(COL_ZB),
                  pl.BlockSpec((1, D_B), lambda bi, i: (0, 0)),
                  pl.BlockSpec((G_B, CHUNK_B, CHUNK_B), lambda bi, i: (0, 0, 0)),
                  pl.BlockSpec((CHUNK_B, G_B), lambda bi, i: (0, 0))],
        out_specs=pl.BlockSpec((None, ts, D_B), lambda bi, i: (bi, i, 0)),
        compiler_params=_cparams(("arbitrary", "arbitrary")),
        name="sgu_prompt",
    )(proj, proj, proj, sgu_g, sgu_w, sgu_bt)


def _sgu_sample_kernel(u_ref, v_ref, z_ref, g_ref, wexp_ref, bexp_ref, y_ref, vn_ref, *, t_new):
    v = v_ref[...]
    ms = jnp.sum(v * v, axis=-1, keepdims=True) * (1.0 / D_B)
    vn = v * lax.rsqrt(ms + EPS) * g_ref[...]
    vn_ref[...] = vn
    mix = bexp_ref[...]
    for s in range(t_new):
        mix = mix + wexp_ref[s] * vn[s:s + 1, :]
    y_ref[...] = (u_ref[...] * mix * _silu(z_ref[...])).astype(BF16)


def _sgu_sample(proj_s, sgu_g, wexp, bexp):
    bd, t_new, _ = proj_s.shape
    col = lambda base: pl.BlockSpec((None, t_new, D_B), lambda bi: (bi, 0, base))
    return pl.pallas_call(
        functools.partial(_sgu_sample_kernel, t_new=t_new),
        out_shape=(jax.ShapeDtypeStruct((bd, t_new, D_B), BF16),
                   jax.ShapeDtypeStruct((bd, t_new, D_B), F32)),
        grid=(bd,),
        in_specs=[col(COL_UB), col(COL_VB), col(COL_ZB),
                  pl.BlockSpec((1, D_B), lambda bi: (0, 0)),
                  pl.BlockSpec((t_new, t_new, D_B), lambda bi: (0, 0, 0)),
                  pl.BlockSpec((t_new, D_B), lambda bi: (0, 0))],
        out_specs=(pl.BlockSpec((None, t_new, D_B), lambda bi: (bi, 0, 0)),
                   pl.BlockSpec((None, t_new, D_B), lambda bi: (bi, 0, 0))),
        compiler_params=_cparams(("arbitrary",)),
        name="sgu_sample",
    )(proj_s, proj_s, proj_s, sgu_g, wexp, bexp)


def _log_sigmoid(x):
    return jnp.minimum(x, 0.0) - jnp.log1p(jnp.exp(-jnp.abs(x)))


def _mlstm_kernel(qk_ref, v_ref, o_ref, z_ref, gt_ref, cbuf_ref, c0_ref, n0_ref, m0_ref,
                  cw_ref, cb_ref, gb_ref, hn_ref,
                  y_ref, conv_out_ref, c_out_ref, n_out_ref, m_out_ref,
                  xp_s, vp_s, gp_s, caug_s, m_s, *, t_rows):
    L = CHUNK_C
    ci = pl.program_id(1)
    last = pl.num_programs(1) - 1
    base = SUBLANES
    carry_rows = CONV_W - 1
    row_l = lax.broadcasted_iota(jnp.int32, (L, LANES), 0)
    lane_l = lax.broadcasted_iota(jnp.int32, (L, LANES), 1)
    zero_rows = jnp.zeros((L - SUBLANES, LANES), F32)

    def to_cols(x8):
        return jnp.concatenate([x8, zero_rows], axis=0).T

    @pl.when(ci == 0)
    def _():
        xp_s[...] = jnp.zeros(xp_s.shape, F32)
        vp_s[...] = jnp.zeros(vp_s.shape, F32)
        gp_s[...] = jnp.zeros(gp_s.shape, F32)
        xp_s[base - carry_rows:base, :] = cbuf_ref[...]
        n0_cols = to_cols(n0_ref[...])
        for h in range(H_C):
            caug_s[h, :, 0:DH_C] = c0_ref[h]
            caug_s[h, :, DH_C:] = jnp.broadcast_to(n0_cols[:, h:h + 1], (DH_C, DH_C))
        m_s[...] = m0_ref[...]

    xp_s[base:base + t_rows, :] = qk_ref[...]
    vp_s[0:t_rows, :] = v_ref[...]
    gp_s[0:t_rows, :] = gt_ref[...]

    acc = cb_ref[...] + cw_ref[CONV_W - 1:CONV_W, :] * xp_s[base:base + L, :]
    for j in range(CONV_W - 1):
        off = base - carry_rows + j
        acc = acc + cw_ref[j:j + 1, :] * xp_s[off:off + L, :]
    qk = _silu(acc)
    new_carry = xp_s[base + t_rows - carry_rows:base + t_rows, :]
    xp_s[base - carry_rows:base, :] = new_carry

    gates_t = (gp_s[...] + gb_ref[...]).T
    lane_t = lax.broadcasted_iota(jnp.int32, (SUBLANES, L), 1)
    real_t = lane_t < t_rows
    i_t = jnp.where(real_t, gates_t[0:SUBLANES, :], NEG)
    logf_t = jnp.where(real_t, _log_sigmoid(gates_t[SUBLANES:2 * SUBLANES, :]), 0.0)
    upper = (row_l <= lane_l).astype(BF16)
    f1 = logf_t.astype(BF16)
    r1 = logf_t - f1.astype(F32)
    f2 = r1.astype(BF16)
    f3 = (r1 - f2.astype(F32)).astype(BF16)
    b_t = (jnp.dot(f1, upper, preferred_element_type=F32) + jnp.dot(f2, upper, preferred_element_type=F32)
           + jnp.dot(f3, upper, preferred_element_type=F32))
    m_prev = m_s[...]
    c_t = i_t - b_t
    pm = c_t
    shift = 1
    while shift < L:
        pm = jnp.maximum(pm, jnp.where(lane_t >= shift, pltpu.roll(pm, shift, axis=1), NEG))
        shift *= 2
    d_t = -jnp.maximum(m_prev, pm)
    wi_t = jnp.exp(m_prev + d_t)
    em_t = jnp.exp(-(b_t - d_t))
    bl = jnp.broadcast_to(b_t[:, L - 1:L], (SUBLANES, L))
    g_t = bl - b_t + i_t
    m_new = jnp.maximum(bl + m_prev, jnp.broadcast_to(jnp.max(g_t, axis=1, keepdims=True), (SUBLANES, L)))
    ws_t = jnp.exp(g_t - m_new)
    wc = jnp.exp(bl + m_prev - m_new)
    m_s[...] = m_new
    stats = jnp.concatenate([d_t, wi_t, em_t, ws_t, zero_rows[0:L - 4 * SUBLANES]], axis=0).T

    causal = row_l >= lane_l
    ones_blk = jnp.ones((L, DH_C), BF16)
    v_all = vp_s[...]
    o_all = o_ref[...]
    z_all = z_ref[...]

    for h in range(H_C):
        cols = slice(h * DH_C, (h + 1) * DH_C)
        q = qk[:, cols].astype(BF16)
        k = qk[:, D_C + h * DH_C:D_C + (h + 1) * DH_C] * (DH_C ** -0.5)
        v_aug = jnp.concatenate([v_all[:, cols].astype(BF16), ones_blk], axis=1)
        d_col = stats[:, h:h + 1]
        wi_col = stats[:, SUBLANES + h:SUBLANES + h + 1]
        em_col = stats[:, 2 * SUBLANES + h:2 * SUBLANES + h + 1]
        ws_col = stats[:, 3 * SUBLANES + h:3 * SUBLANES + h + 1]
        c_prev = caug_s[h]

        a = jnp.exp(jnp.where(causal, d_col + c_t[h:h + 1, :], NEG)) * _bdot_nt(q, k)
        r1_ = jnp.dot(a.astype(BF16), v_aug, preferred_element_type=F32)
        r2_ = jnp.dot(q, c_prev.astype(BF16), preferred_element_type=F32)
        num = r1_[:, 0:DH_C] + wi_col * r2_[:, 0:DH_C]
        den = r1_[:, DH_C:] + wi_col * r2_[:, DH_C:]
        hh = num / jnp.maximum(jnp.abs(den), em_col)

        wc_row = jnp.concatenate([wc[h:h + 1, :], wc[h:h + 1, :]], axis=1)
        caug_s[h] = wc_row * c_prev + _bdot_tn(k * ws_col, v_aug)

        ms = jnp.sum(hh * hh, axis=-1, keepdims=True) * (1.0 / DH_C)
        hn = hh * lax.rsqrt(ms + EPS) * hn_ref[:, cols]
        yv = hn[0:t_rows, :] * jax.nn.sigmoid(o_all[:, cols]) * _silu(z_all[:, cols])
        y_ref[:, cols] = yv.astype(BF16)

    @pl.when(ci == last)
    def _():
        conv_out_ref[...] = new_carry
        n_cols = jnp.zeros((DH_C, LANES), F32)
        for h in range(H_C):
            c_out_ref[h] = caug_s[h, :, 0:DH_C]
            n_cols = jnp.where(lane_l == h, caug_s[h, :, DH_C:], n_cols)
        n_out_ref[...] = n_cols.T[0:SUBLANES, :]
        m_out_ref[...] = m_s[...]


def _mlstm(proj, gates, conv_buf, c0, n0, m0, conv_w, conv_b, gate_bias, hn_g, t_rows):
    b, seq, _ = proj.shape
    nchunks = seq // t_rows
    hp = SUBLANES
    col = lambda base, width: pl.BlockSpec((None, t_rows, width), lambda bi, ci: (bi, ci, base))
    const2 = lambda shape: pl.BlockSpec(shape, lambda bi, ci: (0, 0))
    per_b3 = lambda shape: pl.BlockSpec((None,) + shape, lambda bi, ci: (bi, 0, 0))
    per_b4 = lambda shape: pl.BlockSpec((None,) + shape, lambda bi, ci: (bi, 0, 0, 0))
    return pl.pallas_call(
        functools.partial(_mlstm_kernel, t_rows=t_rows),
        out_shape=(jax.ShapeDtypeStruct((b, seq, D_C), BF16),
                   jax.ShapeDtypeStruct((b, CONV_W - 1, 2 * D_C), F32),
                   jax.ShapeDtypeStruct((b, H_C, DH_C, DH_C), F32),
                   jax.ShapeDtypeStruct((b, hp, DH_C), F32),
                   jax.ShapeDtypeStruct((b, hp, LANES), F32)),
        grid=(b, nchunks),
        in_specs=[col(COL_QK, 2 * D_C), col(COL_VC, D_C), col(COL_OC, D_C), col(COL_ZC, D_C),
                  pl.BlockSpec((None, t_rows, LANES), lambda bi, ci: (bi, ci, 0)),
                  per_b3((CONV_W - 1, 2 * D_C)), per_b4((H_C, DH_C, DH_C)), per_b3((hp, DH_C)),
                  per_b3((hp, LANES)),
                  const2((CONV_W, 2 * D_C)), const2((1, 2 * D_C)), const2((1, LANES)), const2((1, D_C))],
        out_specs=(pl.BlockSpec((None, t_rows, D_C), lambda bi, ci: (bi, ci, 0)),
                   per_b3((CONV_W - 1, 2 * D_C)), per_b4((H_C, DH_C, DH_C)), per_b3((hp, DH_C)),
                   per_b3((hp, LANES))),
        scratch_shapes=[pltpu.VMEM((SUBLANES + CHUNK_C, 2 * D_C), F32),
                        pltpu.VMEM((CHUNK_C, D_C), F32),
                        pltpu.VMEM((CHUNK_C, LANES), F32),
                        pltpu.VMEM((H_C, DH_C, 2 * DH_C), F32),
                        pltpu.VMEM((hp, LANES), F32)],
        compiler_params=_cparams(("arbitrary", "arbitrary")),
        name="mlstm",
    )(proj, proj, proj, proj, gates, conv_buf, c0, n0, m0, conv_w, conv_b, gate_bias, hn_g)


def _gate_lanes(i_part, f_part):
    lead = i_part.shape[:-1]
    gap = jnp.zeros(lead + (SUBLANES - H_C,), i_part.dtype)
    tail = jnp.zeros(lead + (LANES - 2 * SUBLANES,), i_part.dtype)
    return jnp.concatenate([i_part, gap, f_part, gap, tail], axis=-1)


def _pad_heads(a):
    pad = [(0, 0)] * a.ndim
    pad[1] = (0, SUBLANES - a.shape[1])
    return jnp.pad(a, pad)


def _mixers(proj, gates, lw, attn_fn, sgu_fn, conv_buf, c0, n0, m0, t_rows):
    ya, nk, nv = attn_fn(proj)
    sgu_out = sgu_fn(proj)
    m0b = jnp.broadcast_to(_pad_heads(m0)[:, :, None], (m0.shape[0], SUBLANES, LANES))
    yc, nconv, c_new, n_new, m_new = _mlstm(proj, gates, conv_buf, c0, _pad_heads(n0), m0b,
                                            lw["conv_w"], lw["conv_b"], lw["gate_bias"], lw["hn_g"], t_rows)
    return ya, sgu_out, yc, nk, nv, nconv, c_new, n_new[:, :H_C], m_new[:, :H_C, 0]


def kernel(x_prompt, x_sample, c_prompt, c_sample, cache_k_win, cache_v_win, state_conv, state_C, state_n, state_m, rel_bias, norm_g, ada_w, ada_b, w_in, qn_g, kn_g, sgu_g, sgu_w, sgu_b, conv_w, conv_b, f_bias, i_bias, hn_g, w_out):
    depth = w_in.shape[0]
    bp, seq, _ = x_prompt.shape
    bd, t_new, _ = x_sample.shape
    n_past = cache_k_win.shape[2]
    assert t_new == SUBLANES and n_past % LANES == 0 and seq % 1024 == 0

    w_in_t = _unpack_w_in(w_in)
    w_gate = _gate_lanes(w_in[:, :, D_MAIN:D_MAIN + H_C], w_in[:, :, D_MAIN + H_C:])
    w_out_bf = w_out.astype(BF16)

    rel_t = rel_bias.T
    bias_p = _expand_bias(rel_t, jnp.asarray(_prompt_bucket_table()), 64, LOG2E)
    bias_p = bias_p.reshape(H_A, N_PAT, 2, BLK_A, 2 * BLK_A)
    bias_s = _expand_bias(rel_t, jnp.asarray(_sample_bucket_table(n_past, t_new)), SUBLANES, 1.0)
    bias_s = bias_s.reshape(H_A, N_PAT, t_new, n_past + LANES)

    n_c = bp + bd
    c_all = jnp.pad(jnp.concatenate([c_prompt, c_sample], axis=0), ((0, 2 * SUBLANES - n_c), (0, 0)))
    mod = _ada_mod(c_all, ada_w, ada_b)

    cache_kt = jnp.transpose(cache_k_win, (0, 1, 3, 4, 2))
    cache_vt = jnp.transpose(cache_v_win, (0, 1, 3, 4, 2))
    gq_all = [jnp.tile(qn_g[l], H_A)[None] for l in range(depth)]
    gk_all = [jnp.tile(kn_g[l], H_A)[None] for l in range(depth)]
    zeros_conv = jnp.zeros((bp, CONV_W - 1, 2 * D_C), F32)
    zeros_c = jnp.zeros((bp, H_C, DH_C, DH_C), F32)
    zeros_n = jnp.zeros((bp, H_C, DH_C), F32)
    zeros_m = jnp.zeros((bp, H_C), F32)
    tril8 = jnp.tril(jnp.ones((t_new, t_new), F32))

    xp = x_prompt.reshape(bp * seq, D_MODEL)
    xs = x_sample.reshape(bd * t_new, D_MODEL)
    outs = [[] for _ in range(13)]
    for l in range(depth):
        shift, scale, gate = (mod[l, :, i * D_MODEL:(i + 1) * D_MODEL] for i in range(3))
        p_mod = [a[:bp, None, :] for a in (scale, shift, gate)]
        s_mod = [jnp.repeat(a[bp:n_c], t_new, axis=0)[None] for a in (scale, shift, gate)]
        lw = {
            "conv_w": conv_w[l], "conv_b": conv_b[l][None],
            "gate_bias": _gate_lanes(i_bias[l], f_bias[l])[None],
            "hn_g": hn_g[l][None],
        }
        gq2 = jnp.tile(qn_g[l], HPAIR)[None]
        gk2 = jnp.tile(kn_g[l], HPAIR)[None]
        ng = norm_g[l][None]
        sg = sgu_g[l][None]

        proj, gates = _inproj(xp, p_mod[0], p_mod[1], ng, w_in_t, w_gate, l, 1024)
        proj = proj.reshape(bp, seq, D_MAIN)
        gates = gates.reshape(bp, seq, LANES)
        ya, yb, yc, nk, nv, ncv, nc_, nn_, nm = _mixers(
            proj, gates, lw,
            lambda pr: _attn_prompt(pr, gq2, gk2, bias_p),
            lambda pr: _sgu_prompt(pr, sg, sgu_w[l], sgu_b[l].T),
            zeros_conv, zeros_c, zeros_n, zeros_m, CHUNK_C)
        xp = _outproj(ya.reshape(bp * seq, D_A), yb.reshape(bp * seq, D_B), yc.reshape(bp * seq, D_C),
                      xp, p_mod[2], w_out_bf, l, 512)
        keep = nk.shape[2]
        for i, a in enumerate((nk.reshape(bp, H_A, HEAD_DIM_A, keep), nv.reshape(bp, H_A, HEAD_DIM_A, keep),
                               ncv, nc_, nn_, nm)):
            outs[i].append(a)

        proj_s, gates_s = _inproj(xs, s_mod[0], s_mod[1], ng, w_in_t, w_gate, l, bd * t_new)
        proj_s = proj_s.reshape(bd, t_new, D_MAIN)
        gates_s = gates_s.reshape(bd, t_new, LANES)
        w8 = sgu_w[l][:, :t_new, :t_new] * tril8
        wexp = jnp.repeat(jnp.transpose(w8, (2, 1, 0)), C_B, axis=2)
        bexp = jnp.repeat(sgu_b[l][:, :t_new].T, C_B, axis=1)
        ya, sgu_out, yc, nk, nv, ncv, nc_, nn_, nm = _mixers(
            proj_s, gates_s, lw,
            lambda pr: _attn_sample(pr, cache_kt, cache_vt, gq_all[l], gk_all[l], bias_s, l),
            lambda pr: _sgu_sample(pr, sg, wexp, bexp),
            state_conv[l], state_C[l], state_n[l], state_m[l], t_new)
        yb, vn = sgu_out
        xs = _outproj(ya.reshape(bd * t_new, D_A), yb.reshape(bd * t_new, D_B), yc.reshape(bd * t_new, D_C),
                      xs, s_mod[2], w_out_bf, l, bd * t_new)
        for i, a in enumerate((nk.reshape(bd, t_new, H_A, HEAD_DIM_A), nv.reshape(bd, t_new, H_A, HEAD_DIM_A),
                               vn, ncv, nc_, nn_, nm)):
            outs[6 + i].append(a)

    stacked = [jnp.stack(o) for o in outs]
    for i in (0, 1):
        stacked[i] = jnp.transpose(stacked[i], (0, 1, 4, 2, 3))
    return (xp.reshape(bp, seq, D_MODEL), xs.reshape(bd, t_new, D_MODEL), *stacked)
```

```python
import functools
import math

import numpy as np
import jax
import jax.numpy as jnp
from jax import lax
from jax.experimental import pallas as pl
from jax.experimental.pallas import tpu as pltpu

F32 = jnp.float32
BF16 = jnp.bfloat16

D_MODEL = 2048
HEAD_DIM_A = 64
D_A = 768
H_A = 12
D_B = 512
G_B = 4
C_B = 128
CHUNK_B = 128
D_C = 768
DH_C = 128
H_C = 6
CHUNK_C = 128
CONV_W = 4
PATTERNS = ((128, 1), (512, 4), (2048, 16))
N_PAT = len(PATTERNS)
WIN_MAX = 2048
BLK_A = 128
N_BUCKETS = 32
MAX_DIST = 2048
EPS = 1e-6
D_MAIN = 4 * D_A + 3 * D_B + 2 * D_C + 3 * D_C
D_IN = D_MAIN + 2 * H_C

LANES = 128
SUBLANES = 8
VMEM_LIMIT = 56 * 1024 * 1024

NEG = -1e30
LOG2E = 1.4426950408889634
PAD_A = BLK_A * PATTERNS[-1][1]
HPAIR = LANES // HEAD_DIM_A

COL_Q, COL_K, COL_V, COL_Z = 0, D_A // LANES, 2 * D_A // LANES, 3 * D_A // LANES
COL_UB, COL_VB, COL_ZB = 4 * D_A // D_B, 4 * D_A // D_B + 1, 4 * D_A // D_B + 2
COL_QK = (4 * D_A + 3 * D_B) // (2 * D_C)
COL_VC = (4 * D_A + 3 * D_B + 2 * D_C) // D_C
COL_OC, COL_ZC = COL_VC + 1, COL_VC + 2


def _cparams(sem):
    return pltpu.CompilerParams(dimension_semantics=sem, vmem_limit_bytes=VMEM_LIMIT)


def _silu(x):
    return x * jax.nn.sigmoid(x)


def _bdot(a, b):
    return jnp.dot(a.astype(BF16), b.astype(BF16), preferred_element_type=F32)


def _bdot_nt(a, b):
    return lax.dot_general(a.astype(BF16), b.astype(BF16), (((1,), (1,)), ((), ())),
                           preferred_element_type=F32)


def _bdot_tn(a, b):
    return lax.dot_general(a.astype(BF16), b.astype(BF16), (((0,), (0,)), ((), ())),
                           preferred_element_type=F32)


def _bucket_np(dist):
    max_exact = N_BUCKETS // 2
    df = np.maximum(dist, 1).astype(np.float32)
    large = max_exact + (np.log(df / np.float32(max_exact)) / np.float32(math.log(MAX_DIST / max_exact))
                         * np.float32(N_BUCKETS - max_exact)).astype(np.int32)
    return np.where(dist < max_exact, dist, np.minimum(large, N_BUCKETS - 1)).astype(np.int32)


def _prompt_bucket_table():
    qi = np.arange(BLK_A)[:, None]
    ki = np.arange(2 * BLK_A)[None, :]
    j = qi + BLK_A - ki
    out = []
    for win, dil in PATTERNS:
        n_back = win // dil
        band = (j >= 0) & (j <= n_back)
        b = _bucket_np(np.clip(j, 0, n_back) * dil)
        out.append(np.where(band, b, -1))
        out.append(np.where(band & (ki >= BLK_A), b, -1))
    return np.stack(out).reshape(N_PAT * 2 * BLK_A, 2 * BLK_A).astype(np.int32)


def _sample_bucket_table(n_past, t_new):
    c = np.arange(n_past + LANES)[None, :]
    t = np.arange(t_new)[:, None]
    delta = n_past + t - c
    out = []
    for win, dil in PATTERNS:
        valid = (c < n_past + t_new) & (delta >= 0) & (delta % dil == 0) & (delta // dil <= win // dil)
        out.append(np.where(valid, _bucket_np(np.maximum(delta, 0)), -1))
    return np.stack(out).reshape(N_PAT * t_new, n_past + LANES).astype(np.int32)


def _bias_kernel(rb_ref, idx_ref, out_ref, *, scale, row_chunk):
    h = pl.program_id(0)

    def chunk(c, carry):
        rows = pl.ds(pl.multiple_of(c * row_chunk, row_chunk), row_chunk)
        idx = idx_ref[rows, :]
        out = jnp.full(idx.shape, NEG, F32)
        for b in range(N_BUCKETS):
            out = jnp.where(idx == b, rb_ref[h, b] * scale, out)
        out_ref[rows, :] = out
        return carry
    lax.fori_loop(0, idx_ref.shape[0] // row_chunk, chunk, 0)


def _expand_bias(rel_bias_t, idx, row_chunk, scale):
    rows, cols = idx.shape
    return pl.pallas_call(
        functools.partial(_bias_kernel, scale=scale, row_chunk=row_chunk),
        out_shape=jax.ShapeDtypeStruct((H_A, rows, cols), F32),
        grid=(H_A,),
        in_specs=[pl.BlockSpec(memory_space=pltpu.SMEM),
                  pl.BlockSpec((rows, cols), lambda h: (0, 0))],
        out_specs=pl.BlockSpec((None, rows, cols), lambda h: (h, 0, 0)),
        compiler_params=_cparams(("arbitrary",)),
        name="bias_expand",
    )(rel_bias_t, idx)


def _ada_kernel(c_ref, w_ref, b_ref, o_ref):
    c = c_ref[...]
    a = _silu(c)
    w = w_ref[...]
    a_hi = a.astype(BF16)
    a_lo = (a - a_hi.astype(F32)).astype(BF16)
    w_hi = w.astype(BF16)
    w_lo = (w - w_hi.astype(F32)).astype(BF16)
    acc = jnp.dot(a_hi, w_hi, preferred_element_type=F32)
    acc += jnp.dot(a_hi, w_lo, preferred_element_type=F32)
    acc += jnp.dot(a_lo, w_hi, preferred_element_type=F32)
    o_ref[...] = acc + b_ref[...]


def _ada_mod(c_all, ada_w, ada_b):
    depth = ada_w.shape[0]
    rows = c_all.shape[0]
    tn = 768
    n = 3 * D_MODEL
    return pl.pallas_call(
        _ada_kernel,
        out_shape=jax.ShapeDtypeStruct((depth, rows, n), F32),
        grid=(depth, n // tn),
        in_specs=[pl.BlockSpec((rows, D_MODEL), lambda l, j: (0, 0)),
                  pl.BlockSpec((None, D_MODEL, tn), lambda l, j: (l, 0, j)),
                  pl.BlockSpec((None, 1, tn), lambda l, j: (l, 0, j))],
        out_specs=pl.BlockSpec((None, rows, tn), lambda l, j: (l, 0, j)),
        compiler_params=_cparams(("arbitrary", "arbitrary")),
        name="ada_mod",
    )(c_all, ada_w, ada_b.reshape(depth, 1, n))


def _norm_mod(x, g, sc, sh):
    ms = jnp.sum(x * x, axis=-1, keepdims=True) * (1.0 / D_MODEL)
    return ((x * lax.rsqrt(ms + EPS) * g) * (1.0 + sc) + sh).astype(BF16)


def _norm_kernel(x_ref, sc_ref, sh_ref, g_ref, h_ref):
    h_ref[...] = _norm_mod(x_ref[...], g_ref[...], sc_ref[...], sh_ref[...])


def _norm(x2d, scale, shift, norm_g, tm):
    m = x2d.shape[0]
    tiles_per_group = m // tm // scale.shape[0]
    mod_spec = pl.BlockSpec((None, scale.shape[1], D_MODEL), lambda i: (i // tiles_per_group, 0, 0))
    return pl.pallas_call(
        _norm_kernel,
        out_shape=jax.ShapeDtypeStruct((m, D_MODEL), BF16),
        grid=(m // tm,),
        in_specs=[pl.BlockSpec((tm, D_MODEL), lambda i: (i, 0)), mod_spec, mod_spec,
                  pl.BlockSpec((1, D_MODEL), lambda i: (0, 0))],
        out_specs=pl.BlockSpec((tm, D_MODEL), lambda i: (i, 0)),
        compiler_params=_cparams(("arbitrary",)),
        name="norm",
    )(x2d, scale, shift, norm_g)


def _inproj_kernel(h_ref, w_ref, wg_ref, proj_ref, gates_ref):
    @pl.when(pl.program_id(1) == 0)
    def _():
        gates_ref[...] = jnp.dot(h_ref[...], wg_ref[...].astype(BF16), preferred_element_type=F32)

    proj_ref[...] = lax.dot_general(h_ref[...], w_ref[...], (((1,), (1,)), ((), ())),
                                    preferred_element_type=F32)


def _unpack_kernel(x_ref, o_ref, *, depth, k_tiles, tn):
    for l in range(depth):
        for kt in range(k_tiles):
            rows = pl.ds(kt * depth + l, tn, stride=k_tiles * depth)
            o_ref[l, :, kt * LANES:(kt + 1) * LANES] = x_ref[rows, :].astype(BF16)


def _unpack_w_in(w_in):
    depth, d, d_in = w_in.shape
    k_tiles = d // LANES
    tn = LANES
    flat = w_in.reshape(depth, k_tiles, LANES, d_in).transpose(3, 1, 0, 2).reshape(d_in * k_tiles * depth, LANES)
    return pl.pallas_call(
        functools.partial(_unpack_kernel, depth=depth, k_tiles=k_tiles, tn=tn),
        out_shape=jax.ShapeDtypeStruct((depth, D_MAIN, d), BF16),
        grid=(D_MAIN // tn,),
        in_specs=[pl.BlockSpec((tn * k_tiles * depth, LANES), lambda i: (i, 0))],
        out_specs=pl.BlockSpec((depth, tn, d), lambda i: (0, i, 0)),
        compiler_params=_cparams(("arbitrary",)),
        name="unpack_w_in",
    )(flat)


def _inproj(h2d, w_in_t, w_gate, layer, tm):
    m = h2d.shape[0]
    tn = 768
    return pl.pallas_call(
        _inproj_kernel,
        out_shape=(jax.ShapeDtypeStruct((m, D_MAIN), F32), jax.ShapeDtypeStruct((m, LANES), F32)),
        grid=(m // tm, D_MAIN // tn),
        in_specs=[pl.BlockSpec((tm, D_MODEL), lambda i, j: (i, 0)),
                  pl.BlockSpec((None, tn, D_MODEL), lambda i, j: (layer, j, 0)),
                  pl.BlockSpec((None, D_MODEL, LANES), lambda i, j: (layer, 0, 0))],
        out_specs=(pl.BlockSpec((tm, tn), lambda i, j: (i, j)),
                   pl.BlockSpec((tm, LANES), lambda i, j: (i, 0))),
        compiler_params=_cparams(("arbitrary", "arbitrary")),
        name="inproj",
    )(h2d, w_in_t, w_gate)


def _outproj_kernel(ya_ref, yb_ref, yc_ref, x_ref, gate_ref, w_ref, *rest, emit_next):
    y = jnp.dot(ya_ref[...], w_ref[0:D_A, :], preferred_element_type=F32)
    y += jnp.dot(yb_ref[...], w_ref[D_A:D_A + D_B, :], preferred_element_type=F32)
    y += jnp.dot(yc_ref[...], w_ref[D_A + D_B:, :], preferred_element_type=F32)
    x_new = x_ref[...] + gate_ref[...] * y
    if emit_next:
        sc_ref, sh_ref, g_ref, o_ref, h_ref = rest
        h_ref[...] = _norm_mod(x_new, g_ref[...], sc_ref[...], sh_ref[...])
    else:
        (o_ref,) = rest
    o_ref[...] = x_new


def _outproj(ya, yb, yc, x2d, gate, w_out_bf, layer, tm, next_norm=None):
    m = x2d.shape[0]
    tiles_per_group = m // tm // gate.shape[0]
    mod_spec = pl.BlockSpec((None, gate.shape[1], D_MODEL), lambda i: (i // tiles_per_group, 0, 0))
    row = lambda width: pl.BlockSpec((tm, width), lambda i: (i, 0))
    in_specs = [row(D_A), row(D_B), row(D_C), row(D_MODEL), mod_spec,
                pl.BlockSpec((None, D_MODEL, D_MODEL), lambda i: (layer, 0, 0))]
    args = [ya, yb, yc, x2d, gate, w_out_bf]
    out_shape = [jax.ShapeDtypeStruct((m, D_MODEL), F32)]
    out_specs = [row(D_MODEL)]
    if next_norm is not None:
        in_specs += [mod_spec, mod_spec, pl.BlockSpec((1, D_MODEL), lambda i: (0, 0))]
        args += list(next_norm)
        out_shape.append(jax.ShapeDtypeStruct((m, D_MODEL), BF16))
        out_specs.append(row(D_MODEL))
    out = pl.pallas_call(
        functools.partial(_outproj_kernel, emit_next=next_norm is not None),
        out_shape=tuple(out_shape),
        grid=(m // tm,),
        in_specs=in_specs,
        out_specs=tuple(out_specs),
        compiler_params=_cparams(("arbitrary",)),
        name="outproj",
    )(*args)
    return out if next_norm is not None else (out[0], None)


def _head_norm(x, g, left):
    x2 = x * x
    s_left = jnp.sum(jnp.where(left, x2, 0.0), axis=-1, keepdims=True)
    s_right = jnp.sum(jnp.where(left, 0.0, x2), axis=-1, keepdims=True)
    ms = jnp.where(left, s_left, s_right) * (1.0 / HEAD_DIM_A)
    return x * lax.rsqrt(ms + EPS) * g


def _attn_prompt_kernel(q_ref, k_ref, v_ref, z_ref, gq_ref, gk_ref, bias_ref,
                        y_ref, pk_ref, pv_ref,
                        qn_s, kp_s, vp_s, ones_s, o_s, l_s, m_s, *, seq, keep, unroll):
    norm_rows = 512
    lane_n = lax.broadcasted_iota(jnp.int32, (norm_rows, LANES), 1)
    left_n = lane_n < HEAD_DIM_A

    kp_s[0:PAD_A, :] = jnp.zeros((PAD_A, LANES), F32)
    vp_s[0:PAD_A, :] = jnp.zeros((PAD_A, LANES), F32)
    ones_s[...] = jnp.ones(ones_s.shape, BF16)

    def norm_chunk(c, carry):
        r0 = pl.multiple_of(c * norm_rows, norm_rows)
        rows = pl.ds(r0, norm_rows)
        qn_s[rows, :] = _head_norm(q_ref[rows, :], gq_ref[...], left_n) * (HEAD_DIM_A ** -0.5 * LOG2E)
        kp_s[pl.ds(PAD_A + r0, norm_rows), :] = _head_norm(k_ref[rows, :], gk_ref[...], left_n)
        vp_s[pl.ds(PAD_A + r0, norm_rows), :] = v_ref[rows, :]
        return carry
    lax.fori_loop(0, seq // norm_rows, norm_chunk, 0)

    for c in range(keep // LANES):
        src = slice(PAD_A + seq - keep + c * LANES, PAD_A + seq - keep + (c + 1) * LANES)
        pk_ref[:, c * LANES:(c + 1) * LANES] = kp_s[src, :].T
        pv_ref[:, c * LANES:(c + 1) * LANES] = vp_s[src, :].T

    left_b = lax.broadcasted_iota(jnp.int32, (BLK_A, LANES), 1) < HEAD_DIM_A
    left_k = lax.broadcasted_iota(jnp.int32, (2 * BLK_A, LANES), 1) < HEAD_DIM_A

    for p, (win, dil) in enumerate(PATTERNS):
        blocks_per_residue = seq // (dil * BLK_A)

        def block(i, carry, p=p, dil=dil, blocks_per_residue=blocks_per_residue):
            r = i // blocks_per_residue
            n = i % blocks_per_residue
            q_start = r + n * (BLK_A * dil)
            k_start = PAD_A + q_start - BLK_A * dil
            if dil == 1:
                q_rows = pl.ds(pl.multiple_of(q_start, BLK_A), BLK_A)
                k_rows = pl.ds(pl.multiple_of(k_start, BLK_A), 2 * BLK_A)
            else:
                q_rows = pl.ds(q_start, BLK_A, stride=dil)
                k_rows = pl.ds(k_start, 2 * BLK_A, stride=dil)
            first = jnp.where(n == 0, 1, 0)
            q = qn_s[q_rows, :]
            k = kp_s[k_rows, :].astype(BF16)
            v = vp_s[k_rows, :].astype(BF16)
            rhs = jnp.concatenate([v, ones_s[...]], axis=1)
            q2 = jnp.concatenate([jnp.where(left_b, q, 0.0), jnp.where(left_b, 0.0, q)], axis=0)
            s2 = _bdot_nt(q2, k)
            es, ms = [], []
            for h in range(HPAIR):
                s = s2[h * BLK_A:(h + 1) * BLK_A, :] + bias_ref[h, p, first]
                mh = jnp.max(s, axis=-1, keepdims=True)
                es.append(jnp.exp2(s - mh).astype(BF16))
                ms.append(mh)
            res = jnp.dot(jnp.concatenate(es, axis=0), rhs, preferred_element_type=F32)
            o_s[p, q_rows, :] = jnp.where(left_b, res[0:BLK_A, 0:LANES], res[BLK_A:, 0:LANES])
            l_s[p, q_rows, :] = jnp.where(left_b, res[0:BLK_A, LANES:], res[BLK_A:, LANES:])
            m_s[p, q_rows, :] = jnp.where(left_b, ms[0], ms[1])
            return carry
        lax.fori_loop(0, seq // BLK_A, block, 0, unroll=unroll)

    def out_chunk(c, carry):
        rows = pl.ds(pl.multiple_of(c * norm_rows, norm_rows), norm_rows)
        m = [m_s[p, rows, :] for p in range(N_PAT)]
        top = functools.reduce(jnp.maximum, m)
        w = [jnp.exp2(x - top) for x in m]
        num = functools.reduce(lambda a, b: a + b, [w[p] * o_s[p, rows, :] for p in range(N_PAT)])
        den = functools.reduce(lambda a, b: a + b, [w[p] * l_s[p, rows, :] for p in range(N_PAT)])
        y_ref[rows, :] = (num / den * _silu(z_ref[rows, :])).astype(BF16)
        return carry
    lax.fori_loop(0, seq // norm_rows, out_chunk, 0)


def _attn_prompt(proj, gq2, gk2, bias_p):
    b, seq, _ = proj.shape
    keep = min(WIN_MAX, seq)
    assert seq % (BLK_A * PATTERNS[-1][1]) == 0
    col = lambda base: pl.BlockSpec((None, seq, LANES), lambda bi, hp: (bi, 0, base + hp))
    vec = pl.BlockSpec((1, LANES), lambda bi, hp: (0, 0))
    return pl.pallas_call(
        functools.partial(_attn_prompt_kernel, seq=seq, keep=keep, unroll=8),
        out_shape=(jax.ShapeDtypeStruct((b, seq, D_A), BF16),
                   jax.ShapeDtypeStruct((b, D_A, keep), F32),
                   jax.ShapeDtypeStruct((b, D_A, keep), F32)),
        grid=(b, H_A // HPAIR),
        in_specs=[col(COL_Q), col(COL_K), col(COL_V), col(COL_Z), vec, vec,
                  pl.BlockSpec((HPAIR, N_PAT, 2, BLK_A, 2 * BLK_A), lambda bi, hp: (hp, 0, 0, 0, 0))],
        out_specs=(pl.BlockSpec((None, seq, LANES), lambda bi, hp: (bi, 0, hp)),
                   pl.BlockSpec((None, LANES, keep), lambda bi, hp: (bi, hp, 0)),
                   pl.BlockSpec((None, LANES, keep), lambda bi, hp: (bi, hp, 0))),
        scratch_shapes=[pltpu.VMEM((seq, LANES), F32),
                        pltpu.VMEM((PAD_A + seq, LANES), F32),
                        pltpu.VMEM((PAD_A + seq, LANES), F32),
                        pltpu.VMEM((2 * BLK_A, LANES), BF16),
                        pltpu.VMEM((N_PAT, seq, LANES), F32),
                        pltpu.VMEM((N_PAT, seq, LANES), F32),
                        pltpu.VMEM((N_PAT, seq, LANES), F32)],
        compiler_params=_cparams(("arbitrary", "arbitrary")),
        name="attn_prompt",
    )(proj, proj, proj, proj, gq2, gk2, bias_p)


def _attn_sample_kernel(q_ref, k_ref, v_ref, z_ref, ck_ref, cv_ref, gq_ref, gk_ref, bias_ref,
                        y_ref, nk_ref, nv_ref, *, n_past, t_new):
    q_all = q_ref[...]
    k_all = k_ref[...]
    v_all = v_ref[...]
    z_all = z_ref[...]
    nv_ref[...] = v_all
    pad_rows = jnp.zeros((LANES - t_new, HEAD_DIM_A), F32)

    def norm(x, g):
        ms = jnp.sum(x * x, axis=-1, keepdims=True) * (1.0 / HEAD_DIM_A)
        return x * lax.rsqrt(ms + EPS) * g

    for h in range(H_A):
        cols = slice(h * HEAD_DIM_A, (h + 1) * HEAD_DIM_A)
        qn = (norm(q_all[:, cols], gq_ref[:, cols]) * (HEAD_DIM_A ** -0.5)).astype(BF16)
        kn = norm(k_all[:, cols], gk_ref[:, cols])
        nk_ref[:, cols] = kn
        k_new = jnp.concatenate([kn, pad_rows], axis=0).astype(BF16)
        v_new = jnp.concatenate([v_all[:, cols], pad_rows], axis=0).astype(BF16)
        k_t = ck_ref[h].astype(BF16)
        v_t = cv_ref[h].astype(BF16)
        s_c = jnp.dot(qn, k_t, preferred_element_type=F32)
        s_n = _bdot_nt(qn, k_new)
        sc = [s_c + bias_ref[h, p, :, 0:n_past] for p in range(N_PAT)]
        sn = [s_n + bias_ref[h, p, :, n_past:] for p in range(N_PAT)]
        m = functools.reduce(jnp.maximum, [jnp.max(x, axis=-1, keepdims=True) for x in sc + sn])
        e_c = functools.reduce(lambda a, b: a + b, [jnp.exp(x - m) for x in sc])
        e_n = functools.reduce(lambda a, b: a + b, [jnp.exp(x - m) for x in sn])
        den = jnp.sum(e_c, axis=-1, keepdims=True) + jnp.sum(e_n, axis=-1, keepdims=True)
        o = _bdot_nt(e_c, v_t) + jnp.dot(e_n.astype(BF16), v_new, preferred_element_type=F32)
        y_ref[:, cols] = (o / den * _silu(z_all[:, cols])).astype(BF16)


def _attn_sample(proj_s, cache_kt, cache_vt, gq, gk, bias_s, layer):
    bd, t_new, _ = proj_s.shape
    n_past = cache_kt.shape[-1]
    col = lambda base: pl.BlockSpec((None, t_new, D_A), lambda bi: (bi, 0, base))
    vec = pl.BlockSpec((1, D_A), lambda bi: (0, 0))
    cache = pl.BlockSpec((None, None, H_A, HEAD_DIM_A, n_past), lambda bi: (layer, bi, 0, 0, 0))
    out = pl.BlockSpec((None, t_new, D_A), lambda bi: (bi, 0, 0))
    return pl.pallas_call(
        functools.partial(_attn_sample_kernel, n_past=n_past, t_new=t_new),
        out_shape=(jax.ShapeDtypeStruct((bd, t_new, D_A), BF16),
                   jax.ShapeDtypeStruct((bd, t_new, D_A), F32),
                   jax.ShapeDtypeStruct((bd, t_new, D_A), F32)),
        grid=(bd,),
        in_specs=[col(0), col(1), col(2), col(3), cache, cache, vec, vec,
                  pl.BlockSpec((H_A, N_PAT, t_new, n_past + LANES), lambda bi: (0, 0, 0, 0))],
        out_specs=(out, out, out),
        compiler_params=_cparams(("arbitrary",)),
        name="attn_sample",
    )(proj_s, proj_s, proj_s, proj_s, cache_kt, cache_vt, gq, gk, bias_s)


def _sgu_prompt_kernel(u_ref, v_ref, z_ref, g_ref, w_ref, bt_ref, y_ref, *, chunks):
    row = lax.broadcasted_iota(jnp.int32, (CHUNK_B, CHUNK_B), 0)
    colm = lax.broadcasted_iota(jnp.int32, (CHUNK_B, CHUNK_B), 1)
    tril = row >= colm
    ws = [jnp.where(tril, w_ref[g], 0.0).astype(BF16) for g in range(G_B)]
    for c in range(chunks):
        rows = slice(c * CHUNK_B, (c + 1) * CHUNK_B)
        v = v_ref[rows, :]
        ms = jnp.sum(v * v, axis=-1, keepdims=True) * (1.0 / D_B)
        vn = (v * lax.rsqrt(ms + EPS) * g_ref[...]).astype(BF16)
        for g in range(G_B):
            cols = slice(g * C_B, (g + 1) * C_B)
            mix = jnp.dot(ws[g], vn[:, cols], preferred_element_type=F32) + bt_ref[:, g:g + 1]
            y_ref[rows, cols] = (u_ref[rows, cols] * mix * _silu(z_ref[rows, cols])).astype(BF16)


def _sgu_prompt(proj, sgu_g, sgu_w, sgu_bt):
    b, seq, _ = proj.shape
    ts = 512
    col = lambda base: pl.BlockSpec((None, ts, D_B), lambda bi, i: (bi, i, base))
    return pl.pallas_call(
        functools.partial(_sgu_prompt_kernel, chunks=ts // CHUNK_B),
        out_shape=jax.ShapeDtypeStruct((b, seq, D_B), BF16),
        grid=(b, seq // ts),
        in_specs=[col(COL_UB), col(COL_VB), col(COL_ZB),
                  pl.BlockSpec((1, D_B), lambda bi, i: (0, 0)),
                  pl.BlockSpec((G_B, CHUNK_B, CHUNK_B), lambda bi, i: (0, 0, 0)),
                  pl.BlockSpec((CHUNK_B, G_B), lambda bi, i: (0, 0))],
        out_specs=pl.BlockSpec((None, ts, D_B), lambda bi, i: (bi, i, 0)),
        compiler_params=_cparams(("arbitrary", "arbitrary")),
        name="sgu_prompt",
    )(proj, proj, proj, sgu_g, sgu_w, sgu_bt)


def _sgu_sample_kernel(u_ref, v_ref, z_ref, g_ref, wexp_ref, bexp_ref, y_ref, vn_ref, *, t_new):
    v = v_ref[...]
    ms = jnp.sum(v * v, axis=-1, keepdims=True) * (1.0 / D_B)
    vn = v * lax.rsqrt(ms + EPS) * g_ref[...]
    vn_ref[...] = vn
    mix = bexp_ref[...]
    for s in range(t_new):
        mix = mix + wexp_ref[s] * vn[s:s + 1, :]
    y_ref[...] = (u_ref[...] * mix * _silu(z_ref[...])).astype(BF16)


def _sgu_sample(proj_s, sgu_g, wexp, bexp):
    bd, t_new, _ = proj_s.shape
    col = lambda base: pl.BlockSpec((None, t_new, D_B), lambda bi: (bi, 0, base))
    return pl.pallas_call(
        functools.partial(_sgu_sample_kernel, t_new=t_new),
        out_shape=(jax.ShapeDtypeStruct((bd, t_new, D_B), BF16),
                   jax.ShapeDtypeStruct((bd, t_new, D_B), F32)),
        grid=(bd,),
        in_specs=[col(COL_UB), col(COL_VB), col(COL_ZB),
                  pl.BlockSpec((1, D_B), lambda bi: (0, 0)),
                  pl.BlockSpec((t_new, t_new, D_B), lambda bi: (0, 0, 0)),
                  pl.BlockSpec((t_new, D_B), lambda bi: (0, 0))],
        out_specs=(pl.BlockSpec((None, t_new, D_B), lambda bi: (bi, 0, 0)),
                   pl.BlockSpec((None, t_new, D_B), lambda bi: (bi, 0, 0))),
        compiler_params=_cparams(("arbitrary",)),
        name="sgu_sample",
    )(proj_s, proj_s, proj_s, sgu_g, wexp, bexp)


def _log_sigmoid(x):
    return jnp.minimum(x, 0.0) - jnp.log1p(jnp.exp(-jnp.abs(x)))


def _mlstm_kernel(qk_ref, v_ref, o_ref, z_ref, gt_ref, cbuf_ref, c0_ref, n0_ref, m0_ref,
                  cw_ref, cb_ref, gb_ref, hn_ref,
                  y_ref, conv_out_ref, c_out_ref, n_out_ref, m_out_ref,
                  xp_s, vp_s, gp_s, caug_s, m_s, *, t_rows):
    L = CHUNK_C
    ci = pl.program_id(1)
    last = pl.num_programs(1) - 1
    base = SUBLANES
    carry_rows = CONV_W - 1
    row_l = lax.broadcasted_iota(jnp.int32, (L, LANES), 0)
    lane_l = lax.broadcasted_iota(jnp.int32, (L, LANES), 1)
    zero_rows = jnp.zeros((L - SUBLANES, LANES), F32)

    def to_cols(x8):
        return jnp.concatenate([x8, zero_rows], axis=0).T

    @pl.when(ci == 0)
    def _():
        xp_s[...] = jnp.zeros(xp_s.shape, F32)
        vp_s[...] = jnp.zeros(vp_s.shape, F32)
        gp_s[...] = jnp.zeros(gp_s.shape, F32)
        xp_s[base - carry_rows:base, :] = cbuf_ref[...]
        n0_cols = to_cols(n0_ref[...])
        for h in range(H_C):
            caug_s[h, :, 0:DH_C] = c0_ref[h]
            caug_s[h, :, DH_C:] = jnp.broadcast_to(n0_cols[:, h:h + 1], (DH_C, DH_C))
        m_s[...] = m0_ref[...]

    xp_s[base:base + t_rows, :] = qk_ref[...]
    vp_s[0:t_rows, :] = v_ref[...]
    gp_s[0:t_rows, :] = gt_ref[...]

    acc = cb_ref[...] + cw_ref[CONV_W - 1:CONV_W, :] * xp_s[base:base + L, :]
    for j in range(CONV_W - 1):
        off = base - carry_rows + j
        acc = acc + cw_ref[j:j + 1, :] * xp_s[off:off + L, :]
    qk = _silu(acc)
    new_carry = xp_s[base + t_rows - carry_rows:base + t_rows, :]
    xp_s[base - carry_rows:base, :] = new_carry

    gates_t = (gp_s[...] + gb_ref[...]).T
    lane_t = lax.broadcasted_iota(jnp.int32, (SUBLANES, L), 1)
    real_t = lane_t < t_rows
    i_t = jnp.where(real_t, gates_t[0:SUBLANES, :], NEG)
    logf_t = jnp.where(real_t, _log_sigmoid(gates_t[SUBLANES:2 * SUBLANES, :]), 0.0)
    upper = (row_l <= lane_l).astype(BF16)
    f1 = logf_t.astype(BF16)
    r1 = logf_t - f1.astype(F32)
    f2 = r1.astype(BF16)
    f3 = (r1 - f2.astype(F32)).astype(BF16)
    b_t = (jnp.dot(f1, upper, preferred_element_type=F32) + jnp.dot(f2, upper, preferred_element_type=F32)
           + jnp.dot(f3, upper, preferred_element_type=F32))
    m_prev = m_s[...]
    c_t = i_t - b_t
    pm = c_t
    shift = 1
    while shift < L:
        pm = jnp.maximum(pm, jnp.where(lane_t >= shift, pltpu.roll(pm, shift, axis=1), NEG))
        shift *= 2
    d_t = -jnp.maximum(m_prev, pm)
    wi_t = jnp.exp(m_prev + d_t)
    em_t = jnp.exp(-(b_t - d_t))
    bl = jnp.broadcast_to(b_t[:, L - 1:L], (SUBLANES, L))
    g_t = bl - b_t + i_t
    m_new = jnp.maximum(bl + m_prev, jnp.broadcast_to(jnp.max(g_t, axis=1, keepdims=True), (SUBLANES, L)))
    ws_t = jnp.exp(g_t - m_new)
    wc = jnp.exp(bl + m_prev - m_new)
    m_s[...] = m_new
    stats = jnp.concatenate([d_t, wi_t, em_t, ws_t, zero_rows[0:L - 4 * SUBLANES]], axis=0).T

    causal = row_l >= lane_l
    ones_blk = jnp.ones((L, DH_C), BF16)
    v_all = vp_s[...]
    o_all = o_ref[...]
    z_all = z_ref[...]

    for h in range(H_C):
        cols = slice(h * DH_C, (h + 1) * DH_C)
        q = qk[:, cols].astype(BF16)
        k = qk[:, D_C + h * DH_C:D_C + (h + 1) * DH_C] * (DH_C ** -0.5)
        v_aug = jnp.concatenate([v_all[:, cols].astype(BF16), ones_blk], axis=1)
        d_col = stats[:, h:h + 1]
        wi_col = stats[:, SUBLANES + h:SUBLANES + h + 1]
        em_col = stats[:, 2 * SUBLANES + h:2 * SUBLANES + h + 1]
        ws_col = stats[:, 3 * SUBLANES + h:3 * SUBLANES + h + 1]
        c_prev = caug_s[h]

        a = jnp.exp(jnp.where(causal, d_col + c_t[h:h + 1, :], NEG)) * _bdot_nt(q, k)
        r1_ = jnp.dot(a.astype(BF16), v_aug, preferred_element_type=F32)
        r2_ = jnp.dot(q, c_prev.astype(BF16), preferred_element_type=F32)
        num = r1_[:, 0:DH_C] + wi_col * r2_[:, 0:DH_C]
        den = r1_[:, DH_C:] + wi_col * r2_[:, DH_C:]
        hh = num / jnp.maximum(jnp.abs(den), em_col)

        wc_row = jnp.concatenate([wc[h:h + 1, :], wc[h:h + 1, :]], axis=1)
        caug_s[h] = wc_row * c_prev + _bdot_tn(k * ws_col, v_aug)

        ms = jnp.sum(hh * hh, axis=-1, keepdims=True) * (1.0 / DH_C)
        hn = hh * lax.rsqrt(ms + EPS) * hn_ref[:, cols]
        yv = hn[0:t_rows, :] * jax.nn.sigmoid(o_all[:, cols]) * _silu(z_all[:, cols])
        y_ref[:, cols] = yv.astype(BF16)

    @pl.when(ci == last)
    def _():
        conv_out_ref[...] = new_carry
        n_cols = jnp.zeros((DH_C, LANES), F32)
        for h in range(H_C):
            c_out_ref[h] = caug_s[h, :, 0:DH_C]
            n_cols = jnp.where(lane_l == h, caug_s[h, :, DH_C:], n_cols)
        n_out_ref[...] = n_cols.T[0:SUBLANES, :]
        m_out_ref[...] = m_s[...]


def _mlstm(proj, gates, conv_buf, c0, n0, m0, conv_w, conv_b, gate_bias, hn_g, t_rows):
    b, seq, _ = proj.shape
    nchunks = seq // t_rows
    hp = SUBLANES
    col = lambda base, width: pl.BlockSpec((None, t_rows, width), lambda bi, ci: (bi, ci, base))
    const2 = lambda shape: pl.BlockSpec(shape, lambda bi, ci: (0, 0))
    per_b3 = lambda shape: pl.BlockSpec((None,) + shape, lambda bi, ci: (bi, 0, 0))
    per_b4 = lambda shape: pl.BlockSpec((None,) + shape, lambda bi, ci: (bi, 0, 0, 0))
    return pl.pallas_call(
        functools.partial(_mlstm_kernel, t_rows=t_rows),
        out_shape=(jax.ShapeDtypeStruct((b, seq, D_C), BF16),
                   jax.ShapeDtypeStruct((b, CONV_W - 1, 2 * D_C), F32),
                   jax.ShapeDtypeStruct((b, H_C, DH_C, DH_C), F32),
                   jax.ShapeDtypeStruct((b, hp, DH_C), F32),
                   jax.ShapeDtypeStruct((b, hp, LANES), F32)),
        grid=(b, nchunks),
        in_specs=[col(COL_QK, 2 * D_C), col(COL_VC, D_C), col(COL_OC, D_C), col(COL_ZC, D_C),
                  pl.BlockSpec((None, t_rows, LANES), lambda bi, ci: (bi, ci, 0)),
                  per_b3((CONV_W - 1, 2 * D_C)), per_b4((H_C, DH_C, DH_C)), per_b3((hp, DH_C)),
                  per_b3((hp, LANES)),
                  const2((CONV_W, 2 * D_C)), const2((1, 2 * D_C)), const2((1, LANES)), const2((1, D_C))],
        out_specs=(pl.BlockSpec((None, t_rows, D_C), lambda bi, ci: (bi, ci, 0)),
                   per_b3((CONV_W - 1, 2 * D_C)), per_b4((H_C, DH_C, DH_C)), per_b3((hp, DH_C)),
                   per_b3((hp, LANES))),
        scratch_shapes=[pltpu.VMEM((SUBLANES + CHUNK_C, 2 * D_C), F32),
                        pltpu.VMEM((CHUNK_C, D_C), F32),
                        pltpu.VMEM((CHUNK_C, LANES), F32),
                        pltpu.VMEM((H_C, DH_C, 2 * DH_C), F32),
                        pltpu.VMEM((hp, LANES), F32)],
        compiler_params=_cparams(("arbitrary", "arbitrary")),
        name="mlstm",
    )(proj, proj, proj, proj, gates, conv_buf, c0, n0, m0, conv_w, conv_b, gate_bias, hn_g)


def _gate_lanes(i_part, f_part):
    lead = i_part.shape[:-1]
    gap = jnp.zeros(lead + (SUBLANES - H_C,), i_part.dtype)
    tail = jnp.zeros(lead + (LANES - 2 * SUBLANES,), i_part.dtype)
    return jnp.concatenate([i_part, gap, f_part, gap, tail], axis=-1)


def _pad_heads(a):
    pad = [(0, 0)] * a.ndim
    pad[1] = (0, SUBLANES - a.shape[1])
    return jnp.pad(a, pad)


def _mixers(proj, gates, lw, attn_fn, sgu_fn, conv_buf, c0, n0, m0, t_rows):
    ya, nk, nv = attn_fn(proj)
    sgu_out = sgu_fn(proj)
    m0b = jnp.broadcast_to(_pad_heads(m0)[:, :, None], (m0.shape[0], SUBLANES, LANES))
    yc, nconv, c_new, n_new, m_new = _mlstm(proj, gates, conv_buf, c0, _pad_heads(n0), m0b,
                                            lw["conv_w"], lw["conv_b"], lw["gate_bias"], lw["hn_g"], t_rows)
    return ya, sgu_out, yc, nk, nv, nconv, c_new, n_new[:, :H_C], m_new[:, :H_C, 0]


def kernel(x_prompt, x_sample, c_prompt, c_sample, cache_k_win, cache_v_win, state_conv, state_C, state_n, state_m, rel_bias, norm_g, ada_w, ada_b, w_in, qn_g, kn_g, sgu_g, sgu_w, sgu_b, conv_w, conv_b, f_bias, i_bias, hn_g, w_out):
    depth = w_in.shape[0]
    bp, seq, _ = x_prompt.shape
    bd, t_new, _ = x_sample.shape
    n_past = cache_k_win.shape[2]
    assert t_new == SUBLANES and n_past % LANES == 0 and seq % 1024 == 0

    w_in_t = _unpack_w_in(w_in)
    w_gate = _gate_lanes(w_in[:, :, D_MAIN:D_MAIN + H_C], w_in[:, :, D_MAIN + H_C:])
    w_out_bf = w_out.astype(BF16)

    rel_t = rel_bias.T
    bias_p = _expand_bias(rel_t, jnp.asarray(_prompt_bucket_table()), 64, LOG2E)
    bias_p = bias_p.reshape(H_A, N_PAT, 2, BLK_A, 2 * BLK_A)
    bias_s = _expand_bias(rel_t, jnp.asarray(_sample_bucket_table(n_past, t_new)), SUBLANES, 1.0)
    bias_s = bias_s.reshape(H_A, N_PAT, t_new, n_past + LANES)

    n_c = bp + bd
    c_all = jnp.pad(jnp.concatenate([c_prompt, c_sample], axis=0), ((0, 2 * SUBLANES - n_c), (0, 0)))
    mod = _ada_mod(c_all, ada_w, ada_b)

    cache_kt = jnp.transpose(cache_k_win, (0, 1, 3, 4, 2))
    cache_vt = jnp.transpose(cache_v_win, (0, 1, 3, 4, 2))
    gq_all = [jnp.tile(qn_g[l], H_A)[None] for l in range(depth)]
    gk_all = [jnp.tile(kn_g[l], H_A)[None] for l in range(depth)]
    zeros_conv = jnp.zeros((bp, CONV_W - 1, 2 * D_C), F32)
    zeros_c = jnp.zeros((bp, H_C, DH_C, DH_C), F32)
    zeros_n = jnp.zeros((bp, H_C, DH_C), F32)
    zeros_m = jnp.zeros((bp, H_C), F32)
    tril8 = jnp.tril(jnp.ones((t_new, t_new), F32))

    xp = x_prompt.reshape(bp * seq, D_MODEL)
    xs = x_sample.reshape(bd * t_new, D_MODEL)
    outs = [[] for _ in range(13)]
    p_mods, s_mods = [], []
    for l in range(depth):
        shift, scale, gate = (mod[l, :, i * D_MODEL:(i + 1) * D_MODEL] for i in range(3))
        p_mods.append([a[:bp, None, :] for a in (scale, shift, gate)])
        s_mods.append([jnp.repeat(a[bp:n_c], t_new, axis=0)[None] for a in (scale, shift, gate)])
    hp = _norm(xp, p_mods[0][0], p_mods[0][1], norm_g[0][None], 512)
    hs = _norm(xs, s_mods[0][0], s_mods[0][1], norm_g[0][None], bd * t_new)
    for l in range(depth):
        p_mod, s_mod = p_mods[l], s_mods[l]
        last = l == depth - 1
        p_next = None if last else (p_mods[l + 1][0], p_mods[l + 1][1], norm_g[l + 1][None])
        s_next = None if last else (s_mods[l + 1][0], s_mods[l + 1][1], norm_g[l + 1][None])
        lw = {
            "conv_w": conv_w[l], "conv_b": conv_b[l][None],
            "gate_bias": _gate_lanes(i_bias[l], f_bias[l])[None],
            "hn_g": hn_g[l][None],
        }
        gq2 = jnp.tile(qn_g[l], HPAIR)[None]
        gk2 = jnp.tile(kn_g[l], HPAIR)[None]
        sg = sgu_g[l][None]

        proj, gates = _inproj(hp, w_in_t, w_gate, l, 2048)
        proj = proj.reshape(bp, seq, D_MAIN)
        gates = gates.reshape(bp, seq, LANES)
        ya, yb, yc, nk, nv, ncv, nc_, nn_, nm = _mixers(
            proj, gates, lw,
            lambda pr: _attn_prompt(pr, gq2, gk2, bias_p),
            lambda pr: _sgu_prompt(pr, sg, sgu_w[l], sgu_b[l].T),
            zeros_conv, zeros_c, zeros_n, zeros_m, CHUNK_C)
        xp, hp = _outproj(ya.reshape(bp * seq, D_A), yb.reshape(bp * seq, D_B), yc.reshape(bp * seq, D_C),
                          xp, p_mod[2], w_out_bf, l, 512, p_next)
        keep = nk.shape[2]
        for i, a in enumerate((nk.reshape(bp, H_A, HEAD_DIM_A, keep), nv.reshape(bp, H_A, HEAD_DIM_A, keep),
                               ncv, nc_, nn_, nm)):
            outs[i].append(a)

        proj_s, gates_s = _inproj(hs, w_in_t, w_gate, l, bd * t_new)
        proj_s = proj_s.reshape(bd, t_new, D_MAIN)
        gates_s = gates_s.reshape(bd, t_new, LANES)
        w8 = sgu_w[l][:, :t_new, :t_new] * tril8
        wexp = jnp.repeat(jnp.transpose(w8, (2, 1, 0)), C_B, axis=2)
        bexp = jnp.repeat(sgu_b[l][:, :t_new].T, C_B, axis=1)
        ya, sgu_out, yc, nk, nv, ncv, nc_, nn_, nm = _mixers(
            proj_s, gates_s, lw,
            lambda pr: _attn_sample(pr, cache_kt, cache_vt, gq_all[l], gk_all[l], bias_s, l),
            lambda pr: _sgu_sample(pr, sg, wexp, bexp),
            state_conv[l], state_C[l], state_n[l], state_m[l], t_new)
        yb, vn = sgu_out
        xs, hs = _outproj(ya.reshape(bd * t_new, D_A), yb.reshape(bd * t_new, D_B), yc.reshape(bd * t_new, D_C),
                          xs, s_mod[2], w_out_bf, l, bd * t_new, s_next)
        for i, a in enumerate((nk.reshape(bd, t_new, H_A, HEAD_DIM_A), nv.reshape(bd, t_new, H_A, HEAD_DIM_A),
                               vn, ncv, nc_, nn_, nm)):
            outs[6 + i].append(a)

    stacked = [jnp.stack(o) for o in outs]
    for i in (0, 1):
        stacked[i] = jnp.transpose(stacked[i], (0, 1, 4, 2, 3))
    return (xp.reshape(bp, seq, D_MODEL), xs.reshape(bd, t_new, D_MODEL), *stacked)
```

```python
import functools
import math

import numpy as np
import jax
import jax.numpy as jnp
from jax import lax
from jax.experimental import pallas as pl
from jax.experimental.pallas import tpu as pltpu

F32 = jnp.float32
BF16 = jnp.bfloat16

D_MODEL = 2048
HEAD_DIM_A = 64
D_A = 768
H_A = 12
D_B = 512
G_B = 4
C_B = 128
CHUNK_B = 128
D_C = 768
DH_C = 128
H_C = 6
CHUNK_C = 128
CONV_W = 4
PATTERNS = ((128, 1), (512, 4), (2048, 16))
N_PAT = len(PATTERNS)
WIN_MAX = 2048
BLK_A = 128
N_BUCKETS = 32
MAX_DIST = 2048
EPS = 1e-6
D_MAIN = 4 * D_A + 3 * D_B + 2 * D_C + 3 * D_C
D_IN = D_MAIN + 2 * H_C

LANES = 128
SUBLANES = 8
VMEM_LIMIT = 56 * 1024 * 1024

NEG = -1e30
LOG2E = 1.4426950408889634
PAD_A = BLK_A * PATTERNS[-1][1]
HPAIR = LANES // HEAD_DIM_A

COL_Q, COL_K, COL_V, COL_Z = 0, D_A // LANES, 2 * D_A // LANES, 3 * D_A // LANES
COL_UB, COL_VB, COL_ZB = 4 * D_A // D_B, 4 * D_A // D_B + 1, 4 * D_A // D_B + 2
COL_QK = (4 * D_A + 3 * D_B) // (2 * D_C)
COL_VC = (4 * D_A + 3 * D_B + 2 * D_C) // D_C
COL_OC, COL_ZC = COL_VC + 1, COL_VC + 2


def _cparams(sem):
    return pltpu.CompilerParams(dimension_semantics=sem, vmem_limit_bytes=VMEM_LIMIT)


def _silu(x):
    return x * jax.nn.sigmoid(x)


def _bdot(a, b):
    return jnp.dot(a.astype(BF16), b.astype(BF16), preferred_element_type=F32)


def _bdot_nt(a, b):
    return lax.dot_general(a.astype(BF16), b.astype(BF16), (((1,), (1,)), ((), ())),
                           preferred_element_type=F32)


def _bdot_tn(a, b):
    return lax.dot_general(a.astype(BF16), b.astype(BF16), (((0,), (0,)), ((), ())),
                           preferred_element_type=F32)


def _bucket_np(dist):
    max_exact = N_BUCKETS // 2
    df = np.maximum(dist, 1).astype(np.float32)
    large = max_exact + (np.log(df / np.float32(max_exact)) / np.float32(math.log(MAX_DIST / max_exact))
                         * np.float32(N_BUCKETS - max_exact)).astype(np.int32)
    return np.where(dist < max_exact, dist, np.minimum(large, N_BUCKETS - 1)).astype(np.int32)


def _prompt_bucket_table():
    qi = np.arange(BLK_A)[:, None]
    ki = np.arange(2 * BLK_A)[None, :]
    j = qi + BLK_A - ki
    out = []
    for win, dil in PATTERNS:
        n_back = win // dil
        band = (j >= 0) & (j <= n_back)
        b = _bucket_np(np.clip(j, 0, n_back) * dil)
        out.append(np.where(band, b, -1))
        out.append(np.where(band & (ki >= BLK_A), b, -1))
    return np.stack(out).reshape(N_PAT * 2 * BLK_A, 2 * BLK_A).astype(np.int32)


def _sample_bucket_table(n_past, t_new):
    c = np.arange(n_past + LANES)[None, :]
    t = np.arange(t_new)[:, None]
    delta = n_past + t - c
    out = []
    for win, dil in PATTERNS:
        valid = (c < n_past + t_new) & (delta >= 0) & (delta % dil == 0) & (delta // dil <= win // dil)
        out.append(np.where(valid, _bucket_np(np.maximum(delta, 0)), -1))
    return np.stack(out).reshape(N_PAT * t_new, n_past + LANES).astype(np.int32)


def _bias_kernel(rb_ref, idx_ref, out_ref, *, scale, row_chunk):
    h = pl.program_id(0)

    def chunk(c, carry):
        rows = pl.ds(pl.multiple_of(c * row_chunk, row_chunk), row_chunk)
        idx = idx_ref[rows, :]
        out = jnp.full(idx.shape, NEG, F32)
        for b in range(N_BUCKETS):
            out = jnp.where(idx == b, rb_ref[h, b] * scale, out)
        out_ref[rows, :] = out
        return carry
    lax.fori_loop(0, idx_ref.shape[0] // row_chunk, chunk, 0)


def _expand_bias(rel_bias_t, idx, row_chunk, scale):
    rows, cols = idx.shape
    return pl.pallas_call(
        functools.partial(_bias_kernel, scale=scale, row_chunk=row_chunk),
        out_shape=jax.ShapeDtypeStruct((H_A, rows, cols), F32),
        grid=(H_A,),
        in_specs=[pl.BlockSpec(memory_space=pltpu.SMEM),
                  pl.BlockSpec((rows, cols), lambda h: (0, 0))],
        out_specs=pl.BlockSpec((None, rows, cols), lambda h: (h, 0, 0)),
        compiler_params=_cparams(("arbitrary",)),
        name="bias_expand",
    )(rel_bias_t, idx)


def _ada_kernel(c_ref, w_ref, b_ref, o_ref):
    c = c_ref[...]
    a = _silu(c)
    w = w_ref[...]
    a_hi = a.astype(BF16)
    a_lo = (a - a_hi.astype(F32)).astype(BF16)
    w_hi = w.astype(BF16)
    w_lo = (w - w_hi.astype(F32)).astype(BF16)
    acc = jnp.dot(a_hi, w_hi, preferred_element_type=F32)
    acc += jnp.dot(a_hi, w_lo, preferred_element_type=F32)
    acc += jnp.dot(a_lo, w_hi, preferred_element_type=F32)
    o_ref[...] = acc + b_ref[...]


def _ada_mod(c_all, ada_w, ada_b):
    depth = ada_w.shape[0]
    rows = c_all.shape[0]
    tn = 768
    n = 3 * D_MODEL
    return pl.pallas_call(
        _ada_kernel,
        out_shape=jax.ShapeDtypeStruct((depth, rows, n), F32),
        grid=(depth, n // tn),
        in_specs=[pl.BlockSpec((rows, D_MODEL), lambda l, j: (0, 0)),
                  pl.BlockSpec((None, D_MODEL, tn), lambda l, j: (l, 0, j)),
                  pl.BlockSpec((None, 1, tn), lambda l, j: (l, 0, j))],
        out_specs=pl.BlockSpec((None, rows, tn), lambda l, j: (l, 0, j)),
        compiler_params=_cparams(("arbitrary", "arbitrary")),
        name="ada_mod",
    )(c_all, ada_w, ada_b.reshape(depth, 1, n))


def _norm_mod(x, g, sc, sh):
    ms = jnp.sum(x * x, axis=-1, keepdims=True) * (1.0 / D_MODEL)
    return ((x * lax.rsqrt(ms + EPS) * g) * (1.0 + sc) + sh).astype(BF16)


def _norm_kernel(x_ref, sc_ref, sh_ref, g_ref, h_ref):
    h_ref[...] = _norm_mod(x_ref[...], g_ref[...], sc_ref[...], sh_ref[...])


def _norm(x2d, scale, shift, norm_g, tm):
    m = x2d.shape[0]
    tiles_per_group = m // tm // scale.shape[0]
    mod_spec = pl.BlockSpec((None, scale.shape[1], D_MODEL), lambda i: (i // tiles_per_group, 0, 0))
    return pl.pallas_call(
        _norm_kernel,
        out_shape=jax.ShapeDtypeStruct((m, D_MODEL), BF16),
        grid=(m // tm,),
        in_specs=[pl.BlockSpec((tm, D_MODEL), lambda i: (i, 0)), mod_spec, mod_spec,
                  pl.BlockSpec((1, D_MODEL), lambda i: (0, 0))],
        out_specs=pl.BlockSpec((tm, D_MODEL), lambda i: (i, 0)),
        compiler_params=_cparams(("arbitrary",)),
        name="norm",
    )(x2d, scale, shift, norm_g)


def _inproj_kernel(h_ref, w_ref, wg_ref, proj_ref, gates_ref):
    @pl.when(pl.program_id(1) == 0)
    def _():
        gates_ref[...] = jnp.dot(h_ref[...], wg_ref[...].astype(BF16), preferred_element_type=F32)

    proj_ref[...] = lax.dot_general(h_ref[...], w_ref[...], (((1,), (1,)), ((), ())),
                                    preferred_element_type=F32)


UNPACK_PITCH = 72


def _unpack_kernel(x_hbm, o_ref, buf_a, buf_b, sem, *, depth, k_tiles, tn):
    i = pl.program_id(0)
    rows_per_col = k_tiles * depth

    def copy(tile, n, buf, slot):
        return pltpu.make_async_copy(x_hbm.at[tile * tn + n], buf.at[pl.ds(n * UNPACK_PITCH, rows_per_col)],
                                     sem.at[slot])

    def start_all(tile, buf, slot):
        def body(n, carry):
            copy(tile, n, buf, slot).start()
            return carry
        lax.fori_loop(0, tn, body, 0, unroll=8)

    def wait_all(tile, buf, slot):
        def body(n, carry):
            copy(tile, n, buf, slot).wait()
            return carry
        lax.fori_loop(0, tn, body, 0, unroll=8)

    def convert(buf, half):
        for l in range(depth):
            for kt in range(k_tiles):
                rows = pl.ds(kt * depth + l, tn, stride=UNPACK_PITCH)
                o_ref[l, half * tn:(half + 1) * tn, kt * LANES:(kt + 1) * LANES] = buf[rows, :].astype(BF16)

    @pl.when(i == 0)
    def _():
        start_all(0, buf_a, 0)

    start_all(2 * i + 1, buf_b, 1)
    wait_all(2 * i, buf_a, 0)
    convert(buf_a, 0)

    @pl.when(i + 1 < pl.num_programs(0))
    def _():
        start_all(2 * i + 2, buf_a, 0)

    wait_all(2 * i + 1, buf_b, 1)
    convert(buf_b, 1)


def _unpack_w_in(w_in):
    depth, d, d_in = w_in.shape
    k_tiles = d // LANES
    tn = LANES
    assert D_MAIN % (2 * tn) == 0 and k_tiles * depth <= UNPACK_PITCH
    cols = w_in.reshape(depth, k_tiles, LANES, d_in).transpose(3, 1, 0, 2).reshape(d_in, k_tiles * depth, LANES)
    return pl.pallas_call(
        functools.partial(_unpack_kernel, depth=depth, k_tiles=k_tiles, tn=tn),
        out_shape=jax.ShapeDtypeStruct((depth, D_MAIN, d), BF16),
        grid=(D_MAIN // (2 * tn),),
        in_specs=[pl.BlockSpec(memory_space=pl.ANY)],
        out_specs=pl.BlockSpec((depth, 2 * tn, d), lambda i: (0, i, 0)),
        scratch_shapes=[pltpu.VMEM((tn * UNPACK_PITCH, LANES), F32),
                        pltpu.VMEM((tn * UNPACK_PITCH, LANES), F32),
                        pltpu.SemaphoreType.DMA((2,))],
        compiler_params=_cparams(("arbitrary",)),
        name="unpack_w_in",
    )(cols)


def _inproj(h2d, w_in_t, w_gate, layer, tm):
    m = h2d.shape[0]
    tn = 768
    return pl.pallas_call(
        _inproj_kernel,
        out_shape=(jax.ShapeDtypeStruct((m, D_MAIN), F32), jax.ShapeDtypeStruct((m, LANES), F32)),
        grid=(m // tm, D_MAIN // tn),
        in_specs=[pl.BlockSpec((tm, D_MODEL), lambda i, j: (i, 0)),
                  pl.BlockSpec((None, tn, D_MODEL), lambda i, j: (layer, j, 0)),
                  pl.BlockSpec((None, D_MODEL, LANES), lambda i, j: (layer, 0, 0))],
        out_specs=(pl.BlockSpec((tm, tn), lambda i, j: (i, j)),
                   pl.BlockSpec((tm, LANES), lambda i, j: (i, 0))),
        compiler_params=_cparams(("arbitrary", "arbitrary")),
        name="inproj",
    )(h2d, w_in_t, w_gate)


def _outproj_kernel(ya_ref, yb_ref, yc_ref, x_ref, gate_ref, w_ref, *rest, emit_next):
    y = jnp.dot(ya_ref[...], w_ref[0:D_A, :], preferred_element_type=F32)
    y += jnp.dot(yb_ref[...], w_ref[D_A:D_A + D_B, :], preferred_element_type=F32)
    y += jnp.dot(yc_ref[...], w_ref[D_A + D_B:, :], preferred_element_type=F32)
    x_new = x_ref[...] + gate_ref[...] * y
    if emit_next:
        sc_ref, sh_ref, g_ref, o_ref, h_ref = rest
        h_ref[...] = _norm_mod(x_new, g_ref[...], sc_ref[...], sh_ref[...])
    else:
        (o_ref,) = rest
    o_ref[...] = x_new


def _outproj(ya, yb, yc, x2d, gate, w_out_bf, layer, tm, next_norm=None):
    m = x2d.shape[0]
    tiles_per_group = m // tm // gate.shape[0]
    mod_spec = pl.BlockSpec((None, gate.shape[1], D_MODEL), lambda i: (i // tiles_per_group, 0, 0))
    row = lambda width: pl.BlockSpec((tm, width), lambda i: (i, 0))
    in_specs = [row(D_A), row(D_B), row(D_C), row(D_MODEL), mod_spec,
                pl.BlockSpec((None, D_MODEL, D_MODEL), lambda i: (layer, 0, 0))]
    args = [ya, yb, yc, x2d, gate, w_out_bf]
    out_shape = [jax.ShapeDtypeStruct((m, D_MODEL), F32)]
    out_specs = [row(D_MODEL)]
    if next_norm is not None:
        in_specs += [mod_spec, mod_spec, pl.BlockSpec((1, D_MODEL), lambda i: (0, 0))]
        args += list(next_norm)
        out_shape.append(jax.ShapeDtypeStruct((m, D_MODEL), BF16))
        out_specs.append(row(D_MODEL))
    out = pl.pallas_call(
        functools.partial(_outproj_kernel, emit_next=next_norm is not None),
        out_shape=tuple(out_shape),
        grid=(m // tm,),
        in_specs=in_specs,
        out_specs=tuple(out_specs),
        compiler_params=_cparams(("arbitrary",)),
        name="outproj",
    )(*args)
    return out if next_norm is not None else (out[0], None)


def _head_norm(x, g, left):
    x2 = x * x
    s_left = jnp.sum(jnp.where(left, x2, 0.0), axis=-1, keepdims=True)
    s_right = jnp.sum(jnp.where(left, 0.0, x2), axis=-1, keepdims=True)
    ms = jnp.where(left, s_left, s_right) * (1.0 / HEAD_DIM_A)
    return x * lax.rsqrt(ms + EPS) * g


def _attn_prompt_kernel(q_ref, k_ref, v_ref, z_ref, gq_ref, gk_ref, bias_ref,
                        y_ref, pk_ref, pv_ref,
                        qn_s, kp_s, vp_s, ones_s, o_s, l_s, m_s, *, seq, keep, unroll):
    norm_rows = 512
    lane_n = lax.broadcasted_iota(jnp.int32, (norm_rows, LANES), 1)
    left_n = lane_n < HEAD_DIM_A

    kp_s[0:PAD_A, :] = jnp.zeros((PAD_A, LANES), F32)
    vp_s[0:PAD_A, :] = jnp.zeros((PAD_A, LANES), F32)
    ones_s[...] = jnp.ones(ones_s.shape, BF16)

    def norm_chunk(c, carry):
        r0 = pl.multiple_of(c * norm_rows, norm_rows)
        rows = pl.ds(r0, norm_rows)
        qn_s[rows, :] = _head_norm(q_ref[rows, :], gq_ref[...], left_n) * (HEAD_DIM_A ** -0.5 * LOG2E)
        kp_s[pl.ds(PAD_A + r0, norm_rows), :] = _head_norm(k_ref[rows, :], gk_ref[...], left_n)
        vp_s[pl.ds(PAD_A + r0, norm_rows), :] = v_ref[rows, :]
        return carry
    lax.fori_loop(0, seq // norm_rows, norm_chunk, 0)

    for c in range(keep // LANES):
        src = slice(PAD_A + seq - keep + c * LANES, PAD_A + seq - keep + (c + 1) * LANES)
        pk_ref[:, c * LANES:(c + 1) * LANES] = kp_s[src, :].T
        pv_ref[:, c * LANES:(c + 1) * LANES] = vp_s[src, :].T

    left_b = lax.broadcasted_iota(jnp.int32, (BLK_A, LANES), 1) < HEAD_DIM_A
    left_k = lax.broadcasted_iota(jnp.int32, (2 * BLK_A, LANES), 1) < HEAD_DIM_A

    for p, (win, dil) in enumerate(PATTERNS):
        blocks_per_residue = seq // (dil * BLK_A)

        def block(i, carry, p=p, dil=dil, blocks_per_residue=blocks_per_residue):
            r = i // blocks_per_residue
            n = i % blocks_per_residue
            q_start = r + n * (BLK_A * dil)
            k_start = PAD_A + q_start - BLK_A * dil
            if dil == 1:
                q_rows = pl.ds(pl.multiple_of(q_start, BLK_A), BLK_A)
                k_rows = pl.ds(pl.multiple_of(k_start, BLK_A), 2 * BLK_A)
            else:
                q_rows = pl.ds(q_start, BLK_A, stride=dil)
                k_rows = pl.ds(k_start, 2 * BLK_A, stride=dil)
            first = jnp.where(n == 0, 1, 0)
            q = qn_s[q_rows, :]
            k = kp_s[k_rows, :].astype(BF16)
            v = vp_s[k_rows, :].astype(BF16)
            rhs = jnp.concatenate([v, ones_s[...]], axis=1)
            q2 = jnp.concatenate([jnp.where(left_b, q, 0.0), jnp.where(left_b, 0.0, q)], axis=0)
            s2 = _bdot_nt(q2, k)
            es, ms = [], []
            for h in range(HPAIR):
                s = s2[h * BLK_A:(h + 1) * BLK_A, :] + bias_ref[h, p, first]
                mh = jnp.max(s, axis=-1, keepdims=True)
                es.append(jnp.exp2(s - mh).astype(BF16))
                ms.append(mh)
            res = jnp.dot(jnp.concatenate(es, axis=0), rhs, preferred_element_type=F32)
            o_s[p, q_rows, :] = jnp.where(left_b, res[0:BLK_A, 0:LANES], res[BLK_A:, 0:LANES])
            l_s[p, q_rows, :] = jnp.where(left_b, res[0:BLK_A, LANES:], res[BLK_A:, LANES:])
            m_s[p, q_rows, :] = jnp.where(left_b, ms[0], ms[1])
            return carry
        lax.fori_loop(0, seq // BLK_A, block, 0, unroll=unroll)

    def out_chunk(c, carry):
        rows = pl.ds(pl.multiple_of(c * norm_rows, norm_rows), norm_rows)
        m = [m_s[p, rows, :] for p in range(N_PAT)]
        top = functools.reduce(jnp.maximum, m)
        w = [jnp.exp2(x - top) for x in m]
        num = functools.reduce(lambda a, b: a + b, [w[p] * o_s[p, rows, :] for p in range(N_PAT)])
        den = functools.reduce(lambda a, b: a + b, [w[p] * l_s[p, rows, :] for p in range(N_PAT)])
        y_ref[rows, :] = (num / den * _silu(z_ref[rows, :])).astype(BF16)
        return carry
    lax.fori_loop(0, seq // norm_rows, out_chunk, 0)


def _attn_prompt(proj, gq2, gk2, bias_p):
    b, seq, _ = proj.shape
    keep = min(WIN_MAX, seq)
    assert seq % (BLK_A * PATTERNS[-1][1]) == 0
    col = lambda base: pl.BlockSpec((None, seq, LANES), lambda bi, hp: (bi, 0, base + hp))
    vec = pl.BlockSpec((1, LANES), lambda bi, hp: (0, 0))
    return pl.pallas_call(
        functools.partial(_attn_prompt_kernel, seq=seq, keep=keep, unroll=8),
        out_shape=(jax.ShapeDtypeStruct((b, seq, D_A), BF16),
                   jax.ShapeDtypeStruct((b, D_A, keep), F32),
                   jax.ShapeDtypeStruct((b, D_A, keep), F32)),
        grid=(b, H_A // HPAIR),
        in_specs=[col(COL_Q), col(COL_K), col(COL_V), col(COL_Z), vec, vec,
                  pl.BlockSpec((HPAIR, N_PAT, 2, BLK_A, 2 * BLK_A), lambda bi, hp: (hp, 0, 0, 0, 0))],
        out_specs=(pl.BlockSpec((None, seq, LANES), lambda bi, hp: (bi, 0, hp)),
                   pl.BlockSpec((None, LANES, keep), lambda bi, hp: (bi, hp, 0)),
                   pl.BlockSpec((None, LANES, keep), lambda bi, hp: (bi, hp, 0))),
        scratch_shapes=[pltpu.VMEM((seq, LANES), F32),
                        pltpu.VMEM((PAD_A + seq, LANES), F32),
                        pltpu.VMEM((PAD_A + seq, LANES), F32),
                        pltpu.VMEM((2 * BLK_A, LANES), BF16),
                        pltpu.VMEM((N_PAT, seq, LANES), F32),
                        pltpu.VMEM((N_PAT, seq, LANES), F32),
                        pltpu.VMEM((N_PAT, seq, LANES), F32)],
        compiler_params=_cparams(("arbitrary", "arbitrary")),
        name="attn_prompt",
    )(proj, proj, proj, proj, gq2, gk2, bias_p)


def _attn_sample_kernel(q_ref, k_ref, v_ref, z_ref, ck_ref, cv_ref, gq_ref, gk_ref, bias_ref,
                        y_ref, nk_ref, nv_ref, *, n_past, t_new):
    q_all = q_ref[...]
    k_all = k_ref[...]
    v_all = v_ref[...]
    z_all = z_ref[...]
    nv_ref[...] = v_all
    pad_rows = jnp.zeros((LANES - t_new, HEAD_DIM_A), F32)

    def norm(x, g):
        ms = jnp.sum(x * x, axis=-1, keepdims=True) * (1.0 / HEAD_DIM_A)
        return x * lax.rsqrt(ms + EPS) * g

    for h in range(H_A):
        cols = slice(h * HEAD_DIM_A, (h + 1) * HEAD_DIM_A)
        qn = (norm(q_all[:, cols], gq_ref[:, cols]) * (HEAD_DIM_A ** -0.5)).astype(BF16)
        kn = norm(k_all[:, cols], gk_ref[:, cols])
        nk_ref[:, cols] = kn
        k_new = jnp.concatenate([kn, pad_rows], axis=0).astype(BF16)
        v_new = jnp.concatenate([v_all[:, cols], pad_rows], axis=0).astype(BF16)
        k_t = ck_ref[h].astype(BF16)
        v_t = cv_ref[h].astype(BF16)
        s_c = jnp.dot(qn, k_t, preferred_element_type=F32)
        s_n = _bdot_nt(qn, k_new)
        sc = [s_c + bias_ref[h, p, :, 0:n_past] for p in range(N_PAT)]
        sn = [s_n + bias_ref[h, p, :, n_past:] for p in range(N_PAT)]
        m = functools.reduce(jnp.maximum, [jnp.max(x, axis=-1, keepdims=True) for x in sc + sn])
        e_c = functools.reduce(lambda a, b: a + b, [jnp.exp(x - m) for x in sc])
        e_n = functools.reduce(lambda a, b: a + b, [jnp.exp(x - m) for x in sn])
        den = jnp.sum(e_c, axis=-1, keepdims=True) + jnp.sum(e_n, axis=-1, keepdims=True)
        o = _bdot_nt(e_c, v_t) + jnp.dot(e_n.astype(BF16), v_new, preferred_element_type=F32)
        y_ref[:, cols] = (o / den * _silu(z_all[:, cols])).astype(BF16)


def _attn_sample(proj_s, cache_kt, cache_vt, gq, gk, bias_s, layer):
    bd, t_new, _ = proj_s.shape
    n_past = cache_kt.shape[-1]
    col = lambda base: pl.BlockSpec((None, t_new, D_A), lambda bi: (bi, 0, base))
    vec = pl.BlockSpec((1, D_A), lambda bi: (0, 0))
    cache = pl.BlockSpec((None, None, H_A, HEAD_DIM_A, n_past), lambda bi: (layer, bi, 0, 0, 0))
    out = pl.BlockSpec((None, t_new, D_A), lambda bi: (bi, 0, 0))
    return pl.pallas_call(
        functools.partial(_attn_sample_kernel, n_past=n_past, t_new=t_new),
        out_shape=(jax.ShapeDtypeStruct((bd, t_new, D_A), BF16),
                   jax.ShapeDtypeStruct((bd, t_new, D_A), F32),
                   jax.ShapeDtypeStruct((bd, t_new, D_A), F32)),
        grid=(bd,),
        in_specs=[col(0), col(1), col(2), col(3), cache, cache, vec, vec,
                  pl.BlockSpec((H_A, N_PAT, t_new, n_past + LANES), lambda bi: (0, 0, 0, 0))],
        out_specs=(out, out, out),
        compiler_params=_cparams(("arbitrary",)),
        name="attn_sample",
    )(proj_s, proj_s, proj_s, proj_s, cache_kt, cache_vt, gq, gk, bias_s)


def _sgu_prompt_kernel(u_ref, v_ref, z_ref, g_ref, w_ref, bt_ref, y_ref, *, chunks):
    row = lax.broadcasted_iota(jnp.int32, (CHUNK_B, CHUNK_B), 0)
    colm = lax.broadcasted_iota(jnp.int32, (CHUNK_B, CHUNK_B), 1)
    tril = row >= colm
    ws = [jnp.where(tril, w_ref[g], 0.0).astype(BF16) for g in range(G_B)]
    for c in range(chunks):
        rows = slice(c * CHUNK_B, (c + 1) * CHUNK_B)
        v = v_ref[rows, :]
        ms = jnp.sum(v * v, axis=-1, keepdims=True) * (1.0 / D_B)
        vn = (v * lax.rsqrt(ms + EPS) * g_ref[...]).astype(BF16)
        for g in range(G_B):
            cols = slice(g * C_B, (g + 1) * C_B)
            mix = jnp.dot(ws[g], vn[:, cols], preferred_element_type=F32) + bt_ref[:, g:g + 1]
            y_ref[rows, cols] = (u_ref[rows, cols] * mix * _silu(z_ref[rows, cols])).astype(BF16)


def _sgu_prompt(proj, sgu_g, sgu_w, sgu_bt):
    b, seq, _ = proj.shape
    ts = 512
    col = lambda base: pl.BlockSpec((None, ts, D_B), lambda bi, i: (bi, i, base))
    return pl.pallas_call(
        functools.partial(_sgu_prompt_kernel, chunks=ts // CHUNK_B),
        out_shape=jax.ShapeDtypeStruct((b, seq, D_B), BF16),
        grid=(b, seq // ts),
        in_specs=[col(COL_UB), col(COL_VB), col(COL_ZB),
                  pl.BlockSpec((1, D_B), lambda bi, i: (0, 0)),
                  pl.BlockSpec((G_B, CHUNK_B, CHUNK_B), lambda bi, i: (0, 0, 0)),
                  pl.BlockSpec((CHUNK_B, G_B), lambda bi, i: (0, 0))],
        out_specs=pl.BlockSpec((None, ts, D_B), lambda bi, i: (bi, i, 0)),
        compiler_params=_cparams(("arbitrary", "arbitrary")),
        name="sgu_prompt",
    )(proj, proj, proj, sgu_g, sgu_w, sgu_bt)


def _sgu_sample_kernel(u_ref, v_ref, z_ref, g_ref, wexp_ref, bexp_ref, y_ref, vn_ref, *, t_new):
    v = v_ref[...]
    ms = jnp.sum(v * v, axis=-1, keepdims=True) * (1.0 / D_B)
    vn = v * lax.rsqrt(ms + EPS) * g_ref[...]
    vn_ref[...] = vn
    mix = bexp_ref[...]
    for s in range(t_new):
        mix = mix + wexp_ref[s] * vn[s:s + 1, :]
    y_ref[...] = (u_ref[...] * mix * _silu(z_ref[...])).astype(BF16)


def _sgu_sample(proj_s, sgu_g, wexp, bexp):
    bd, t_new, _ = proj_s.shape
    col = lambda base: pl.BlockSpec((None, t_new, D_B), lambda bi: (bi, 0, base))
    return pl.pallas_call(
        functools.partial(_sgu_sample_kernel, t_new=t_new),
        out_shape=(jax.ShapeDtypeStruct((bd, t_new, D_B), BF16),
                   jax.ShapeDtypeStruct((bd, t_new, D_B), F32)),
        grid=(bd,),
        in_specs=[col(COL_UB), col(COL_VB), col(COL_ZB),
                  pl.BlockSpec((1, D_B), lambda bi: (0, 0)),
                  pl.BlockSpec((t_new, t_new, D_B), lambda bi: (0, 0, 0)),
                  pl.BlockSpec((t_new, D_B), lambda bi: (0, 0))],
        out_specs=(pl.BlockSpec((None, t_new, D_B), lambda bi: (bi, 0, 0)),
                   pl.BlockSpec((None, t_new, D_B), lambda bi: (bi, 0, 0))),
        compiler_params=_cparams(("arbitrary",)),
        name="sgu_sample",
    )(proj_s, proj_s, proj_s, sgu_g, wexp, bexp)


def _log_sigmoid(x):
    return jnp.minimum(x, 0.0) - jnp.log1p(jnp.exp(-jnp.abs(x)))


def _mlstm_kernel(qk_ref, v_ref, o_ref, z_ref, gt_ref, cbuf_ref, c0_ref, n0_ref, m0_ref,
                  cw_ref, cb_ref, gb_ref, hn_ref,
                  y_ref, conv_out_ref, c_out_ref, n_out_ref, m_out_ref,
                  xp_s, vp_s, gp_s, caug_s, m_s, *, t_rows):
    L = CHUNK_C
    ci = pl.program_id(1)
    last = pl.num_programs(1) - 1
    base = SUBLANES
    carry_rows = CONV_W - 1
    row_l = lax.broadcasted_iota(jnp.int32, (L, LANES), 0)
    lane_l = lax.broadcasted_iota(jnp.int32, (L, LANES), 1)
    zero_rows = jnp.zeros((L - SUBLANES, LANES), F32)

    def to_cols(x8):
        return jnp.concatenate([x8, zero_rows], axis=0).T

    @pl.when(ci == 0)
    def _():
        xp_s[...] = jnp.zeros(xp_s.shape, F32)
        vp_s[...] = jnp.zeros(vp_s.shape, F32)
        gp_s[...] = jnp.zeros(gp_s.shape, F32)
        xp_s[base - carry_rows:base, :] = cbuf_ref[...]
        n0_cols = to_cols(n0_ref[...])
        for h in range(H_C):
            caug_s[h, :, 0:DH_C] = c0_ref[h]
            caug_s[h, :, DH_C:] = jnp.broadcast_to(n0_cols[:, h:h + 1], (DH_C, DH_C))
        m_s[...] = m0_ref[...]

    xp_s[base:base + t_rows, :] = qk_ref[...]
    vp_s[0:t_rows, :] = v_ref[...]
    gp_s[0:t_rows, :] = gt_ref[...]

    acc = cb_ref[...] + cw_ref[CONV_W - 1:CONV_W, :] * xp_s[base:base + L, :]
    for j in range(CONV_W - 1):
        off = base - carry_rows + j
        acc = acc + cw_ref[j:j + 1, :] * xp_s[off:off + L, :]
    qk = _silu(acc)
    new_carry = xp_s[base + t_rows - carry_rows:base + t_rows, :]
    xp_s[base - carry_rows:base, :] = new_carry

    gates_t = (gp_s[...] + gb_ref[...]).T
    lane_t = lax.broadcasted_iota(jnp.int32, (SUBLANES, L), 1)
    real_t = lane_t < t_rows
    i_t = jnp.where(real_t, gates_t[0:SUBLANES, :], NEG)
    logf_t = jnp.where(real_t, _log_sigmoid(gates_t[SUBLANES:2 * SUBLANES, :]), 0.0)
    upper = (row_l <= lane_l).astype(BF16)
    f1 = logf_t.astype(BF16)
    r1 = logf_t - f1.astype(F32)
    f2 = r1.astype(BF16)
    f3 = (r1 - f2.astype(F32)).astype(BF16)
    b_t = (jnp.dot(f1, upper, preferred_element_type=F32) + jnp.dot(f2, upper, preferred_element_type=F32)
           + jnp.dot(f3, upper, preferred_element_type=F32))
    m_prev = m_s[...]
    c_t = i_t - b_t
    pm = c_t
    shift = 1
    while shift < L:
        pm = jnp.maximum(pm, jnp.where(lane_t >= shift, pltpu.roll(pm, shift, axis=1), NEG))
        shift *= 2
    d_t = -jnp.maximum(m_prev, pm)
    wi_t = jnp.exp(m_prev + d_t)
    em_t = jnp.exp(-(b_t - d_t))
    bl = jnp.broadcast_to(b_t[:, L - 1:L], (SUBLANES, L))
    g_t = bl - b_t + i_t
    m_new = jnp.maximum(bl + m_prev, jnp.broadcast_to(jnp.max(g_t, axis=1, keepdims=True), (SUBLANES, L)))
    ws_t = jnp.exp(g_t - m_new)
    wc = jnp.exp(bl + m_prev - m_new)
    m_s[...] = m_new
    stats = jnp.concatenate([d_t, wi_t, em_t, ws_t, zero_rows[0:L - 4 * SUBLANES]], axis=0).T

    causal = row_l >= lane_l
    ones_blk = jnp.ones((L, DH_C), BF16)
    v_all = vp_s[...]
    o_all = o_ref[...]
    z_all = z_ref[...]

    for h in range(H_C):
        cols = slice(h * DH_C, (h + 1) * DH_C)
        q = qk[:, cols].astype(BF16)
        k = qk[:, D_C + h * DH_C:D_C + (h + 1) * DH_C] * (DH_C ** -0.5)
        v_aug = jnp.concatenate([v_all[:, cols].astype(BF16), ones_blk], axis=1)
        d_col = stats[:, h:h + 1]
        wi_col = stats[:, SUBLANES + h:SUBLANES + h + 1]
        em_col = stats[:, 2 * SUBLANES + h:2 * SUBLANES + h + 1]
        ws_col = stats[:, 3 * SUBLANES + h:3 * SUBLANES + h + 1]
        c_prev = caug_s[h]

        a = jnp.exp(jnp.where(causal, d_col + c_t[h:h + 1, :], NEG)) * _bdot_nt(q, k)
        r1_ = jnp.dot(a.astype(BF16), v_aug, preferred_element_type=F32)
        r2_ = jnp.dot(q, c_prev.astype(BF16), preferred_element_type=F32)
        num = r1_[:, 0:DH_C] + wi_col * r2_[:, 0:DH_C]
        den = r1_[:, DH_C:] + wi_col * r2_[:, DH_C:]
        hh = num / jnp.maximum(jnp.abs(den), em_col)

        wc_row = jnp.concatenate([wc[h:h + 1, :], wc[h:h + 1, :]], axis=1)
        caug_s[h] = wc_row * c_prev + _bdot_tn(k * ws_col, v_aug)

        ms = jnp.sum(hh * hh, axis=-1, keepdims=True) * (1.0 / DH_C)
        hn = hh * lax.rsqrt(ms + EPS) * hn_ref[:, cols]
        z = z_all[:, cols]
        gate = z / ((1.0 + jnp.exp(-o_all[:, cols])) * (1.0 + jnp.exp(-z)))
        y_ref[:, cols] = (hn[0:t_rows, :] * gate).astype(BF16)

    @pl.when(ci == last)
    def _():
        conv_out_ref[...] = new_carry
        n_cols = jnp.zeros((DH_C, LANES), F32)
        for h in range(H_C):
            c_out_ref[h] = caug_s[h, :, 0:DH_C]
            n_cols = jnp.where(lane_l == h, caug_s[h, :, DH_C:], n_cols)
        n_out_ref[...] = n_cols.T[0:SUBLANES, :]
        m_out_ref[...] = m_s[...]


def _mlstm(proj, gates, conv_buf, c0, n0, m0, conv_w, conv_b, gate_bias, hn_g, t_rows):
    b, seq, _ = proj.shape
    nchunks = seq // t_rows
    hp = SUBLANES
    col = lambda base, width: pl.BlockSpec((None, t_rows, width), lambda bi, ci: (bi, ci, base))
    const2 = lambda shape: pl.BlockSpec(shape, lambda bi, ci: (0, 0))
    per_b3 = lambda shape: pl.BlockSpec((None,) + shape, lambda bi, ci: (bi, 0, 0))
    per_b4 = lambda shape: pl.BlockSpec((None,) + shape, lambda bi, ci: (bi, 0, 0, 0))
    return pl.pallas_call(
        functools.partial(_mlstm_kernel, t_rows=t_rows),
        out_shape=(jax.ShapeDtypeStruct((b, seq, D_C), BF16),
                   jax.ShapeDtypeStruct((b, CONV_W - 1, 2 * D_C), F32),
                   jax.ShapeDtypeStruct((b, H_C, DH_C, DH_C), F32),
                   jax.ShapeDtypeStruct((b, hp, DH_C), F32),
                   jax.ShapeDtypeStruct((b, hp, LANES), F32)),
        grid=(b, nchunks),
        in_specs=[col(COL_QK, 2 * D_C), col(COL_VC, D_C), col(COL_OC, D_C), col(COL_ZC, D_C),
                  pl.BlockSpec((None, t_rows, LANES), lambda bi, ci: (bi, ci, 0)),
                  per_b3((CONV_W - 1, 2 * D_C)), per_b4((H_C, DH_C, DH_C)), per_b3((hp, DH_C)),
                  per_b3((hp, LANES)),
                  const2((CONV_W, 2 * D_C)), const2((1, 2 * D_C)), const2((1, LANES)), const2((1, D_C))],
        out_specs=(pl.BlockSpec((None, t_rows, D_C), lambda bi, ci: (bi, ci, 0)),
                   per_b3((CONV_W - 1, 2 * D_C)), per_b4((H_C, DH_C, DH_C)), per_b3((hp, DH_C)),
                   per_b3((hp, LANES))),
        scratch_shapes=[pltpu.VMEM((SUBLANES + CHUNK_C, 2 * D_C), F32),
                        pltpu.VMEM((CHUNK_C, D_C), F32),
                        pltpu.VMEM((CHUNK_C, LANES), F32),
                        pltpu.VMEM((H_C, DH_C, 2 * DH_C), F32),
                        pltpu.VMEM((hp, LANES), F32)],
        compiler_params=_cparams(("arbitrary", "arbitrary")),
        name="mlstm",
    )(proj, proj, proj, proj, gates, conv_buf, c0, n0, m0, conv_w, conv_b, gate_bias, hn_g)


def _gate_lanes(i_part, f_part):
    lead = i_part.shape[:-1]
    gap = jnp.zeros(lead + (SUBLANES - H_C,), i_part.dtype)
    tail = jnp.zeros(lead + (LANES - 2 * SUBLANES,), i_part.dtype)
    return jnp.concatenate([i_part, gap, f_part, gap, tail], axis=-1)


def _pad_heads(a):
    pad = [(0, 0)] * a.ndim
    pad[1] = (0, SUBLANES - a.shape[1])
    return jnp.pad(a, pad)


def _mixers(proj, gates, lw, attn_fn, sgu_fn, conv_buf, c0, n0, m0, t_rows):
    ya, nk, nv = attn_fn(proj)
    sgu_out = sgu_fn(proj)
    m0b = jnp.broadcast_to(_pad_heads(m0)[:, :, None], (m0.shape[0], SUBLANES, LANES))
    yc, nconv, c_new, n_new, m_new = _mlstm(proj, gates, conv_buf, c0, _pad_heads(n0), m0b,
                                            lw["conv_w"], lw["conv_b"], lw["gate_bias"], lw["hn_g"], t_rows)
    return ya, sgu_out, yc, nk, nv, nconv, c_new, n_new[:, :H_C], m_new[:, :H_C, 0]


def kernel(x_prompt, x_sample, c_prompt, c_sample, cache_k_win, cache_v_win, state_conv, state_C, state_n, state_m, rel_bias, norm_g, ada_w, ada_b, w_in, qn_g, kn_g, sgu_g, sgu_w, sgu_b, conv_w, conv_b, f_bias, i_bias, hn_g, w_out):
    depth = w_in.shape[0]
    bp, seq, _ = x_prompt.shape
    bd, t_new, _ = x_sample.shape
    n_past = cache_k_win.shape[2]
    assert t_new == SUBLANES and n_past % LANES == 0 and seq % 1024 == 0

    w_in_t = _unpack_w_in(w_in)
    w_gate = _gate_lanes(w_in[:, :, D_MAIN:D_MAIN + H_C], w_in[:, :, D_MAIN + H_C:])
    w_out_bf = w_out.astype(BF16)

    rel_t = rel_bias.T
    bias_p = _expand_bias(rel_t, jnp.asarray(_prompt_bucket_table()), 64, LOG2E)
    bias_p = bias_p.reshape(H_A, N_PAT, 2, BLK_A, 2 * BLK_A)
    bias_s = _expand_bias(rel_t, jnp.asarray(_sample_bucket_table(n_past, t_new)), SUBLANES, 1.0)
    bias_s = bias_s.reshape(H_A, N_PAT, t_new, n_past + LANES)

    n_c = bp + bd
    c_all = jnp.pad(jnp.concatenate([c_prompt, c_sample], axis=0), ((0, 2 * SUBLANES - n_c), (0, 0)))
    mod = _ada_mod(c_all, ada_w, ada_b)

    cache_kt = jnp.transpose(cache_k_win, (0, 1, 3, 4, 2))
    cache_vt = jnp.transpose(cache_v_win, (0, 1, 3, 4, 2))
    gq_all = [jnp.tile(qn_g[l], H_A)[None] for l in range(depth)]
    gk_all = [jnp.tile(kn_g[l], H_A)[None] for l in range(depth)]
    zeros_conv = jnp.zeros((bp, CONV_W - 1, 2 * D_C), F32)
    zeros_c = jnp.zeros((bp, H_C, DH_C, DH_C), F32)
    zeros_n = jnp.zeros((bp, H_C, DH_C), F32)
    zeros_m = jnp.zeros((bp, H_C), F32)
    tril8 = jnp.tril(jnp.ones((t_new, t_new), F32))

    xp = x_prompt.reshape(bp * seq, D_MODEL)
    xs = x_sample.reshape(bd * t_new, D_MODEL)
    outs = [[] for _ in range(13)]
    p_mods, s_mods = [], []
    for l in range(depth):
        shift, scale, gate = (mod[l, :, i * D_MODEL:(i + 1) * D_MODEL] for i in range(3))
        p_mods.append([a[:bp, None, :] for a in (scale, shift, gate)])
        s_mods.append([jnp.repeat(a[bp:n_c], t_new, axis=0)[None] for a in (scale, shift, gate)])
    hp = _norm(xp, p_mods[0][0], p_mods[0][1], norm_g[0][None], 512)
    hs = _norm(xs, s_mods[0][0], s_mods[0][1], norm_g[0][None], bd * t_new)
    for l in range(depth):
        p_mod, s_mod = p_mods[l], s_mods[l]
        last = l == depth - 1
        p_next = None if last else (p_mods[l + 1][0], p_mods[l + 1][1], norm_g[l + 1][None])
        s_next = None if last else (s_mods[l + 1][0], s_mods[l + 1][1], norm_g[l + 1][None])
        lw = {
            "conv_w": conv_w[l], "conv_b": conv_b[l][None],
            "gate_bias": _gate_lanes(i_bias[l], f_bias[l])[None],
            "hn_g": hn_g[l][None],
        }
        gq2 = jnp.tile(qn_g[l], HPAIR)[None]
        gk2 = jnp.tile(kn_g[l], HPAIR)[None]
        sg = sgu_g[l][None]

        proj, gates = _inproj(hp, w_in_t, w_gate, l, 2048)
        proj = proj.reshape(bp, seq, D_MAIN)
        gates = gates.reshape(bp, seq, LANES)
        ya, yb, yc, nk, nv, ncv, nc_, nn_, nm = _mixers(
            proj, gates, lw,
            lambda pr: _attn_prompt(pr, gq2, gk2, bias_p),
            lambda pr: _sgu_prompt(pr, sg, sgu_w[l], sgu_b[l].T),
            zeros_conv, zeros_c, zeros_n, zeros_m, CHUNK_C)
        xp, hp = _outproj(ya.reshape(bp * seq, D_A), yb.reshape(bp * seq, D_B), yc.reshape(bp * seq, D_C),
                          xp, p_mod[2], w_out_bf, l, 512, p_next)
        keep = nk.shape[2]
        for i, a in enumerate((nk.reshape(bp, H_A, HEAD_DIM_A, keep), nv.reshape(bp, H_A, HEAD_DIM_A, keep),
                               ncv, nc_, nn_, nm)):
            outs[i].append(a)

        proj_s, gates_s = _inproj(hs, w_in_t, w_gate, l, bd * t_new)
        proj_s = proj_s.reshape(bd, t_new, D_MAIN)
        gates_s = gates_s.reshape(bd, t_new, LANES)
        w8 = sgu_w[l][:, :t_new, :t_new] * tril8
        wexp = jnp.repeat(jnp.transpose(w8, (2, 1, 0)), C_B, axis=2)
        bexp = jnp.repeat(sgu_b[l][:, :t_new].T, C_B, axis=1)
        ya, sgu_out, yc, nk, nv, ncv, nc_, nn_, nm = _mixers(
            proj_s, gates_s, lw,
            lambda pr: _attn_sample(pr, cache_kt, cache_vt, gq_all[l], gk_all[l], bias_s, l),
            lambda pr: _sgu_sample(pr, sg, wexp, bexp),
            state_conv[l], state_C[l], state_n[l], state_m[l], t_new)
        yb, vn = sgu_out
        xs, hs = _outproj(ya.reshape(bd * t_new, D_A), yb.reshape(bd * t_new, D_B), yc.reshape(bd * t_new, D_C),
                          xs, s_mod[2], w_out_bf, l, bd * t_new, s_next)
        for i, a in enumerate((nk.reshape(bd, t_new, H_A, HEAD_DIM_A), nv.reshape(bd, t_new, H_A, HEAD_DIM_A),
                               vn, ncv, nc_, nn_, nm)):
            outs[6 + i].append(a)

    stacked = [jnp.stack(o) for o in outs]
    for i in (0, 1):
        stacked[i] = jnp.transpose(stacked[i], (0, 1, 4, 2, 3))
    return (xp.reshape(bp, seq, D_MODEL), xs.reshape(bd, t_new, D_MODEL), *stacked)
```

```python
import functools
import math

import numpy as np
import jax
import jax.numpy as jnp
from jax import lax
from jax.experimental import pallas as pl
from jax.experimental.pallas import tpu as pltpu

F32 = jnp.float32
BF16 = jnp.bfloat16

D_MODEL = 2048
HEAD_DIM_A = 64
D_A = 768
H_A = 12
D_B = 512
G_B = 4
C_B = 128
CHUNK_B = 128
D_C = 768
DH_C = 128
H_C = 6
CHUNK_C = 128
CONV_W = 4
PATTERNS = ((128, 1), (512, 4), (2048, 16))
N_PAT = len(PATTERNS)
WIN_MAX = 2048
BLK_A = 128
N_BUCKETS = 32
MAX_DIST = 2048
EPS = 1e-6
D_MAIN = 4 * D_A + 3 * D_B + 2 * D_C + 3 * D_C
D_IN = D_MAIN + 2 * H_C

LANES = 128
SUBLANES = 8
VMEM_LIMIT = 56 * 1024 * 1024

NEG = -1e30
LOG2E = 1.4426950408889634
PAD_A = BLK_A * PATTERNS[-1][1]
HPAIR = LANES // HEAD_DIM_A

COL_Q, COL_K, COL_V, COL_Z = 0, D_A // LANES, 2 * D_A // LANES, 3 * D_A // LANES
COL_UB, COL_VB, COL_ZB = 4 * D_A // D_B, 4 * D_A // D_B + 1, 4 * D_A // D_B + 2
COL_QK = (4 * D_A + 3 * D_B) // (2 * D_C)
COL_VC = (4 * D_A + 3 * D_B + 2 * D_C) // D_C
COL_OC, COL_ZC = COL_VC + 1, COL_VC + 2


def _cparams(sem):
    return pltpu.CompilerParams(dimension_semantics=sem, vmem_limit_bytes=VMEM_LIMIT)


def _silu(x):
    return x * jax.nn.sigmoid(x)


def _bdot(a, b):
    return jnp.dot(a.astype(BF16), b.astype(BF16), preferred_element_type=F32)


def _bdot_nt(a, b):
    return lax.dot_general(a.astype(BF16), b.astype(BF16), (((1,), (1,)), ((), ())),
                           preferred_element_type=F32)


def _bdot_tn(a, b):
    return lax.dot_general(a.astype(BF16), b.astype(BF16), (((0,), (0,)), ((), ())),
                           preferred_element_type=F32)


def _bucket_np(dist):
    max_exact = N_BUCKETS // 2
    df = np.maximum(dist, 1).astype(np.float32)
    large = max_exact + (np.log(df / np.float32(max_exact)) / np.float32(math.log(MAX_DIST / max_exact))
                         * np.float32(N_BUCKETS - max_exact)).astype(np.int32)
    return np.where(dist < max_exact, dist, np.minimum(large, N_BUCKETS - 1)).astype(np.int32)


def _prompt_bucket_table():
    qi = np.arange(BLK_A)[:, None]
    ki = np.arange(2 * BLK_A)[None, :]
    j = qi + BLK_A - ki
    out = []
    for win, dil in PATTERNS:
        n_back = win // dil
        band = (j >= 0) & (j <= n_back)
        b = _bucket_np(np.clip(j, 0, n_back) * dil)
        out.append(np.where(band, b, -1))
        out.append(np.where(band & (ki >= BLK_A), b, -1))
    return np.stack(out).reshape(N_PAT * 2 * BLK_A, 2 * BLK_A).astype(np.int32)


def _sample_bucket_table(n_past, t_new):
    c = np.arange(n_past + LANES)[None, :]
    t = np.arange(t_new)[:, None]
    delta = n_past + t - c
    out = []
    for win, dil in PATTERNS:
        valid = (c < n_past + t_new) & (delta >= 0) & (delta % dil == 0) & (delta // dil <= win // dil)
        out.append(np.where(valid, _bucket_np(np.maximum(delta, 0)), -1))
    return np.stack(out).reshape(N_PAT * t_new, n_past + LANES).astype(np.int32)


def _bias_kernel(rb_ref, idx_ref, out_ref, *, scale, row_chunk):
    h = pl.program_id(0)

    def chunk(c, carry):
        rows = pl.ds(pl.multiple_of(c * row_chunk, row_chunk), row_chunk)
        idx = idx_ref[rows, :]
        out = jnp.full(idx.shape, NEG, F32)
        for b in range(N_BUCKETS):
            out = jnp.where(idx == b, rb_ref[h, b] * scale, out)
        out_ref[rows, :] = out
        return carry
    lax.fori_loop(0, idx_ref.shape[0] // row_chunk, chunk, 0)


def _expand_bias(rel_bias_t, idx, row_chunk, scale):
    rows, cols = idx.shape
    return pl.pallas_call(
        functools.partial(_bias_kernel, scale=scale, row_chunk=row_chunk),
        out_shape=jax.ShapeDtypeStruct((H_A, rows, cols), F32),
        grid=(H_A,),
        in_specs=[pl.BlockSpec(memory_space=pltpu.SMEM),
                  pl.BlockSpec((rows, cols), lambda h: (0, 0))],
        out_specs=pl.BlockSpec((None, rows, cols), lambda h: (h, 0, 0)),
        compiler_params=_cparams(("arbitrary",)),
        name="bias_expand",
    )(rel_bias_t, idx)


def _ada_kernel(c_ref, w_ref, b_ref, o_ref):
    c = c_ref[...]
    a = _silu(c)
    w = w_ref[...]
    a_hi = a.astype(BF16)
    a_lo = (a - a_hi.astype(F32)).astype(BF16)
    w_hi = w.astype(BF16)
    w_lo = (w - w_hi.astype(F32)).astype(BF16)
    acc = jnp.dot(a_hi, w_hi, preferred_element_type=F32)
    acc += jnp.dot(a_hi, w_lo, preferred_element_type=F32)
    acc += jnp.dot(a_lo, w_hi, preferred_element_type=F32)
    o_ref[...] = acc + b_ref[...]


def _ada_mod(c_all, ada_w, ada_b):
    depth = ada_w.shape[0]
    rows = c_all.shape[0]
    tn = 768
    n = 3 * D_MODEL
    return pl.pallas_call(
        _ada_kernel,
        out_shape=jax.ShapeDtypeStruct((depth, rows, n), F32),
        grid=(depth, n // tn),
        in_specs=[pl.BlockSpec((rows, D_MODEL), lambda l, j: (0, 0)),
                  pl.BlockSpec((None, D_MODEL, tn), lambda l, j: (l, 0, j)),
                  pl.BlockSpec((None, 1, tn), lambda l, j: (l, 0, j))],
        out_specs=pl.BlockSpec((None, rows, tn), lambda l, j: (l, 0, j)),
        compiler_params=_cparams(("arbitrary", "arbitrary")),
        name="ada_mod",
    )(c_all, ada_w, ada_b.reshape(depth, 1, n))


def _norm_mod(x, g, sc, sh):
    ms = jnp.sum(x * x, axis=-1, keepdims=True) * (1.0 / D_MODEL)
    return ((x * lax.rsqrt(ms + EPS) * g) * (1.0 + sc) + sh).astype(BF16)


def _norm_kernel(x_ref, sc_ref, sh_ref, g_ref, h_ref):
    h_ref[...] = _norm_mod(x_ref[...], g_ref[...], sc_ref[...], sh_ref[...])


def _norm(x2d, scale, shift, norm_g, tm):
    m = x2d.shape[0]
    tiles_per_group = m // tm // scale.shape[0]
    mod_spec = pl.BlockSpec((None, scale.shape[1], D_MODEL), lambda i: (i // tiles_per_group, 0, 0))
    return pl.pallas_call(
        _norm_kernel,
        out_shape=jax.ShapeDtypeStruct((m, D_MODEL), BF16),
        grid=(m // tm,),
        in_specs=[pl.BlockSpec((tm, D_MODEL), lambda i: (i, 0)), mod_spec, mod_spec,
                  pl.BlockSpec((1, D_MODEL), lambda i: (0, 0))],
        out_specs=pl.BlockSpec((tm, D_MODEL), lambda i: (i, 0)),
        compiler_params=_cparams(("arbitrary",)),
        name="norm",
    )(x2d, scale, shift, norm_g)


def _inproj_kernel(h_ref, w_ref, wg_ref, proj_ref, gates_ref):
    @pl.when(pl.program_id(1) == 0)
    def _():
        gates_ref[...] = jnp.dot(h_ref[...], wg_ref[...].astype(BF16), preferred_element_type=F32)

    proj_ref[...] = lax.dot_general(h_ref[...], w_ref[...], (((1,), (1,)), ((), ())),
                                    preferred_element_type=F32)


UNPACK_PITCH = 72


def _unpack_kernel(x_hbm, o_ref, buf_a, buf_b, sem, *, depth, k_tiles, tn):
    i = pl.program_id(0)
    rows_per_col = k_tiles * depth

    def copy(tile, n, buf, slot):
        return pltpu.make_async_copy(x_hbm.at[tile * tn + n], buf.at[pl.ds(n * UNPACK_PITCH, rows_per_col)],
                                     sem.at[slot])

    def start_all(tile, buf, slot):
        def body(n, carry):
            copy(tile, n, buf, slot).start()
            return carry
        lax.fori_loop(0, tn, body, 0, unroll=8)

    def wait_all(tile, buf, slot):
        def body(n, carry):
            copy(tile, n, buf, slot).wait()
            return carry
        lax.fori_loop(0, tn, body, 0, unroll=8)

    def convert(buf, half):
        for l in range(depth):
            for kt in range(k_tiles):
                rows = pl.ds(kt * depth + l, tn, stride=UNPACK_PITCH)
                o_ref[l, half * tn:(half + 1) * tn, kt * LANES:(kt + 1) * LANES] = buf[rows, :].astype(BF16)

    @pl.when(i == 0)
    def _():
        start_all(0, buf_a, 0)

    start_all(2 * i + 1, buf_b, 1)
    wait_all(2 * i, buf_a, 0)
    convert(buf_a, 0)

    @pl.when(i + 1 < pl.num_programs(0))
    def _():
        start_all(2 * i + 2, buf_a, 0)

    wait_all(2 * i + 1, buf_b, 1)
    convert(buf_b, 1)


def _unpack_w_in(w_in):
    depth, d, d_in = w_in.shape
    k_tiles = d // LANES
    tn = LANES
    assert D_MAIN % (2 * tn) == 0 and k_tiles * depth <= UNPACK_PITCH
    cols = w_in.reshape(depth, k_tiles, LANES, d_in).transpose(3, 1, 0, 2).reshape(d_in, k_tiles * depth, LANES)
    return pl.pallas_call(
        functools.partial(_unpack_kernel, depth=depth, k_tiles=k_tiles, tn=tn),
        out_shape=jax.ShapeDtypeStruct((depth, D_MAIN, d), BF16),
        grid=(D_MAIN // (2 * tn),),
        in_specs=[pl.BlockSpec(memory_space=pl.ANY)],
        out_specs=pl.BlockSpec((depth, 2 * tn, d), lambda i: (0, i, 0)),
        scratch_shapes=[pltpu.VMEM((tn * UNPACK_PITCH, LANES), F32),
                        pltpu.VMEM((tn * UNPACK_PITCH, LANES), F32),
                        pltpu.SemaphoreType.DMA((2,))],
        compiler_params=_cparams(("arbitrary",)),
        name="unpack_w_in",
    )(cols)


def _inproj(h2d, w_in_t, w_gate, layer, tm):
    m = h2d.shape[0]
    tn = 768
    return pl.pallas_call(
        _inproj_kernel,
        out_shape=(jax.ShapeDtypeStruct((m, D_MAIN), F32), jax.ShapeDtypeStruct((m, LANES), F32)),
        grid=(m // tm, D_MAIN // tn),
        in_specs=[pl.BlockSpec((tm, D_MODEL), lambda i, j: (i, 0)),
                  pl.BlockSpec((None, tn, D_MODEL), lambda i, j: (layer, j, 0)),
                  pl.BlockSpec((None, D_MODEL, LANES), lambda i, j: (layer, 0, 0))],
        out_specs=(pl.BlockSpec((tm, tn), lambda i, j: (i, j)),
                   pl.BlockSpec((tm, LANES), lambda i, j: (i, 0))),
        compiler_params=_cparams(("arbitrary", "arbitrary")),
        name="inproj",
    )(h2d, w_in_t, w_gate)


def _outproj_kernel(ya_ref, yb_ref, yc_ref, x_ref, gate_ref, w_ref, *rest, emit_next):
    y = jnp.dot(ya_ref[...], w_ref[0:D_A, :], preferred_element_type=F32)
    y += jnp.dot(yb_ref[...], w_ref[D_A:D_A + D_B, :], preferred_element_type=F32)
    y += jnp.dot(yc_ref[...], w_ref[D_A + D_B:, :], preferred_element_type=F32)
    x_new = x_ref[...] + gate_ref[...] * y
    if emit_next:
        sc_ref, sh_ref, g_ref, o_ref, h_ref = rest
        h_ref[...] = _norm_mod(x_new, g_ref[...], sc_ref[...], sh_ref[...])
    else:
        (o_ref,) = rest
    o_ref[...] = x_new


def _outproj(ya, yb, yc, x2d, gate, w_out_bf, layer, tm, next_norm=None):
    m = x2d.shape[0]
    tiles_per_group = m // tm // gate.shape[0]
    mod_spec = pl.BlockSpec((None, gate.shape[1], D_MODEL), lambda i: (i // tiles_per_group, 0, 0))
    row = lambda width: pl.BlockSpec((tm, width), lambda i: (i, 0))
    in_specs = [row(D_A), row(D_B), row(D_C), row(D_MODEL), mod_spec,
                pl.BlockSpec((None, D_MODEL, D_MODEL), lambda i: (layer, 0, 0))]
    args = [ya, yb, yc, x2d, gate, w_out_bf]
    out_shape = [jax.ShapeDtypeStruct((m, D_MODEL), F32)]
    out_specs = [row(D_MODEL)]
    if next_norm is not None:
        in_specs += [mod_spec, mod_spec, pl.BlockSpec((1, D_MODEL), lambda i: (0, 0))]
        args += list(next_norm)
        out_shape.append(jax.ShapeDtypeStruct((m, D_MODEL), BF16))
        out_specs.append(row(D_MODEL))
    out = pl.pallas_call(
        functools.partial(_outproj_kernel, emit_next=next_norm is not None),
        out_shape=tuple(out_shape),
        grid=(m // tm,),
        in_specs=in_specs,
        out_specs=tuple(out_specs),
        compiler_params=_cparams(("arbitrary",)),
        name="outproj",
    )(*args)
    return out if next_norm is not None else (out[0], None)


def _head_norm(x, g, left):
    x2 = x * x
    s_left = jnp.sum(jnp.where(left, x2, 0.0), axis=-1, keepdims=True)
    s_right = jnp.sum(jnp.where(left, 0.0, x2), axis=-1, keepdims=True)
    ms = jnp.where(left, s_left, s_right) * (1.0 / HEAD_DIM_A)
    return x * lax.rsqrt(ms + EPS) * g


def _attn_prompt_kernel(q_ref, k_ref, v_ref, z_ref, gq_ref, gk_ref, bias_ref,
                        y_ref, pk_ref, pv_ref,
                        qn_s, kp_s, vp_s, ones_s, o_s, l_s, m_s, *, seq, keep, unroll):
    norm_rows = 512
    lane_n = lax.broadcasted_iota(jnp.int32, (norm_rows, LANES), 1)
    left_n = lane_n < HEAD_DIM_A

    kp_s[0:PAD_A, :] = jnp.zeros((PAD_A, LANES), F32)
    vp_s[0:PAD_A, :] = jnp.zeros((PAD_A, LANES), F32)
    ones_s[...] = jnp.ones(ones_s.shape, BF16)

    def norm_chunk(c, carry):
        r0 = pl.multiple_of(c * norm_rows, norm_rows)
        rows = pl.ds(r0, norm_rows)
        qn_s[rows, :] = _head_norm(q_ref[rows, :], gq_ref[...], left_n) * (HEAD_DIM_A ** -0.5 * LOG2E)
        kp_s[pl.ds(PAD_A + r0, norm_rows), :] = _head_norm(k_ref[rows, :], gk_ref[...], left_n)
        vp_s[pl.ds(PAD_A + r0, norm_rows), :] = v_ref[rows, :]
        return carry
    lax.fori_loop(0, seq // norm_rows, norm_chunk, 0)

    for c in range(keep // LANES):
        src = slice(PAD_A + seq - keep + c * LANES, PAD_A + seq - keep + (c + 1) * LANES)
        pk_ref[:, c * LANES:(c + 1) * LANES] = kp_s[src, :].T
        pv_ref[:, c * LANES:(c + 1) * LANES] = vp_s[src, :].T

    left_b = lax.broadcasted_iota(jnp.int32, (BLK_A, LANES), 1) < HEAD_DIM_A
    left_k = lax.broadcasted_iota(jnp.int32, (2 * BLK_A, LANES), 1) < HEAD_DIM_A

    for p, (win, dil) in enumerate(PATTERNS):
        blocks_per_residue = seq // (dil * BLK_A)

        def block(i, carry, p=p, dil=dil, blocks_per_residue=blocks_per_residue):
            r = i // blocks_per_residue
            n = i % blocks_per_residue
            q_start = r + n * (BLK_A * dil)
            k_start = PAD_A + q_start - BLK_A * dil
            if dil == 1:
                q_rows = pl.ds(pl.multiple_of(q_start, BLK_A), BLK_A)
                k_rows = pl.ds(pl.multiple_of(k_start, BLK_A), 2 * BLK_A)
            else:
                q_rows = pl.ds(q_start, BLK_A, stride=dil)
                k_rows = pl.ds(k_start, 2 * BLK_A, stride=dil)
            first = jnp.where(n == 0, 1, 0)
            q = qn_s[q_rows, :]
            k = kp_s[k_rows, :].astype(BF16)
            v = vp_s[k_rows, :].astype(BF16)
            rhs = jnp.concatenate([v, ones_s[...]], axis=1)
            q2 = jnp.concatenate([jnp.where(left_b, q, 0.0), jnp.where(left_b, 0.0, q)], axis=0)
            s2 = _bdot_nt(q2, k)
            es, ms = [], []
            for h in range(HPAIR):
                s = s2[h * BLK_A:(h + 1) * BLK_A, :] + bias_ref[h, p, first]
                mh = jnp.max(s, axis=-1, keepdims=True)
                es.append(jnp.exp2(s - mh).astype(BF16))
                ms.append(mh)
            res = jnp.dot(jnp.concatenate(es, axis=0), rhs, preferred_element_type=F32)
            o_s[p, q_rows, :] = jnp.where(left_b, res[0:BLK_A, 0:LANES], res[BLK_A:, 0:LANES])
            l_s[p, q_rows, :] = jnp.where(left_b, res[0:BLK_A, LANES:], res[BLK_A:, LANES:])
            m_s[p, q_rows, :] = jnp.where(left_b, ms[0], ms[1])
            return carry
        lax.fori_loop(0, seq // BLK_A, block, 0, unroll=unroll)

    def out_chunk(c, carry):
        rows = pl.ds(pl.multiple_of(c * norm_rows, norm_rows), norm_rows)
        m = [m_s[p, rows, :] for p in range(N_PAT)]
        top = functools.reduce(jnp.maximum, m)
        w = [jnp.exp2(x - top) for x in m]
        num = functools.reduce(lambda a, b: a + b, [w[p] * o_s[p, rows, :] for p in range(N_PAT)])
        den = functools.reduce(lambda a, b: a + b, [w[p] * l_s[p, rows, :] for p in range(N_PAT)])
        y_ref[rows, :] = (num / den * _silu(z_ref[rows, :])).astype(BF16)
        return carry
    lax.fori_loop(0, seq // norm_rows, out_chunk, 0)


def _attn_prompt(proj, gq2, gk2, bias_p):
    b, seq, _ = proj.shape
    keep = min(WIN_MAX, seq)
    assert seq % (BLK_A * PATTERNS[-1][1]) == 0
    col = lambda base: pl.BlockSpec((None, seq, LANES), lambda bi, hp: (bi, 0, base + hp))
    vec = pl.BlockSpec((1, LANES), lambda bi, hp: (0, 0))
    return pl.pallas_call(
        functools.partial(_attn_prompt_kernel, seq=seq, keep=keep, unroll=seq // BLK_A),
        out_shape=(jax.ShapeDtypeStruct((b, seq, D_A), BF16),
                   jax.ShapeDtypeStruct((b, D_A, keep), F32),
                   jax.ShapeDtypeStruct((b, D_A, keep), F32)),
        grid=(b, H_A // HPAIR),
        in_specs=[col(COL_Q), col(COL_K), col(COL_V), col(COL_Z), vec, vec,
                  pl.BlockSpec((HPAIR, N_PAT, 2, BLK_A, 2 * BLK_A), lambda bi, hp: (hp, 0, 0, 0, 0))],
        out_specs=(pl.BlockSpec((None, seq, LANES), lambda bi, hp: (bi, 0, hp)),
                   pl.BlockSpec((None, LANES, keep), lambda bi, hp: (bi, hp, 0)),
                   pl.BlockSpec((None, LANES, keep), lambda bi, hp: (bi, hp, 0))),
        scratch_shapes=[pltpu.VMEM((seq, LANES), F32),
                        pltpu.VMEM((PAD_A + seq, LANES), F32),
                        pltpu.VMEM((PAD_A + seq, LANES), F32),
                        pltpu.VMEM((2 * BLK_A, LANES), BF16),
                        pltpu.VMEM((N_PAT, seq, LANES), F32),
                        pltpu.VMEM((N_PAT, seq, LANES), F32),
                        pltpu.VMEM((N_PAT, seq, LANES), F32)],
        compiler_params=_cparams(("arbitrary", "arbitrary")),
        name="attn_prompt",
    )(proj, proj, proj, proj, gq2, gk2, bias_p)


def _attn_sample_kernel(q_ref, k_ref, v_ref, z_ref, ck_ref, cv_ref, gq_ref, gk_ref, bias_ref,
                        y_ref, nk_ref, nv_ref, *, n_past, t_new):
    q_all = q_ref[...]
    k_all = k_ref[...]
    v_all = v_ref[...]
    z_all = z_ref[...]
    nv_ref[...] = v_all
    pad_rows = jnp.zeros((LANES - t_new, HEAD_DIM_A), F32)

    def norm(x, g):
        ms = jnp.sum(x * x, axis=-1, keepdims=True) * (1.0 / HEAD_DIM_A)
        return x * lax.rsqrt(ms + EPS) * g

    for h in range(H_A):
        cols = slice(h * HEAD_DIM_A, (h + 1) * HEAD_DIM_A)
        qn = (norm(q_all[:, cols], gq_ref[:, cols]) * (HEAD_DIM_A ** -0.5)).astype(BF16)
        kn = norm(k_all[:, cols], gk_ref[:, cols])
        nk_ref[:, cols] = kn
        k_new = jnp.concatenate([kn, pad_rows], axis=0).astype(BF16)
        v_new = jnp.concatenate([v_all[:, cols], pad_rows], axis=0).astype(BF16)
        k_t = ck_ref[h].astype(BF16)
        v_t = cv_ref[h].astype(BF16)
        s_c = jnp.dot(qn, k_t, preferred_element_type=F32)
        s_n = _bdot_nt(qn, k_new)
        sc = [s_c + bias_ref[h, p, :, 0:n_past] for p in range(N_PAT)]
        sn = [s_n + bias_ref[h, p, :, n_past:] for p in range(N_PAT)]
        m = functools.reduce(jnp.maximum, [jnp.max(x, axis=-1, keepdims=True) for x in sc + sn])
        e_c = functools.reduce(lambda a, b: a + b, [jnp.exp(x - m) for x in sc])
        e_n = functools.reduce(lambda a, b: a + b, [jnp.exp(x - m) for x in sn])
        den = jnp.sum(e_c, axis=-1, keepdims=True) + jnp.sum(e_n, axis=-1, keepdims=True)
        o = _bdot_nt(e_c, v_t) + jnp.dot(e_n.astype(BF16), v_new, preferred_element_type=F32)
        y_ref[:, cols] = (o / den * _silu(z_all[:, cols])).astype(BF16)


def _attn_sample(proj_s, cache_kt, cache_vt, gq, gk, bias_s, layer):
    bd, t_new, _ = proj_s.shape
    n_past = cache_kt.shape[-1]
    col = lambda base: pl.BlockSpec((None, t_new, D_A), lambda bi: (bi, 0, base))
    vec = pl.BlockSpec((1, D_A), lambda bi: (0, 0))
    cache = pl.BlockSpec((None, None, H_A, HEAD_DIM_A, n_past), lambda bi: (layer, bi, 0, 0, 0))
    out = pl.BlockSpec((None, t_new, D_A), lambda bi: (bi, 0, 0))
    return pl.pallas_call(
        functools.partial(_attn_sample_kernel, n_past=n_past, t_new=t_new),
        out_shape=(jax.ShapeDtypeStruct((bd, t_new, D_A), BF16),
                   jax.ShapeDtypeStruct((bd, t_new, D_A), F32),
                   jax.ShapeDtypeStruct((bd, t_new, D_A), F32)),
        grid=(bd,),
        in_specs=[col(0), col(1), col(2), col(3), cache, cache, vec, vec,
                  pl.BlockSpec((H_A, N_PAT, t_new, n_past + LANES), lambda bi: (0, 0, 0, 0))],
        out_specs=(out, out, out),
        compiler_params=_cparams(("arbitrary",)),
        name="attn_sample",
    )(proj_s, proj_s, proj_s, proj_s, cache_kt, cache_vt, gq, gk, bias_s)


def _sgu_prompt_kernel(u_ref, v_ref, z_ref, g_ref, w_ref, bt_ref, y_ref, *, chunks):
    row = lax.broadcasted_iota(jnp.int32, (CHUNK_B, CHUNK_B), 0)
    colm = lax.broadcasted_iota(jnp.int32, (CHUNK_B, CHUNK_B), 1)
    tril = row >= colm
    ws = [jnp.where(tril, w_ref[g], 0.0).astype(BF16) for g in range(G_B)]
    for c in range(chunks):
        rows = slice(c * CHUNK_B, (c + 1) * CHUNK_B)
        v = v_ref[rows, :]
        ms = jnp.sum(v * v, axis=-1, keepdims=True) * (1.0 / D_B)
        vn = (v * lax.rsqrt(ms + EPS) * g_ref[...]).astype(BF16)
        for g in range(G_B):
            cols = slice(g * C_B, (g + 1) * C_B)
            mix = jnp.dot(ws[g], vn[:, cols], preferred_element_type=F32) + bt_ref[:, g:g + 1]
            y_ref[rows, cols] = (u_ref[rows, cols] * mix * _silu(z_ref[rows, cols])).astype(BF16)


def _sgu_prompt(proj, sgu_g, sgu_w, sgu_bt):
    b, seq, _ = proj.shape
    ts = 512
    col = lambda base: pl.BlockSpec((None, ts, D_B), lambda bi, i: (bi, i, base))
    return pl.pallas_call(
        functools.partial(_sgu_prompt_kernel, chunks=ts // CHUNK_B),
        out_shape=jax.ShapeDtypeStruct((b, seq, D_B), BF16),
        grid=(b, seq // ts),
        in_specs=[col(COL_UB), col(COL_VB), col(COL_ZB),
                  pl.BlockSpec((1, D_B), lambda bi, i: (0, 0)),
                  pl.BlockSpec((G_B, CHUNK_B, CHUNK_B), lambda bi, i: (0, 0, 0)),
                  pl.BlockSpec((CHUNK_B, G_B), lambda bi, i: (0, 0))],
        out_specs=pl.BlockSpec((None, ts, D_B), lambda bi, i: (bi, i, 0)),
        compiler_params=_cparams(("arbitrary", "arbitrary")),
        name="sgu_prompt",
    )(proj, proj, proj, sgu_g, sgu_w, sgu_bt)


def _sgu_sample_kernel(u_ref, v_ref, z_ref, g_ref, wexp_ref, bexp_ref, y_ref, vn_ref, *, t_new):
    v = v_ref[...]
    ms = jnp.sum(v * v, axis=-1, keepdims=True) * (1.0 / D_B)
    vn = v * lax.rsqrt(ms + EPS) * g_ref[...]
    vn_ref[...] = vn
    mix = bexp_ref[...]
    for s in range(t_new):
        mix = mix + wexp_ref[s] * vn[s:s + 1, :]
    y_ref[...] = (u_ref[...] * mix * _silu(z_ref[...])).astype(BF16)


def _sgu_sample(proj_s, sgu_g, wexp, bexp):
    bd, t_new, _ = proj_s.shape
    col = lambda base: pl.BlockSpec((None, t_new, D_B), lambda bi: (bi, 0, base))
    return pl.pallas_call(
        functools.partial(_sgu_sample_kernel, t_new=t_new),
        out_shape=(jax.ShapeDtypeStruct((bd, t_new, D_B), BF16),
                   jax.ShapeDtypeStruct((bd, t_new, D_B), F32)),
        grid=(bd,),
        in_specs=[col(COL_UB), col(COL_VB), col(COL_ZB),
                  pl.BlockSpec((1, D_B), lambda bi: (0, 0)),
                  pl.BlockSpec((t_new, t_new, D_B), lambda bi: (0, 0, 0)),
                  pl.BlockSpec((t_new, D_B), lambda bi: (0, 0))],
        out_specs=(pl.BlockSpec((None, t_new, D_B), lambda bi: (bi, 0, 0)),
                   pl.BlockSpec((None, t_new, D_B), lambda bi: (bi, 0, 0))),
        compiler_params=_cparams(("arbitrary",)),
        name="sgu_sample",
    )(proj_s, proj_s, proj_s, sgu_g, wexp, bexp)


def _log_sigmoid(x):
    return jnp.minimum(x, 0.0) - jnp.log1p(jnp.exp(-jnp.abs(x)))


MLSTM_SEQS_PER_STEP = 2
MLSTM_ROW0 = SUBLANES
MLSTM_CARRY = CONV_W - 1


def _mlstm_init(cbuf_ref, c0_ref, n0_ref, m0_ref, xp_s, vp_s, gp_s, caug_s, m_s):
    xp_s[...] = jnp.zeros(xp_s.shape, F32)
    vp_s[...] = jnp.zeros(vp_s.shape, F32)
    gp_s[...] = jnp.zeros(gp_s.shape, F32)
    xp_s[MLSTM_ROW0 - MLSTM_CARRY:MLSTM_ROW0, :] = cbuf_ref[...]
    n0 = n0_ref[...]
    n0_cols = jnp.concatenate([n0, jnp.zeros((DH_C - n0.shape[0], DH_C), F32)], axis=0).T
    for h in range(H_C):
        caug_s[h, :, 0:DH_C] = c0_ref[h]
        caug_s[h, :, DH_C:] = jnp.broadcast_to(n0_cols[:, h:h + 1], (DH_C, DH_C))
    m_s[...] = m0_ref[...]


def _mlstm_final(new_carry, conv_out_ref, c_out_ref, n_out_ref, m_out_ref, caug_s, m_s):
    conv_out_ref[...] = new_carry
    lane = lax.broadcasted_iota(jnp.int32, (DH_C, LANES), 1)
    n_cols = jnp.zeros((DH_C, LANES), F32)
    for h in range(H_C):
        c_out_ref[h] = caug_s[h, :, 0:DH_C]
        n_cols = jnp.where(lane == h, caug_s[h, :, DH_C:], n_cols)
    n_out_ref[...] = n_cols.T[0:SUBLANES, :]
    m_out_ref[...] = m_s[...]


def _mlstm_chunk(qk_ref, v_ref, o_ref, z_ref, gt_ref, cw_ref, cb_ref, gb_ref, hn_ref, y_ref,
                 xp_s, vp_s, gp_s, caug_s, m_s, *, t_rows):
    L = CHUNK_C
    base = MLSTM_ROW0
    carry_rows = MLSTM_CARRY
    row_l = lax.broadcasted_iota(jnp.int32, (L, LANES), 0)
    lane_l = lax.broadcasted_iota(jnp.int32, (L, LANES), 1)
    zero_rows = jnp.zeros((L - SUBLANES, LANES), F32)

    xp_s[base:base + t_rows, :] = qk_ref[...]
    vp_s[0:t_rows, :] = v_ref[...]
    gp_s[0:t_rows, :] = gt_ref[...]

    acc = cb_ref[...] + cw_ref[CONV_W - 1:CONV_W, :] * xp_s[base:base + L, :]
    for j in range(CONV_W - 1):
        off = base - carry_rows + j
        acc = acc + cw_ref[j:j + 1, :] * xp_s[off:off + L, :]
    qk = _silu(acc)
    new_carry = xp_s[base + t_rows - carry_rows:base + t_rows, :]
    xp_s[base - carry_rows:base, :] = new_carry

    gates_t = (gp_s[...] + gb_ref[...]).T
    lane_t = lax.broadcasted_iota(jnp.int32, (SUBLANES, L), 1)
    real_t = lane_t < t_rows
    i_t = jnp.where(real_t, gates_t[0:SUBLANES, :], NEG)
    logf_t = jnp.where(real_t, _log_sigmoid(gates_t[SUBLANES:2 * SUBLANES, :]), 0.0)
    upper = (row_l <= lane_l).astype(BF16)
    f1 = logf_t.astype(BF16)
    r1 = logf_t - f1.astype(F32)
    f2 = r1.astype(BF16)
    f3 = (r1 - f2.astype(F32)).astype(BF16)
    b_t = (jnp.dot(f1, upper, preferred_element_type=F32) + jnp.dot(f2, upper, preferred_element_type=F32)
           + jnp.dot(f3, upper, preferred_element_type=F32))
    m_prev = m_s[...]
    c_t = i_t - b_t
    pm = c_t
    shift = 1
    while shift < L:
        pm = jnp.maximum(pm, jnp.where(lane_t >= shift, pltpu.roll(pm, shift, axis=1), NEG))
        shift *= 2
    d_t = -jnp.maximum(m_prev, pm)
    wi_t = jnp.exp(m_prev + d_t)
    em_t = jnp.exp(-(b_t - d_t))
    bl = jnp.broadcast_to(b_t[:, L - 1:L], (SUBLANES, L))
    g_t = bl - b_t + i_t
    m_new = jnp.maximum(bl + m_prev, jnp.broadcast_to(jnp.max(g_t, axis=1, keepdims=True), (SUBLANES, L)))
    ws_t = jnp.exp(g_t - m_new)
    wc = jnp.exp(bl + m_prev - m_new)
    m_s[...] = m_new
    stats = jnp.concatenate([d_t, wi_t, em_t, ws_t, zero_rows[0:L - 4 * SUBLANES]], axis=0).T

    causal = row_l >= lane_l
    ones_blk = jnp.ones((L, DH_C), BF16)
    v_all = vp_s[...]
    o_all = o_ref[...]
    z_all = z_ref[...]

    for h in range(H_C):
        cols = slice(h * DH_C, (h + 1) * DH_C)
        q = qk[:, cols].astype(BF16)
        k = qk[:, D_C + h * DH_C:D_C + (h + 1) * DH_C] * (DH_C ** -0.5)
        v_aug = jnp.concatenate([v_all[:, cols].astype(BF16), ones_blk], axis=1)
        d_col = stats[:, h:h + 1]
        wi_col = stats[:, SUBLANES + h:SUBLANES + h + 1]
        em_col = stats[:, 2 * SUBLANES + h:2 * SUBLANES + h + 1]
        ws_col = stats[:, 3 * SUBLANES + h:3 * SUBLANES + h + 1]
        c_prev = caug_s[h]

        a = jnp.exp(jnp.where(causal, d_col + c_t[h:h + 1, :], NEG)) * _bdot_nt(q, k)
        r1_ = jnp.dot(a.astype(BF16), v_aug, preferred_element_type=F32)
        r2_ = jnp.dot(q, c_prev.astype(BF16), preferred_element_type=F32)
        num = r1_[:, 0:DH_C] + wi_col * r2_[:, 0:DH_C]
        den = r1_[:, DH_C:] + wi_col * r2_[:, DH_C:]
        hh = num / jnp.maximum(jnp.abs(den), em_col)

        wc_row = jnp.concatenate([wc[h:h + 1, :], wc[h:h + 1, :]], axis=1)
        caug_s[h] = wc_row * c_prev + _bdot_tn(k * ws_col, v_aug)

        ms = jnp.sum(hh * hh, axis=-1, keepdims=True) * (1.0 / DH_C)
        hn = hh * lax.rsqrt(ms + EPS) * hn_ref[:, cols]
        z = z_all[:, cols]
        gate = z / ((1.0 + jnp.exp(-o_all[:, cols])) * (1.0 + jnp.exp(-z)))
        y_ref[:, cols] = (hn[0:t_rows, :] * gate).astype(BF16)
    return new_carry


def _mlstm_kernel(*refs, t_rows, nb):
    (qk_b, v_b, o_b, z_b, gt_b, cbuf_b, c0_b, n0_b, m0_b, cw_ref, cb_ref, gb_ref, hn_ref,
     y_b, conv_out_b, c_out_b, n_out_b, m_out_b, xp_b, vp_b, gp_b, caug_b, m_b) = refs
    ci = pl.program_id(1)
    seqs = range(nb)

    @pl.when(ci == 0)
    def _():
        for bi in seqs:
            _mlstm_init(cbuf_b.at[bi], c0_b.at[bi], n0_b.at[bi], m0_b.at[bi],
                        xp_b.at[bi], vp_b.at[bi], gp_b.at[bi], caug_b.at[bi], m_b.at[bi])

    carries = [_mlstm_chunk(qk_b.at[bi], v_b.at[bi], o_b.at[bi], z_b.at[bi], gt_b.at[bi],
                            cw_ref, cb_ref, gb_ref, hn_ref, y_b.at[bi],
                            xp_b.at[bi], vp_b.at[bi], gp_b.at[bi], caug_b.at[bi], m_b.at[bi], t_rows=t_rows)
               for bi in seqs]

    @pl.when(ci == pl.num_programs(1) - 1)
    def _():
        for bi in seqs:
            _mlstm_final(carries[bi], conv_out_b.at[bi], c_out_b.at[bi], n_out_b.at[bi], m_out_b.at[bi],
                         caug_b.at[bi], m_b.at[bi])


def _mlstm(proj, gates, conv_buf, c0, n0, m0, conv_w, conv_b, gate_bias, hn_g, t_rows):
    b, seq, _ = proj.shape
    nchunks = seq // t_rows
    hp = SUBLANES
    nb = MLSTM_SEQS_PER_STEP
    assert b % nb == 0
    col = lambda base, width: pl.BlockSpec((nb, t_rows, width), lambda bi, ci: (bi, ci, base))
    const2 = lambda shape: pl.BlockSpec(shape, lambda bi, ci: (0, 0))
    per_b3 = lambda shape: pl.BlockSpec((nb,) + shape, lambda bi, ci: (bi, 0, 0))
    per_b4 = lambda shape: pl.BlockSpec((nb,) + shape, lambda bi, ci: (bi, 0, 0, 0))
    return pl.pallas_call(
        functools.partial(_mlstm_kernel, t_rows=t_rows, nb=nb),
        out_shape=(jax.ShapeDtypeStruct((b, seq, D_C), BF16),
                   jax.ShapeDtypeStruct((b, CONV_W - 1, 2 * D_C), F32),
                   jax.ShapeDtypeStruct((b, H_C, DH_C, DH_C), F32),
                   jax.ShapeDtypeStruct((b, hp, DH_C), F32),
                   jax.ShapeDtypeStruct((b, hp, LANES), F32)),
        grid=(b // nb, nchunks),
        in_specs=[col(COL_QK, 2 * D_C), col(COL_VC, D_C), col(COL_OC, D_C), col(COL_ZC, D_C),
                  pl.BlockSpec((nb, t_rows, LANES), lambda bi, ci: (bi, ci, 0)),
                  per_b3((CONV_W - 1, 2 * D_C)), per_b4((H_C, DH_C, DH_C)), per_b3((hp, DH_C)),
                  per_b3((hp, LANES)),
                  const2((CONV_W, 2 * D_C)), const2((1, 2 * D_C)), const2((1, LANES)), const2((1, D_C))],
        out_specs=(pl.BlockSpec((nb, t_rows, D_C), lambda bi, ci: (bi, ci, 0)),
                   per_b3((CONV_W - 1, 2 * D_C)), per_b4((H_C, DH_C, DH_C)), per_b3((hp, DH_C)),
                   per_b3((hp, LANES))),
        scratch_shapes=[pltpu.VMEM((nb, MLSTM_ROW0 + CHUNK_C, 2 * D_C), F32),
                        pltpu.VMEM((nb, CHUNK_C, D_C), F32),
                        pltpu.VMEM((nb, CHUNK_C, LANES), F32),
                        pltpu.VMEM((nb, H_C, DH_C, 2 * DH_C), F32),
                        pltpu.VMEM((nb, hp, LANES), F32)],
        compiler_params=_cparams(("arbitrary", "arbitrary")),
        name="mlstm",
    )(proj, proj, proj, proj, gates, conv_buf, c0, n0, m0, conv_w, conv_b, gate_bias, hn_g)


def _gate_lanes(i_part, f_part):
    lead = i_part.shape[:-1]
    gap = jnp.zeros(lead + (SUBLANES - H_C,), i_part.dtype)
    tail = jnp.zeros(lead + (LANES - 2 * SUBLANES,), i_part.dtype)
    return jnp.concatenate([i_part, gap, f_part, gap, tail], axis=-1)


def _pad_heads(a):
    pad = [(0, 0)] * a.ndim
    pad[1] = (0, SUBLANES - a.shape[1])
    return jnp.pad(a, pad)


def _mixers(proj, gates, lw, attn_fn, sgu_fn, conv_buf, c0, n0, m0, t_rows):
    ya, nk, nv = attn_fn(proj)
    sgu_out = sgu_fn(proj)
    m0b = jnp.broadcast_to(_pad_heads(m0)[:, :, None], (m0.shape[0], SUBLANES, LANES))
    yc, nconv, c_new, n_new, m_new = _mlstm(proj, gates, conv_buf, c0, _pad_heads(n0), m0b,
                                            lw["conv_w"], lw["conv_b"], lw["gate_bias"], lw["hn_g"], t_rows)
    return ya, sgu_out, yc, nk, nv, nconv, c_new, n_new[:, :H_C], m_new[:, :H_C, 0]


def kernel(x_prompt, x_sample, c_prompt, c_sample, cache_k_win, cache_v_win, state_conv, state_C, state_n, state_m, rel_bias, norm_g, ada_w, ada_b, w_in, qn_g, kn_g, sgu_g, sgu_w, sgu_b, conv_w, conv_b, f_bias, i_bias, hn_g, w_out):
    depth = w_in.shape[0]
    bp, seq, _ = x_prompt.shape
    bd, t_new, _ = x_sample.shape
    n_past = cache_k_win.shape[2]
    assert t_new == SUBLANES and n_past % LANES == 0 and seq % 1024 == 0

    w_in_t = _unpack_w_in(w_in)
    w_gate = _gate_lanes(w_in[:, :, D_MAIN:D_MAIN + H_C], w_in[:, :, D_MAIN + H_C:])
    w_out_bf = w_out.astype(BF16)

    rel_t = rel_bias.T
    bias_p = _expand_bias(rel_t, jnp.asarray(_prompt_bucket_table()), 64, LOG2E)
    bias_p = bias_p.reshape(H_A, N_PAT, 2, BLK_A, 2 * BLK_A)
    bias_s = _expand_bias(rel_t, jnp.asarray(_sample_bucket_table(n_past, t_new)), SUBLANES, 1.0)
    bias_s = bias_s.reshape(H_A, N_PAT, t_new, n_past + LANES)

    n_c = bp + bd
    c_all = jnp.pad(jnp.concatenate([c_prompt, c_sample], axis=0), ((0, 2 * SUBLANES - n_c), (0, 0)))
    mod = _ada_mod(c_all, ada_w, ada_b)

    cache_kt = jnp.transpose(cache_k_win, (0, 1, 3, 4, 2))
    cache_vt = jnp.transpose(cache_v_win, (0, 1, 3, 4, 2))
    gq_all = [jnp.tile(qn_g[l], H_A)[None] for l in range(depth)]
    gk_all = [jnp.tile(kn_g[l], H_A)[None] for l in range(depth)]
    zeros_conv = jnp.zeros((bp, CONV_W - 1, 2 * D_C), F32)
    zeros_c = jnp.zeros((bp, H_C, DH_C, DH_C), F32)
    zeros_n = jnp.zeros((bp, H_C, DH_C), F32)
    zeros_m = jnp.zeros((bp, H_C), F32)
    tril8 = jnp.tril(jnp.ones((t_new, t_new), F32))

    xp = x_prompt.reshape(bp * seq, D_MODEL)
    xs = x_sample.reshape(bd * t_new, D_MODEL)
    outs = [[] for _ in range(13)]
    p_mods, s_mods = [], []
    for l in range(depth):
        shift, scale, gate = (mod[l, :, i * D_MODEL:(i + 1) * D_MODEL] for i in range(3))
        p_mods.append([a[:bp, None, :] for a in (scale, shift, gate)])
        s_mods.append([jnp.repeat(a[bp:n_c], t_new, axis=0)[None] for a in (scale, shift, gate)])
    hp = _norm(xp, p_mods[0][0], p_mods[0][1], norm_g[0][None], 512)
    hs = _norm(xs, s_mods[0][0], s_mods[0][1], norm_g[0][None], bd * t_new)
    for l in range(depth):
        p_mod, s_mod = p_mods[l], s_mods[l]
        last = l == depth - 1
        p_next = None if last else (p_mods[l + 1][0], p_mods[l + 1][1], norm_g[l + 1][None])
        s_next = None if last else (s_mods[l + 1][0], s_mods[l + 1][1], norm_g[l + 1][None])
        lw = {
            "conv_w": conv_w[l], "conv_b": conv_b[l][None],
            "gate_bias": _gate_lanes(i_bias[l], f_bias[l])[None],
            "hn_g": hn_g[l][None],
        }
        gq2 = jnp.tile(qn_g[l], HPAIR)[None]
        gk2 = jnp.tile(kn_g[l], HPAIR)[None]
        sg = sgu_g[l][None]

        proj, gates = _inproj(hp, w_in_t, w_gate, l, 2048)
        proj = proj.reshape(bp, seq, D_MAIN)
        gates = gates.reshape(bp, seq, LANES)
        ya, yb, yc, nk, nv, ncv, nc_, nn_, nm = _mixers(
            proj, gates, lw,
            lambda pr: _attn_prompt(pr, gq2, gk2, bias_p),
            lambda pr: _sgu_prompt(pr, sg, sgu_w[l], sgu_b[l].T),
            zeros_conv, zeros_c, zeros_n, zeros_m, CHUNK_C)
        xp, hp = _outproj(ya.reshape(bp * seq, D_A), yb.reshape(bp * seq, D_B), yc.reshape(bp * seq, D_C),
                          xp, p_mod[2], w_out_bf, l, 512, p_next)
        keep = nk.shape[2]
        for i, a in enumerate((nk.reshape(bp, H_A, HEAD_DIM_A, keep), nv.reshape(bp, H_A, HEAD_DIM_A, keep),
                               ncv, nc_, nn_, nm)):
            outs[i].append(a)

        proj_s, gates_s = _inproj(hs, w_in_t, w_gate, l, bd * t_new)
        proj_s = proj_s.reshape(bd, t_new, D_MAIN)
        gates_s = gates_s.reshape(bd, t_new, LANES)
        w8 = sgu_w[l][:, :t_new, :t_new] * tril8
        wexp = jnp.repeat(jnp.transpose(w8, (2, 1, 0)), C_B, axis=2)
        bexp = jnp.repeat(sgu_b[l][:, :t_new].T, C_B, axis=1)
        ya, sgu_out, yc, nk, nv, ncv, nc_, nn_, nm = _mixers(
            proj_s, gates_s, lw,
            lambda pr: _attn_sample(pr, cache_kt, cache_vt, gq_all[l], gk_all[l], bias_s, l),
            lambda pr: _sgu_sample(pr, sg, wexp, bexp),
            state_conv[l], state_C[l], state_n[l], state_m[l], t_new)
        yb, vn = sgu_out
        xs, hs = _outproj(ya.reshape(bd * t_new, D_A), yb.reshape(bd * t_new, D_B), yc.reshape(bd * t_new, D_C),
                          xs, s_mod[2], w_out_bf, l, bd * t_new, s_next)
        for i, a in enumerate((nk.reshape(bd, t_new, H_A, HEAD_DIM_A), nv.reshape(bd, t_new, H_A, HEAD_DIM_A),
                               vn, ncv, nc_, nn_, nm)):
            outs[6 + i].append(a)

    stacked = [jnp.stack(o) for o in outs]
    for i in (0, 1):
        stacked[i] = jnp.transpose(stacked[i], (0, 1, 4, 2, 3))
    return (xp.reshape(bp, seq, D_MODEL), xs.reshape(bd, t_new, D_MODEL), *stacked)
```

```python
import functools
import math

import numpy as np
import jax
import jax.numpy as jnp
from jax import lax
from jax.experimental import pallas as pl
from jax.experimental.pallas import tpu as pltpu

F32 = jnp.float32
BF16 = jnp.bfloat16

D_MODEL = 2048
HEAD_DIM_A = 64
D_A = 768
H_A = 12
D_B = 512
G_B = 4
C_B = 128
CHUNK_B = 128
D_C = 768
DH_C = 128
H_C = 6
CHUNK_C = 128
CONV_W = 4
PATTERNS = ((128, 1), (512, 4), (2048, 16))
N_PAT = len(PATTERNS)
WIN_MAX = 2048
BLK_A = 128
N_BUCKETS = 32
MAX_DIST = 2048
EPS = 1e-6
D_MAIN = 4 * D_A + 3 * D_B + 2 * D_C + 3 * D_C
D_IN = D_MAIN + 2 * H_C

LANES = 128
SUBLANES = 8
VMEM_LIMIT = 56 * 1024 * 1024

NEG = -1e30
LOG2E = 1.4426950408889634
PAD_A = BLK_A * PATTERNS[-1][1]
HPAIR = LANES // HEAD_DIM_A

COL_Q, COL_K, COL_V, COL_Z = 0, D_A // LANES, 2 * D_A // LANES, 3 * D_A // LANES
COL_UB, COL_VB, COL_ZB = 4 * D_A // D_B, 4 * D_A // D_B + 1, 4 * D_A // D_B + 2
COL_QK = (4 * D_A + 3 * D_B) // (2 * D_C)
COL_VC = (4 * D_A + 3 * D_B + 2 * D_C) // D_C
COL_OC, COL_ZC = COL_VC + 1, COL_VC + 2


def _cparams(sem):
    return pltpu.CompilerParams(dimension_semantics=sem, vmem_limit_bytes=VMEM_LIMIT)


def _silu(x):
    return x * jax.nn.sigmoid(x)


def _bdot(a, b):
    return jnp.dot(a.astype(BF16), b.astype(BF16), preferred_element_type=F32)


def _bdot_nt(a, b):
    return lax.dot_general(a.astype(BF16), b.astype(BF16), (((1,), (1,)), ((), ())),
                           preferred_element_type=F32)


def _bdot_tn(a, b):
    return lax.dot_general(a.astype(BF16), b.astype(BF16), (((0,), (0,)), ((), ())),
                           preferred_element_type=F32)


def _bucket_np(dist):
    max_exact = N_BUCKETS // 2
    df = np.maximum(dist, 1).astype(np.float32)
    large = max_exact + (np.log(df / np.float32(max_exact)) / np.float32(math.log(MAX_DIST / max_exact))
                         * np.float32(N_BUCKETS - max_exact)).astype(np.int32)
    return np.where(dist < max_exact, dist, np.minimum(large, N_BUCKETS - 1)).astype(np.int32)


def _prompt_bucket_table():
    qi = np.arange(BLK_A)[:, None]
    ki = np.arange(2 * BLK_A)[None, :]
    j = qi + BLK_A - ki
    out = []
    for win, dil in PATTERNS:
        n_back = win // dil
        band = (j >= 0) & (j <= n_back)
        b = _bucket_np(np.clip(j, 0, n_back) * dil)
        out.append(np.where(band, b, -1))
        out.append(np.where(band & (ki >= BLK_A), b, -1))
    return np.stack(out).reshape(N_PAT * 2 * BLK_A, 2 * BLK_A).astype(np.int32)


def _sample_bucket_table(n_past, t_new):
    c = np.arange(n_past + LANES)[None, :]
    t = np.arange(t_new)[:, None]
    delta = n_past + t - c
    out = []
    for win, dil in PATTERNS:
        valid = (c < n_past + t_new) & (delta >= 0) & (delta % dil == 0) & (delta // dil <= win // dil)
        out.append(np.where(valid, _bucket_np(np.maximum(delta, 0)), -1))
    return np.stack(out).reshape(N_PAT * t_new, n_past + LANES).astype(np.int32)


def _bias_kernel(rb_ref, idx_ref, out_ref, *, scale, row_chunk):
    h = pl.program_id(0)

    def chunk(c, carry):
        rows = pl.ds(pl.multiple_of(c * row_chunk, row_chunk), row_chunk)
        idx = idx_ref[rows, :]
        out = jnp.full(idx.shape, NEG, F32)
        for b in range(N_BUCKETS):
            out = jnp.where(idx == b, rb_ref[h, b] * scale, out)
        out_ref[rows, :] = out
        return carry
    lax.fori_loop(0, idx_ref.shape[0] // row_chunk, chunk, 0)


def _expand_bias(rel_bias_t, idx, row_chunk, scale):
    rows, cols = idx.shape
    return pl.pallas_call(
        functools.partial(_bias_kernel, scale=scale, row_chunk=row_chunk),
        out_shape=jax.ShapeDtypeStruct((H_A, rows, cols), F32),
        grid=(H_A,),
        in_specs=[pl.BlockSpec(memory_space=pltpu.SMEM),
                  pl.BlockSpec((rows, cols), lambda h: (0, 0))],
        out_specs=pl.BlockSpec((None, rows, cols), lambda h: (h, 0, 0)),
        compiler_params=_cparams(("arbitrary",)),
        name="bias_expand",
    )(rel_bias_t, idx)


def _ada_kernel(c_ref, w_ref, b_ref, o_ref):
    c = c_ref[...]
    a = _silu(c)
    w = w_ref[...]
    a_hi = a.astype(BF16)
    a_lo = (a - a_hi.astype(F32)).astype(BF16)
    w_hi = w.astype(BF16)
    w_lo = (w - w_hi.astype(F32)).astype(BF16)
    acc = jnp.dot(a_hi, w_hi, preferred_element_type=F32)
    acc += jnp.dot(a_hi, w_lo, preferred_element_type=F32)
    acc += jnp.dot(a_lo, w_hi, preferred_element_type=F32)
    o_ref[...] = acc + b_ref[...]


def _ada_mod(c_all, ada_w, ada_b):
    depth = ada_w.shape[0]
    rows = c_all.shape[0]
    tn = 768
    n = 3 * D_MODEL
    return pl.pallas_call(
        _ada_kernel,
        out_shape=jax.ShapeDtypeStruct((depth, rows, n), F32),
        grid=(depth, n // tn),
        in_specs=[pl.BlockSpec((rows, D_MODEL), lambda l, j: (0, 0)),
                  pl.BlockSpec((None, D_MODEL, tn), lambda l, j: (l, 0, j)),
                  pl.BlockSpec((None, 1, tn), lambda l, j: (l, 0, j))],
        out_specs=pl.BlockSpec((None, rows, tn), lambda l, j: (l, 0, j)),
        compiler_params=_cparams(("arbitrary", "arbitrary")),
        name="ada_mod",
    )(c_all, ada_w, ada_b.reshape(depth, 1, n))


def _norm_mod(x, g, sc, sh):
    ms = jnp.sum(x * x, axis=-1, keepdims=True) * (1.0 / D_MODEL)
    return ((x * lax.rsqrt(ms + EPS) * g) * (1.0 + sc) + sh).astype(BF16)


def _norm_kernel(x_ref, sc_ref, sh_ref, g_ref, h_ref):
    h_ref[...] = _norm_mod(x_ref[...], g_ref[...], sc_ref[...], sh_ref[...])


def _norm(x2d, scale, shift, norm_g, tm):
    m = x2d.shape[0]
    tiles_per_group = m // tm // scale.shape[0]
    mod_spec = pl.BlockSpec((None, scale.shape[1], D_MODEL), lambda i: (i // tiles_per_group, 0, 0))
    return pl.pallas_call(
        _norm_kernel,
        out_shape=jax.ShapeDtypeStruct((m, D_MODEL), BF16),
        grid=(m // tm,),
        in_specs=[pl.BlockSpec((tm, D_MODEL), lambda i: (i, 0)), mod_spec, mod_spec,
                  pl.BlockSpec((1, D_MODEL), lambda i: (0, 0))],
        out_specs=pl.BlockSpec((tm, D_MODEL), lambda i: (i, 0)),
        compiler_params=_cparams(("arbitrary",)),
        name="norm",
    )(x2d, scale, shift, norm_g)


def _inproj_kernel(h_ref, w_ref, wg_ref, proj_ref, gates_ref):
    @pl.when(pl.program_id(1) == 0)
    def _():
        gates_ref[...] = jnp.dot(h_ref[...], wg_ref[...].astype(BF16), preferred_element_type=F32)

    proj_ref[...] = lax.dot_general(h_ref[...], w_ref[...], (((1,), (1,)), ((), ())),
                                    preferred_element_type=F32)


UNPACK_PITCH = 72


def _unpack_kernel(x_hbm, o_ref, buf_a, buf_b, sem, *, depth, k_tiles, tn):
    i = pl.program_id(0)
    rows_per_col = k_tiles * depth

    def copy(tile, n, buf, slot):
        return pltpu.make_async_copy(x_hbm.at[tile * tn + n], buf.at[pl.ds(n * UNPACK_PITCH, rows_per_col)],
                                     sem.at[slot])

    def start_all(tile, buf, slot):
        def body(n, carry):
            copy(tile, n, buf, slot).start()
            return carry
        lax.fori_loop(0, tn, body, 0, unroll=8)

    def wait_all(tile, buf, slot):
        def body(n, carry):
            copy(tile, n, buf, slot).wait()
            return carry
        lax.fori_loop(0, tn, body, 0, unroll=8)

    def convert(buf, half):
        for l in range(depth):
            for kt in range(k_tiles):
                rows = pl.ds(kt * depth + l, tn, stride=UNPACK_PITCH)
                o_ref[l, half * tn:(half + 1) * tn, kt * LANES:(kt + 1) * LANES] = buf[rows, :].astype(BF16)

    @pl.when(i == 0)
    def _():
        start_all(0, buf_a, 0)

    start_all(2 * i + 1, buf_b, 1)
    wait_all(2 * i, buf_a, 0)
    convert(buf_a, 0)

    @pl.when(i + 1 < pl.num_programs(0))
    def _():
        start_all(2 * i + 2, buf_a, 0)

    wait_all(2 * i + 1, buf_b, 1)
    convert(buf_b, 1)


def _unpack_w_in(w_in):
    depth, d, d_in = w_in.shape
    k_tiles = d // LANES
    tn = LANES
    assert D_MAIN % (2 * tn) == 0 and k_tiles * depth <= UNPACK_PITCH
    cols = w_in.reshape(depth, k_tiles, LANES, d_in).transpose(3, 1, 0, 2).reshape(d_in, k_tiles * depth, LANES)
    return pl.pallas_call(
        functools.partial(_unpack_kernel, depth=depth, k_tiles=k_tiles, tn=tn),
        out_shape=jax.ShapeDtypeStruct((depth, D_MAIN, d), BF16),
        grid=(D_MAIN // (2 * tn),),
        in_specs=[pl.BlockSpec(memory_space=pl.ANY)],
        out_specs=pl.BlockSpec((depth, 2 * tn, d), lambda i: (0, i, 0)),
        scratch_shapes=[pltpu.VMEM((tn * UNPACK_PITCH, LANES), F32),
                        pltpu.VMEM((tn * UNPACK_PITCH, LANES), F32),
                        pltpu.SemaphoreType.DMA((2,))],
        compiler_params=_cparams(("arbitrary",)),
        name="unpack_w_in",
    )(cols)


def _inproj(h2d, w_in_t, w_gate, layer, tm):
    m = h2d.shape[0]
    tn = 768
    return pl.pallas_call(
        _inproj_kernel,
        out_shape=(jax.ShapeDtypeStruct((m, D_MAIN), F32), jax.ShapeDtypeStruct((m, LANES), F32)),
        grid=(m // tm, D_MAIN // tn),
        in_specs=[pl.BlockSpec((tm, D_MODEL), lambda i, j: (i, 0)),
                  pl.BlockSpec((None, tn, D_MODEL), lambda i, j: (layer, j, 0)),
                  pl.BlockSpec((None, D_MODEL, LANES), lambda i, j: (layer, 0, 0))],
        out_specs=(pl.BlockSpec((tm, tn), lambda i, j: (i, j)),
                   pl.BlockSpec((tm, LANES), lambda i, j: (i, 0))),
        compiler_params=_cparams(("arbitrary", "arbitrary")),
        name="inproj",
    )(h2d, w_in_t, w_gate)


def _outproj_kernel(ya_ref, yb_ref, yc_ref, x_ref, gate_ref, w_ref, *rest, emit_next):
    y = jnp.dot(ya_ref[...], w_ref[0:D_A, :], preferred_element_type=F32)
    y += jnp.dot(yb_ref[...], w_ref[D_A:D_A + D_B, :], preferred_element_type=F32)
    y += jnp.dot(yc_ref[...], w_ref[D_A + D_B:, :], preferred_element_type=F32)
    x_new = x_ref[...] + gate_ref[...] * y
    if emit_next:
        sc_ref, sh_ref, g_ref, o_ref, h_ref = rest
        h_ref[...] = _norm_mod(x_new, g_ref[...], sc_ref[...], sh_ref[...])
    else:
        (o_ref,) = rest
    o_ref[...] = x_new


def _outproj(ya, yb, yc, x2d, gate, w_out_bf, layer, tm, next_norm=None):
    m = x2d.shape[0]
    tiles_per_group = m // tm // gate.shape[0]
    mod_spec = pl.BlockSpec((None, gate.shape[1], D_MODEL), lambda i: (i // tiles_per_group, 0, 0))
    row = lambda width: pl.BlockSpec((tm, width), lambda i: (i, 0))
    in_specs = [row(D_A), row(D_B), row(D_C), row(D_MODEL), mod_spec,
                pl.BlockSpec((None, D_MODEL, D_MODEL), lambda i: (layer, 0, 0))]
    args = [ya, yb, yc, x2d, gate, w_out_bf]
    out_shape = [jax.ShapeDtypeStruct((m, D_MODEL), F32)]
    out_specs = [row(D_MODEL)]
    if next_norm is not None:
        in_specs += [mod_spec, mod_spec, pl.BlockSpec((1, D_MODEL), lambda i: (0, 0))]
        args += list(next_norm)
        out_shape.append(jax.ShapeDtypeStruct((m, D_MODEL), BF16))
        out_specs.append(row(D_MODEL))
    out = pl.pallas_call(
        functools.partial(_outproj_kernel, emit_next=next_norm is not None),
        out_shape=tuple(out_shape),
        grid=(m // tm,),
        in_specs=in_specs,
        out_specs=tuple(out_specs),
        compiler_params=_cparams(("arbitrary",)),
        name="outproj",
    )(*args)
    return out if next_norm is not None else (out[0], None)


def _head_mean_matrix():
    row = lax.broadcasted_iota(jnp.int32, (LANES, LANES), 0)
    col = lax.broadcasted_iota(jnp.int32, (LANES, LANES), 1)
    same_head = (row < HEAD_DIM_A) == (col < HEAD_DIM_A)
    return jnp.where(same_head, 1.0 / HEAD_DIM_A, 0.0).astype(BF16)


def _head_norm(x, g, mean_mat):
    x2 = x * x
    hi = x2.astype(BF16)
    lo = (x2 - hi.astype(F32)).astype(BF16)
    ms = (jnp.dot(hi, mean_mat, preferred_element_type=F32)
          + jnp.dot(lo, mean_mat, preferred_element_type=F32))
    return x * lax.rsqrt(ms + EPS) * g


def _attn_prompt_kernel(q_ref, k_ref, v_ref, z_ref, gq_ref, gk_ref, bias_ref,
                        y_ref, pk_ref, pv_ref,
                        qn_s, kp_s, vp_s, ones_s, o_s, l_s, m_s, *, seq, keep, unroll):
    norm_rows = 512
    mean_mat = _head_mean_matrix()

    kp_s[0:PAD_A, :] = jnp.zeros((PAD_A, LANES), F32)
    vp_s[0:PAD_A, :] = jnp.zeros((PAD_A, LANES), F32)
    ones_s[...] = jnp.ones(ones_s.shape, BF16)

    def norm_chunk(c, carry):
        r0 = pl.multiple_of(c * norm_rows, norm_rows)
        rows = pl.ds(r0, norm_rows)
        qn_s[rows, :] = _head_norm(q_ref[rows, :], gq_ref[...], mean_mat) * (HEAD_DIM_A ** -0.5 * LOG2E)
        kp_s[pl.ds(PAD_A + r0, norm_rows), :] = _head_norm(k_ref[rows, :], gk_ref[...], mean_mat)
        vp_s[pl.ds(PAD_A + r0, norm_rows), :] = v_ref[rows, :]
        return carry
    lax.fori_loop(0, seq // norm_rows, norm_chunk, 0, unroll=True)

    for c in range(keep // LANES):
        src = slice(PAD_A + seq - keep + c * LANES, PAD_A + seq - keep + (c + 1) * LANES)
        pk_ref[:, c * LANES:(c + 1) * LANES] = kp_s[src, :].T
        pv_ref[:, c * LANES:(c + 1) * LANES] = vp_s[src, :].T

    left_b = lax.broadcasted_iota(jnp.int32, (BLK_A, LANES), 1) < HEAD_DIM_A
    left_k = lax.broadcasted_iota(jnp.int32, (2 * BLK_A, LANES), 1) < HEAD_DIM_A

    for p, (win, dil) in enumerate(PATTERNS):
        blocks_per_residue = seq // (dil * BLK_A)

        def block(i, carry, p=p, dil=dil, blocks_per_residue=blocks_per_residue):
            r = i // blocks_per_residue
            n = i % blocks_per_residue
            q_start = r + n * (BLK_A * dil)
            k_start = PAD_A + q_start - BLK_A * dil
            if dil == 1:
                q_rows = pl.ds(pl.multiple_of(q_start, BLK_A), BLK_A)
                k_rows = pl.ds(pl.multiple_of(k_start, BLK_A), 2 * BLK_A)
            else:
                q_rows = pl.ds(q_start, BLK_A, stride=dil)
                k_rows = pl.ds(k_start, 2 * BLK_A, stride=dil)
            first = jnp.where(n == 0, 1, 0)
            q = qn_s[q_rows, :]
            k = kp_s[k_rows, :].astype(BF16)
            v = vp_s[k_rows, :].astype(BF16)
            rhs = jnp.concatenate([v, ones_s[...]], axis=1)
            q2 = jnp.concatenate([jnp.where(left_b, q, 0.0), jnp.where(left_b, 0.0, q)], axis=0)
            s2 = _bdot_nt(q2, k)
            es, ms = [], []
            for h in range(HPAIR):
                s = s2[h * BLK_A:(h + 1) * BLK_A, :] + bias_ref[h, p, first]
                mh = jnp.max(s, axis=-1, keepdims=True)
                es.append(jnp.exp2(s - mh).astype(BF16))
                ms.append(mh)
            res = jnp.dot(jnp.concatenate(es, axis=0), rhs, preferred_element_type=F32)
            o_s[p, q_rows, :] = jnp.where(left_b, res[0:BLK_A, 0:LANES], res[BLK_A:, 0:LANES])
            l_s[p, q_rows, :] = jnp.where(left_b, res[0:BLK_A, LANES:], res[BLK_A:, LANES:])
            m_s[p, q_rows, :] = jnp.where(left_b, ms[0], ms[1])
            return carry
        lax.fori_loop(0, seq // BLK_A, block, 0, unroll=unroll)

    def out_chunk(c, carry):
        rows = pl.ds(pl.multiple_of(c * norm_rows, norm_rows), norm_rows)
        m = [m_s[p, rows, :] for p in range(N_PAT)]
        top = functools.reduce(jnp.maximum, m)
        w = [jnp.exp2(x - top) for x in m]
        num = functools.reduce(lambda a, b: a + b, [w[p] * o_s[p, rows, :] for p in range(N_PAT)])
        den = functools.reduce(lambda a, b: a + b, [w[p] * l_s[p, rows, :] for p in range(N_PAT)])
        y_ref[rows, :] = (num / den * _silu(z_ref[rows, :])).astype(BF16)
        return carry
    lax.fori_loop(0, seq // norm_rows, out_chunk, 0)


def _attn_prompt(proj, gq2, gk2, bias_p):
    b, seq, _ = proj.shape
    keep = min(WIN_MAX, seq)
    assert seq % (BLK_A * PATTERNS[-1][1]) == 0
    col = lambda base: pl.BlockSpec((None, seq, LANES), lambda bi, hp: (bi, 0, base + hp))
    vec = pl.BlockSpec((1, LANES), lambda bi, hp: (0, 0))
    return pl.pallas_call(
        functools.partial(_attn_prompt_kernel, seq=seq, keep=keep, unroll=seq // BLK_A),
        out_shape=(jax.ShapeDtypeStruct((b, seq, D_A), BF16),
                   jax.ShapeDtypeStruct((b, D_A, keep), F32),
                   jax.ShapeDtypeStruct((b, D_A, keep), F32)),
        grid=(b, H_A // HPAIR),
        in_specs=[col(COL_Q), col(COL_K), col(COL_V), col(COL_Z), vec, vec,
                  pl.BlockSpec((HPAIR, N_PAT, 2, BLK_A, 2 * BLK_A), lambda bi, hp: (hp, 0, 0, 0, 0))],
        out_specs=(pl.BlockSpec((None, seq, LANES), lambda bi, hp: (bi, 0, hp)),
                   pl.BlockSpec((None, LANES, keep), lambda bi, hp: (bi, hp, 0)),
                   pl.BlockSpec((None, LANES, keep), lambda bi, hp: (bi, hp, 0))),
        scratch_shapes=[pltpu.VMEM((seq, LANES), F32),
                        pltpu.VMEM((PAD_A + seq, LANES), F32),
                        pltpu.VMEM((PAD_A + seq, LANES), F32),
                        pltpu.VMEM((2 * BLK_A, LANES), BF16),
                        pltpu.VMEM((N_PAT, seq, LANES), F32),
                        pltpu.VMEM((N_PAT, seq, LANES), F32),
                        pltpu.VMEM((N_PAT, seq, LANES), F32)],
        compiler_params=_cparams(("arbitrary", "arbitrary")),
        name="attn_prompt",
    )(proj, proj, proj, proj, gq2, gk2, bias_p)


def _attn_sample_kernel(q_ref, k_ref, v_ref, z_ref, ck_ref, cv_ref, gq_ref, gk_ref, bias_ref,
                        y_ref, nk_ref, nv_ref, *, n_past, t_new):
    q_all = q_ref[...]
    k_all = k_ref[...]
    v_all = v_ref[...]
    z_all = z_ref[...]
    nv_ref[...] = v_all
    pad_rows = jnp.zeros((LANES - t_new, HEAD_DIM_A), F32)

    def norm(x, g):
        ms = jnp.sum(x * x, axis=-1, keepdims=True) * (1.0 / HEAD_DIM_A)
        return x * lax.rsqrt(ms + EPS) * g

    for h in range(H_A):
        cols = slice(h * HEAD_DIM_A, (h + 1) * HEAD_DIM_A)
        qn = (norm(q_all[:, cols], gq_ref[:, cols]) * (HEAD_DIM_A ** -0.5)).astype(BF16)
        kn = norm(k_all[:, cols], gk_ref[:, cols])
        nk_ref[:, cols] = kn
        k_new = jnp.concatenate([kn, pad_rows], axis=0).astype(BF16)
        v_new = jnp.concatenate([v_all[:, cols], pad_rows], axis=0).astype(BF16)
        k_t = ck_ref[h].astype(BF16)
        v_t = cv_ref[h].astype(BF16)
        s_c = jnp.dot(qn, k_t, preferred_element_type=F32)
        s_n = _bdot_nt(qn, k_new)
        sc = [s_c + bias_ref[h, p, :, 0:n_past] for p in range(N_PAT)]
        sn = [s_n + bias_ref[h, p, :, n_past:] for p in range(N_PAT)]
        m = functools.reduce(jnp.maximum, [jnp.max(x, axis=-1, keepdims=True) for x in sc + sn])
        e_c = functools.reduce(lambda a, b: a + b, [jnp.exp(x - m) for x in sc])
        e_n = functools.reduce(lambda a, b: a + b, [jnp.exp(x - m) for x in sn])
        den = jnp.sum(e_c, axis=-1, keepdims=True) + jnp.sum(e_n, axis=-1, keepdims=True)
        o = _bdot_nt(e_c, v_t) + jnp.dot(e_n.astype(BF16), v_new, preferred_element_type=F32)
        y_ref[:, cols] = (o / den * _silu(z_all[:, cols])).astype(BF16)


def _attn_sample(proj_s, cache_kt, cache_vt, gq, gk, bias_s, layer):
    bd, t_new, _ = proj_s.shape
    n_past = cache_kt.shape[-1]
    col = lambda base: pl.BlockSpec((None, t_new, D_A), lambda bi: (bi, 0, base))
    vec = pl.BlockSpec((1, D_A), lambda bi: (0, 0))
    cache = pl.BlockSpec((None, None, H_A, HEAD_DIM_A, n_past), lambda bi: (layer, bi, 0, 0, 0))
    out = pl.BlockSpec((None, t_new, D_A), lambda bi: (bi, 0, 0))
    return pl.pallas_call(
        functools.partial(_attn_sample_kernel, n_past=n_past, t_new=t_new),
        out_shape=(jax.ShapeDtypeStruct((bd, t_new, D_A), BF16),
                   jax.ShapeDtypeStruct((bd, t_new, D_A), F32),
                   jax.ShapeDtypeStruct((bd, t_new, D_A), F32)),
        grid=(bd,),
        in_specs=[col(0), col(1), col(2), col(3), cache, cache, vec, vec,
                  pl.BlockSpec((H_A, N_PAT, t_new, n_past + LANES), lambda bi: (0, 0, 0, 0))],
        out_specs=(out, out, out),
        compiler_params=_cparams(("arbitrary",)),
        name="attn_sample",
    )(proj_s, proj_s, proj_s, proj_s, cache_kt, cache_vt, gq, gk, bias_s)


def _sgu_prompt_kernel(u_ref, v_ref, z_ref, g_ref, w_ref, bt_ref, y_ref, *, chunks):
    row = lax.broadcasted_iota(jnp.int32, (CHUNK_B, CHUNK_B), 0)
    colm = lax.broadcasted_iota(jnp.int32, (CHUNK_B, CHUNK_B), 1)
    tril = row >= colm
    ws = [jnp.where(tril, w_ref[g], 0.0).astype(BF16) for g in range(G_B)]
    for c in range(chunks):
        rows = slice(c * CHUNK_B, (c + 1) * CHUNK_B)
        v = v_ref[rows, :]
        ms = jnp.sum(v * v, axis=-1, keepdims=True) * (1.0 / D_B)
        vn = (v * lax.rsqrt(ms + EPS) * g_ref[...]).astype(BF16)
        for g in range(G_B):
            cols = slice(g * C_B, (g + 1) * C_B)
            mix = jnp.dot(ws[g], vn[:, cols], preferred_element_type=F32) + bt_ref[:, g:g + 1]
            y_ref[rows, cols] = (u_ref[rows, cols] * mix * _silu(z_ref[rows, cols])).astype(BF16)


def _sgu_prompt(proj, sgu_g, sgu_w, sgu_bt):
    b, seq, _ = proj.shape
    ts = 512
    col = lambda base: pl.BlockSpec((None, ts, D_B), lambda bi, i: (bi, i, base))
    return pl.pallas_call(
        functools.partial(_sgu_prompt_kernel, chunks=ts // CHUNK_B),
        out_shape=jax.ShapeDtypeStruct((b, seq, D_B), BF16),
        grid=(b, seq // ts),
        in_specs=[col(COL_UB), col(COL_VB), col(COL_ZB),
                  pl.BlockSpec((1, D_B), lambda bi, i: (0, 0)),
                  pl.BlockSpec((G_B, CHUNK_B, CHUNK_B), lambda bi, i: (0, 0, 0)),
                  pl.BlockSpec((CHUNK_B, G_B), lambda bi, i: (0, 0))],
        out_specs=pl.BlockSpec((None, ts, D_B), lambda bi, i: (bi, i, 0)),
        compiler_params=_cparams(("arbitrary", "arbitrary")),
        name="sgu_prompt",
    )(proj, proj, proj, sgu_g, sgu_w, sgu_bt)


def _sgu_sample_kernel(u_ref, v_ref, z_ref, g_ref, wexp_ref, bexp_ref, y_ref, vn_ref, *, t_new):
    v = v_ref[...]
    ms = jnp.sum(v * v, axis=-1, keepdims=True) * (1.0 / D_B)
    vn = v * lax.rsqrt(ms + EPS) * g_ref[...]
    vn_ref[...] = vn
    mix = bexp_ref[...]
    for s in range(t_new):
        mix = mix + wexp_ref[s] * vn[s:s + 1, :]
    y_ref[...] = (u_ref[...] * mix * _silu(z_ref[...])).astype(BF16)


def _sgu_sample(proj_s, sgu_g, wexp, bexp):
    bd, t_new, _ = proj_s.shape
    col = lambda base: pl.BlockSpec((None, t_new, D_B), lambda bi: (bi, 0, base))
    return pl.pallas_call(
        functools.partial(_sgu_sample_kernel, t_new=t_new),
        out_shape=(jax.ShapeDtypeStruct((bd, t_new, D_B), BF16),
                   jax.ShapeDtypeStruct((bd, t_new, D_B), F32)),
        grid=(bd,),
        in_specs=[col(COL_UB), col(COL_VB), col(COL_ZB),
                  pl.BlockSpec((1, D_B), lambda bi: (0, 0)),
                  pl.BlockSpec((t_new, t_new, D_B), lambda bi: (0, 0, 0)),
                  pl.BlockSpec((t_new, D_B), lambda bi: (0, 0))],
        out_specs=(pl.BlockSpec((None, t_new, D_B), lambda bi: (bi, 0, 0)),
                   pl.BlockSpec((None, t_new, D_B), lambda bi: (bi, 0, 0))),
        compiler_params=_cparams(("arbitrary",)),
        name="sgu_sample",
    )(proj_s, proj_s, proj_s, sgu_g, wexp, bexp)


def _log_sigmoid(x):
    return jnp.minimum(x, 0.0) - jnp.log1p(jnp.exp(-jnp.abs(x)))


MLSTM_SEQS_PER_STEP = 2
MLSTM_ROW0 = SUBLANES
MLSTM_CARRY = CONV_W - 1


def _mlstm_init(cbuf_ref, c0_ref, n0_ref, m0_ref, xp_s, vp_s, gp_s, caug_s, m_s):
    xp_s[...] = jnp.zeros(xp_s.shape, F32)
    vp_s[...] = jnp.zeros(vp_s.shape, F32)
    gp_s[...] = jnp.zeros(gp_s.shape, F32)
    xp_s[MLSTM_ROW0 - MLSTM_CARRY:MLSTM_ROW0, :] = cbuf_ref[...]
    n0 = n0_ref[...]
    n0_cols = jnp.concatenate([n0, jnp.zeros((DH_C - n0.shape[0], DH_C), F32)], axis=0).T
    for h in range(H_C):
        caug_s[h, :, 0:DH_C] = c0_ref[h]
        caug_s[h, :, DH_C:] = jnp.broadcast_to(n0_cols[:, h:h + 1], (DH_C, DH_C))
    m_s[...] = m0_ref[...]


def _mlstm_final(new_carry, conv_out_ref, c_out_ref, n_out_ref, m_out_ref, caug_s, m_s):
    conv_out_ref[...] = new_carry
    lane = lax.broadcasted_iota(jnp.int32, (DH_C, LANES), 1)
    n_cols = jnp.zeros((DH_C, LANES), F32)
    for h in range(H_C):
        c_out_ref[h] = caug_s[h, :, 0:DH_C]
        n_cols = jnp.where(lane == h, caug_s[h, :, DH_C:], n_cols)
    n_out_ref[...] = n_cols.T[0:SUBLANES, :]
    m_out_ref[...] = m_s[...]


def _mlstm_chunk(qk_ref, v_ref, o_ref, z_ref, gt_ref, cw_ref, cb_ref, gb_ref, hn_ref, y_ref,
                 xp_s, vp_s, gp_s, caug_s, m_s, *, t_rows):
    L = CHUNK_C
    base = MLSTM_ROW0
    carry_rows = MLSTM_CARRY
    row_l = lax.broadcasted_iota(jnp.int32, (L, LANES), 0)
    lane_l = lax.broadcasted_iota(jnp.int32, (L, LANES), 1)
    zero_rows = jnp.zeros((L - SUBLANES, LANES), F32)

    xp_s[base:base + t_rows, :] = qk_ref[...]
    vp_s[0:t_rows, :] = v_ref[...]
    gp_s[0:t_rows, :] = gt_ref[...]

    acc = cb_ref[...] + cw_ref[CONV_W - 1:CONV_W, :] * xp_s[base:base + L, :]
    for j in range(CONV_W - 1):
        off = base - carry_rows + j
        acc = acc + cw_ref[j:j + 1, :] * xp_s[off:off + L, :]
    qk = _silu(acc)
    new_carry = xp_s[base + t_rows - carry_rows:base + t_rows, :]
    xp_s[base - carry_rows:base, :] = new_carry

    gates_t = (gp_s[...] + gb_ref[...]).T
    lane_t = lax.broadcasted_iota(jnp.int32, (SUBLANES, L), 1)
    real_t = lane_t < t_rows
    i_t = jnp.where(real_t, gates_t[0:SUBLANES, :], NEG)
    logf_t = jnp.where(real_t, _log_sigmoid(gates_t[SUBLANES:2 * SUBLANES, :]), 0.0)
    upper = (row_l <= lane_l).astype(BF16)
    f1 = logf_t.astype(BF16)
    r1 = logf_t - f1.astype(F32)
    f2 = r1.astype(BF16)
    f3 = (r1 - f2.astype(F32)).astype(BF16)
    b_t = (jnp.dot(f1, upper, preferred_element_type=F32) + jnp.dot(f2, upper, preferred_element_type=F32)
           + jnp.dot(f3, upper, preferred_element_type=F32))
    m_prev = m_s[...]
    c_t = i_t - b_t
    pm = c_t
    shift = 1
    while shift < L:
        pm = jnp.maximum(pm, jnp.where(lane_t >= shift, pltpu.roll(pm, shift, axis=1), NEG))
        shift *= 2
    d_t = -jnp.maximum(m_prev, pm)
    wi_t = jnp.exp(m_prev + d_t)
    em_t = jnp.exp(-(b_t - d_t))
    bl = jnp.broadcast_to(b_t[:, L - 1:L], (SUBLANES, L))
    g_t = bl - b_t + i_t
    m_new = jnp.maximum(bl + m_prev, jnp.broadcast_to(jnp.max(g_t, axis=1, keepdims=True), (SUBLANES, L)))
    ws_t = jnp.exp(g_t - m_new)
    wc = jnp.exp(bl + m_prev - m_new)
    m_s[...] = m_new
    stats = jnp.concatenate([d_t, wi_t, em_t, ws_t, zero_rows[0:L - 4 * SUBLANES]], axis=0).T

    causal = row_l >= lane_l
    ones_blk = jnp.ones((L, DH_C), BF16)
    v_all = vp_s[...]
    o_all = o_ref[...]
    z_all = z_ref[...]

    for h in range(H_C):
        cols = slice(h * DH_C, (h + 1) * DH_C)
        q = qk[:, cols].astype(BF16)
        k = qk[:, D_C + h * DH_C:D_C + (h + 1) * DH_C] * (DH_C ** -0.5)
        v_aug = jnp.concatenate([v_all[:, cols].astype(BF16), ones_blk], axis=1)
        d_col = stats[:, h:h + 1]
        wi_col = stats[:, SUBLANES + h:SUBLANES + h + 1]
        em_col = stats[:, 2 * SUBLANES + h:2 * SUBLANES + h + 1]
        ws_col = stats[:, 3 * SUBLANES + h:3 * SUBLANES + h + 1]
        c_prev = caug_s[h]

        a = jnp.exp(jnp.where(causal, d_col + c_t[h:h + 1, :], NEG)) * _bdot_nt(q, k)
        r1_ = jnp.dot(a.astype(BF16), v_aug, preferred_element_type=F32)
        r2_ = jnp.dot(q, c_prev.astype(BF16), preferred_element_type=F32)
        num = r1_[:, 0:DH_C] + wi_col * r2_[:, 0:DH_C]
        den = r1_[:, DH_C:] + wi_col * r2_[:, DH_C:]
        hh = num / jnp.maximum(jnp.abs(den), em_col)

        wc_row = jnp.concatenate([wc[h:h + 1, :], wc[h:h + 1, :]], axis=1)
        caug_s[h] = wc_row * c_prev + _bdot_tn(k * ws_col, v_aug)

        ms = jnp.sum(hh * hh, axis=-1, keepdims=True) * (1.0 / DH_C)
        hn = hh * lax.rsqrt(ms + EPS) * hn_ref[:, cols]
        z = z_all[:, cols]
        gate = z / ((1.0 + jnp.exp(-o_all[:, cols])) * (1.0 + jnp.exp(-z)))
        y_ref[:, cols] = (hn[0:t_rows, :] * gate).astype(BF16)
    return new_carry


def _mlstm_kernel(*refs, t_rows, nb):
    (qk_b, v_b, o_b, z_b, gt_b, cbuf_b, c0_b, n0_b, m0_b, cw_ref, cb_ref, gb_ref, hn_ref,
     y_b, conv_out_b, c_out_b, n_out_b, m_out_b, xp_b, vp_b, gp_b, caug_b, m_b) = refs
    ci = pl.program_id(1)
    seqs = range(nb)

    @pl.when(ci == 0)
    def _():
        for bi in seqs:
            _mlstm_init(cbuf_b.at[bi], c0_b.at[bi], n0_b.at[bi], m0_b.at[bi],
                        xp_b.at[bi], vp_b.at[bi], gp_b.at[bi], caug_b.at[bi], m_b.at[bi])

    carries = [_mlstm_chunk(qk_b.at[bi], v_b.at[bi], o_b.at[bi], z_b.at[bi], gt_b.at[bi],
                            cw_ref, cb_ref, gb_ref, hn_ref, y_b.at[bi],
                            xp_b.at[bi], vp_b.at[bi], gp_b.at[bi], caug_b.at[bi], m_b.at[bi], t_rows=t_rows)
               for bi in seqs]

    @pl.when(ci == pl.num_programs(1) - 1)
    def _():
        for bi in seqs:
            _mlstm_final(carries[bi], conv_out_b.at[bi], c_out_b.at[bi], n_out_b.at[bi], m_out_b.at[bi],
                         caug_b.at[bi], m_b.at[bi])


def _mlstm(proj, gates, conv_buf, c0, n0, m0, conv_w, conv_b, gate_bias, hn_g, t_rows):
    b, seq, _ = proj.shape
    nchunks = seq // t_rows
    hp = SUBLANES
    nb = MLSTM_SEQS_PER_STEP
    assert b % nb == 0
    col = lambda base, width: pl.BlockSpec((nb, t_rows, width), lambda bi, ci: (bi, ci, base))
    const2 = lambda shape: pl.BlockSpec(shape, lambda bi, ci: (0, 0))
    per_b3 = lambda shape: pl.BlockSpec((nb,) + shape, lambda bi, ci: (bi, 0, 0))
    per_b4 = lambda shape: pl.BlockSpec((nb,) + shape, lambda bi, ci: (bi, 0, 0, 0))
    return pl.pallas_call(
        functools.partial(_mlstm_kernel, t_rows=t_rows, nb=nb),
        out_shape=(jax.ShapeDtypeStruct((b, seq, D_C), BF16),
                   jax.ShapeDtypeStruct((b, CONV_W - 1, 2 * D_C), F32),
                   jax.ShapeDtypeStruct((b, H_C, DH_C, DH_C), F32),
                   jax.ShapeDtypeStruct((b, hp, DH_C), F32),
                   jax.ShapeDtypeStruct((b, hp, LANES), F32)),
        grid=(b // nb, nchunks),
        in_specs=[col(COL_QK, 2 * D_C), col(COL_VC, D_C), col(COL_OC, D_C), col(COL_ZC, D_C),
                  pl.BlockSpec((nb, t_rows, LANES), lambda bi, ci: (bi, ci, 0)),
                  per_b3((CONV_W - 1, 2 * D_C)), per_b4((H_C, DH_C, DH_C)), per_b3((hp, DH_C)),
                  per_b3((hp, LANES)),
                  const2((CONV_W, 2 * D_C)), const2((1, 2 * D_C)), const2((1, LANES)), const2((1, D_C))],
        out_specs=(pl.BlockSpec((nb, t_rows, D_C), lambda bi, ci: (bi, ci, 0)),
                   per_b3((CONV_W - 1, 2 * D_C)), per_b4((H_C, DH_C, DH_C)), per_b3((hp, DH_C)),
                   per_b3((hp, LANES))),
        scratch_shapes=[pltpu.VMEM((nb, MLSTM_ROW0 + CHUNK_C, 2 * D_C), F32),
                        pltpu.VMEM((nb, CHUNK_C, D_C), F32),
                        pltpu.VMEM((nb, CHUNK_C, LANES), F32),
                        pltpu.VMEM((nb, H_C, DH_C, 2 * DH_C), F32),
                        pltpu.VMEM((nb, hp, LANES), F32)],
        compiler_params=_cparams(("arbitrary", "arbitrary")),
        name="mlstm",
    )(proj, proj, proj, proj, gates, conv_buf, c0, n0, m0, conv_w, conv_b, gate_bias, hn_g)


def _gate_lanes(i_part, f_part):
    lead = i_part.shape[:-1]
    gap = jnp.zeros(lead + (SUBLANES - H_C,), i_part.dtype)
    tail = jnp.zeros(lead + (LANES - 2 * SUBLANES,), i_part.dtype)
    return jnp.concatenate([i_part, gap, f_part, gap, tail], axis=-1)


def _pad_heads(a):
    pad = [(0, 0)] * a.ndim
    pad[1] = (0, SUBLANES - a.shape[1])
    return jnp.pad(a, pad)


def _mixers(proj, gates, lw, attn_fn, sgu_fn, conv_buf, c0, n0, m0, t_rows):
    ya, nk, nv = attn_fn(proj)
    sgu_out = sgu_fn(proj)
    m0b = jnp.broadcast_to(_pad_heads(m0)[:, :, None], (m0.shape[0], SUBLANES, LANES))
    yc, nconv, c_new, n_new, m_new = _mlstm(proj, gates, conv_buf, c0, _pad_heads(n0), m0b,
                                            lw["conv_w"], lw["conv_b"], lw["gate_bias"], lw["hn_g"], t_rows)
    return ya, sgu_out, yc, nk, nv, nconv, c_new, n_new[:, :H_C], m_new[:, :H_C, 0]


def kernel(x_prompt, x_sample, c_prompt, c_sample, cache_k_win, cache_v_win, state_conv, state_C, state_n, state_m, rel_bias, norm_g, ada_w, ada_b, w_in, qn_g, kn_g, sgu_g, sgu_w, sgu_b, conv_w, conv_b, f_bias, i_bias, hn_g, w_out):
    depth = w_in.shape[0]
    bp, seq, _ = x_prompt.shape
    bd, t_new, _ = x_sample.shape
    n_past = cache_k_win.shape[2]
    assert t_new == SUBLANES and n_past % LANES == 0 and seq % 1024 == 0

    w_in_t = _unpack_w_in(w_in)
    w_gate = _gate_lanes(w_in[:, :, D_MAIN:D_MAIN + H_C], w_in[:, :, D_MAIN + H_C:])
    w_out_bf = w_out.astype(BF16)

    rel_t = rel_bias.T
    bias_p = _expand_bias(rel_t, jnp.asarray(_prompt_bucket_table()), 64, LOG2E)
    bias_p = bias_p.reshape(H_A, N_PAT, 2, BLK_A, 2 * BLK_A)
    bias_s = _expand_bias(rel_t, jnp.asarray(_sample_bucket_table(n_past, t_new)), SUBLANES, 1.0)
    bias_s = bias_s.reshape(H_A, N_PAT, t_new, n_past + LANES)

    n_c = bp + bd
    c_all = jnp.pad(jnp.concatenate([c_prompt, c_sample], axis=0), ((0, 2 * SUBLANES - n_c), (0, 0)))
    mod = _ada_mod(c_all, ada_w, ada_b)

    cache_kt = jnp.transpose(cache_k_win, (0, 1, 3, 4, 2))
    cache_vt = jnp.transpose(cache_v_win, (0, 1, 3, 4, 2))
    gq_all = [jnp.tile(qn_g[l], H_A)[None] for l in range(depth)]
    gk_all = [jnp.tile(kn_g[l], H_A)[None] for l in range(depth)]
    zeros_conv = jnp.zeros((bp, CONV_W - 1, 2 * D_C), F32)
    zeros_c = jnp.zeros((bp, H_C, DH_C, DH_C), F32)
    zeros_n = jnp.zeros((bp, H_C, DH_C), F32)
    zeros_m = jnp.zeros((bp, H_C), F32)
    tril8 = jnp.tril(jnp.ones((t_new, t_new), F32))

    xp = x_prompt.reshape(bp * seq, D_MODEL)
    xs = x_sample.reshape(bd * t_new, D_MODEL)
    outs = [[] for _ in range(13)]
    p_mods, s_mods = [], []
    for l in range(depth):
        shift, scale, gate = (mod[l, :, i * D_MODEL:(i + 1) * D_MODEL] for i in range(3))
        p_mods.append([a[:bp, None, :] for a in (scale, shift, gate)])
        s_mods.append([jnp.repeat(a[bp:n_c], t_new, axis=0)[None] for a in (scale, shift, gate)])
    hp = _norm(xp, p_mods[0][0], p_mods[0][1], norm_g[0][None], 512)
    hs = _norm(xs, s_mods[0][0], s_mods[0][1], norm_g[0][None], bd * t_new)
    for l in range(depth):
        p_mod, s_mod = p_mods[l], s_mods[l]
        last = l == depth - 1
        p_next = None if last else (p_mods[l + 1][0], p_mods[l + 1][1], norm_g[l + 1][None])
        s_next = None if last else (s_mods[l + 1][0], s_mods[l + 1][1], norm_g[l + 1][None])
        lw = {
            "conv_w": conv_w[l], "conv_b": conv_b[l][None],
            "gate_bias": _gate_lanes(i_bias[l], f_bias[l])[None],
            "hn_g": hn_g[l][None],
        }
        gq2 = jnp.tile(qn_g[l], HPAIR)[None]
        gk2 = jnp.tile(kn_g[l], HPAIR)[None]
        sg = sgu_g[l][None]

        proj, gates = _inproj(hp, w_in_t, w_gate, l, 2048)
        proj = proj.reshape(bp, seq, D_MAIN)
        gates = gates.reshape(bp, seq, LANES)
        ya, yb, yc, nk, nv, ncv, nc_, nn_, nm = _mixers(
            proj, gates, lw,
            lambda pr: _attn_prompt(pr, gq2, gk2, bias_p),
            lambda pr: _sgu_prompt(pr, sg, sgu_w[l], sgu_b[l].T),
            zeros_conv, zeros_c, zeros_n, zeros_m, CHUNK_C)
        xp, hp = _outproj(ya.reshape(bp * seq, D_A), yb.reshape(bp * seq, D_B), yc.reshape(bp * seq, D_C),
                          xp, p_mod[2], w_out_bf, l, 512, p_next)
        keep = nk.shape[2]
        for i, a in enumerate((nk.reshape(bp, H_A, HEAD_DIM_A, keep), nv.reshape(bp, H_A, HEAD_DIM_A, keep),
                               ncv, nc_, nn_, nm)):
            outs[i].append(a)

        proj_s, gates_s = _inproj(hs, w_in_t, w_gate, l, bd * t_new)
        proj_s = proj_s.reshape(bd, t_new, D_MAIN)
        gates_s = gates_s.reshape(bd, t_new, LANES)
        w8 = sgu_w[l][:, :t_new, :t_new] * tril8
        wexp = jnp.repeat(jnp.transpose(w8, (2, 1, 0)), C_B, axis=2)
        bexp = jnp.repeat(sgu_b[l][:, :t_new].T, C_B, axis=1)
        ya, sgu_out, yc, nk, nv, ncv, nc_, nn_, nm = _mixers(
            proj_s, gates_s, lw,
            lambda pr: _attn_sample(pr, cache_kt, cache_vt, gq_all[l], gk_all[l], bias_s, l),
            lambda pr: _sgu_sample(pr, sg, wexp, bexp),
            state_conv[l], state_C[l], state_n[l], state_m[l], t_new)
        yb, vn = sgu_out
        xs, hs = _outproj(ya.reshape(bd * t_new, D_A), yb.reshape(bd * t_new, D_B), yc.reshape(bd * t_new, D_C),
                          xs, s_mod[2], w_out_bf, l, bd * t_new, s_next)
        for i, a in enumerate((nk.reshape(bd, t_new, H_A, HEAD_DIM_A), nv.reshape(bd, t_new, H_A, HEAD_DIM_A),
                               vn, ncv, nc_, nn_, nm)):
            outs[6 + i].append(a)

    stacked = [jnp.stack(o) for o in outs]
    for i in (0, 1):
        stacked[i] = jnp.transpose(stacked[i], (0, 1, 4, 2, 3))
    return (xp.reshape(bp, seq, D_MODEL), xs.reshape(bd, t_new, D_MODEL), *stacked)
```

```python
import functools
import math

import numpy as np
import jax
import jax.numpy as jnp
from jax import lax
from jax.experimental import pallas as pl
from jax.experimental.pallas import tpu as pltpu

F32 = jnp.float32
BF16 = jnp.bfloat16

D_MODEL = 2048
HEAD_DIM_A = 64
D_A = 768
H_A = 12
D_B = 512
G_B = 4
C_B = 128
CHUNK_B = 128
D_C = 768
DH_C = 128
H_C = 6
CHUNK_C = 128
CONV_W = 4
PATTERNS = ((128, 1), (512, 4), (2048, 16))
N_PAT = len(PATTERNS)
WIN_MAX = 2048
BLK_A = 128
N_BUCKETS = 32
MAX_DIST = 2048
EPS = 1e-6
D_MAIN = 4 * D_A + 3 * D_B + 2 * D_C + 3 * D_C
D_IN = D_MAIN + 2 * H_C

LANES = 128
SUBLANES = 8
VMEM_LIMIT = 56 * 1024 * 1024

NEG = -1e30
LOG2E = 1.4426950408889634
PAD_A = BLK_A * PATTERNS[-1][1]
HPAIR = LANES // HEAD_DIM_A

COL_Q, COL_K, COL_V, COL_Z = 0, D_A // LANES, 2 * D_A // LANES, 3 * D_A // LANES
COL_UB, COL_VB, COL_ZB = 4 * D_A // D_B, 4 * D_A // D_B + 1, 4 * D_A // D_B + 2
COL_QK = (4 * D_A + 3 * D_B) // (2 * D_C)
COL_VC = (4 * D_A + 3 * D_B + 2 * D_C) // D_C
COL_OC, COL_ZC = COL_VC + 1, COL_VC + 2


def _cparams(sem):
    return pltpu.CompilerParams(dimension_semantics=sem, vmem_limit_bytes=VMEM_LIMIT)


def _silu(x):
    return x * jax.nn.sigmoid(x)


def _bdot(a, b):
    return jnp.dot(a.astype(BF16), b.astype(BF16), preferred_element_type=F32)


def _bdot_nt(a, b):
    return lax.dot_general(a.astype(BF16), b.astype(BF16), (((1,), (1,)), ((), ())),
                           preferred_element_type=F32)


def _bdot_tn(a, b):
    return lax.dot_general(a.astype(BF16), b.astype(BF16), (((0,), (0,)), ((), ())),
                           preferred_element_type=F32)


def _bucket_np(dist):
    max_exact = N_BUCKETS // 2
    df = np.maximum(dist, 1).astype(np.float32)
    large = max_exact + (np.log(df / np.float32(max_exact)) / np.float32(math.log(MAX_DIST / max_exact))
                         * np.float32(N_BUCKETS - max_exact)).astype(np.int32)
    return np.where(dist < max_exact, dist, np.minimum(large, N_BUCKETS - 1)).astype(np.int32)


def _prompt_bucket_table():
    qi = np.arange(BLK_A)[:, None]
    ki = np.arange(2 * BLK_A)[None, :]
    j = qi + BLK_A - ki
    out = []
    for win, dil in PATTERNS:
        n_back = win // dil
        band = (j >= 0) & (j <= n_back)
        b = _bucket_np(np.clip(j, 0, n_back) * dil)
        out.append(np.where(band, b, -1))
        out.append(np.where(band & (ki >= BLK_A), b, -1))
    return np.stack(out).reshape(N_PAT * 2 * BLK_A, 2 * BLK_A).astype(np.int32)


def _sample_bucket_table(n_past, t_new):
    c = np.arange(n_past + LANES)[None, :]
    t = np.arange(t_new)[:, None]
    delta = n_past + t - c
    out = []
    for win, dil in PATTERNS:
        valid = (c < n_past + t_new) & (delta >= 0) & (delta % dil == 0) & (delta // dil <= win // dil)
        out.append(np.where(valid, _bucket_np(np.maximum(delta, 0)), -1))
    return np.stack(out).reshape(N_PAT * t_new, n_past + LANES).astype(np.int32)


def _bias_kernel(rb_ref, idx_ref, out_ref, *, scale, row_chunk):
    h = pl.program_id(0)

    def chunk(c, carry):
        rows = pl.ds(pl.multiple_of(c * row_chunk, row_chunk), row_chunk)
        idx = idx_ref[rows, :]
        out = jnp.full(idx.shape, NEG, F32)
        for b in range(N_BUCKETS):
            out = jnp.where(idx == b, rb_ref[h, b] * scale, out)
        out_ref[rows, :] = out
        return carry
    lax.fori_loop(0, idx_ref.shape[0] // row_chunk, chunk, 0)


def _expand_bias(rel_bias_t, idx, row_chunk, scale):
    rows, cols = idx.shape
    return pl.pallas_call(
        functools.partial(_bias_kernel, scale=scale, row_chunk=row_chunk),
        out_shape=jax.ShapeDtypeStruct((H_A, rows, cols), F32),
        grid=(H_A,),
        in_specs=[pl.BlockSpec(memory_space=pltpu.SMEM),
                  pl.BlockSpec((rows, cols), lambda h: (0, 0))],
        out_specs=pl.BlockSpec((None, rows, cols), lambda h: (h, 0, 0)),
        compiler_params=_cparams(("arbitrary",)),
        name="bias_expand",
    )(rel_bias_t, idx)


def _ada_kernel(c_ref, w_ref, b_ref, o_ref):
    c = c_ref[...]
    a = _silu(c)
    w = w_ref[...]
    a_hi = a.astype(BF16)
    a_lo = (a - a_hi.astype(F32)).astype(BF16)
    w_hi = w.astype(BF16)
    w_lo = (w - w_hi.astype(F32)).astype(BF16)
    acc = jnp.dot(a_hi, w_hi, preferred_element_type=F32)
    acc += jnp.dot(a_hi, w_lo, preferred_element_type=F32)
    acc += jnp.dot(a_lo, w_hi, preferred_element_type=F32)
    o_ref[...] = acc + b_ref[...]


def _ada_mod(c_all, ada_w, ada_b):
    depth = ada_w.shape[0]
    rows = c_all.shape[0]
    tn = 768
    n = 3 * D_MODEL
    return pl.pallas_call(
        _ada_kernel,
        out_shape=jax.ShapeDtypeStruct((depth, rows, n), F32),
        grid=(depth, n // tn),
        in_specs=[pl.BlockSpec((rows, D_MODEL), lambda l, j: (0, 0)),
                  pl.BlockSpec((None, D_MODEL, tn), lambda l, j: (l, 0, j)),
                  pl.BlockSpec((None, 1, tn), lambda l, j: (l, 0, j))],
        out_specs=pl.BlockSpec((None, rows, tn), lambda l, j: (l, 0, j)),
        compiler_params=_cparams(("arbitrary", "arbitrary")),
        name="ada_mod",
    )(c_all, ada_w, ada_b.reshape(depth, 1, n))


def _norm_mod(x, g, sc, sh):
    ms = jnp.sum(x * x, axis=-1, keepdims=True) * (1.0 / D_MODEL)
    return ((x * lax.rsqrt(ms + EPS) * g) * (1.0 + sc) + sh).astype(BF16)


def _norm_kernel(x_ref, sc_ref, sh_ref, g_ref, h_ref):
    h_ref[...] = _norm_mod(x_ref[...], g_ref[...], sc_ref[...], sh_ref[...])


MOD_SHIFT, MOD_SCALE, MOD_GATE = 0, 1, 2


def _mod_spec(mod5, layer, which, tiles_per_group):
    return pl.BlockSpec((None, None, None, mod5.shape[3], D_MODEL),
                        lambda i: (layer, which, i // tiles_per_group, 0, 0))


def _norm(x2d, mod5, norm_g3, layer, tm):
    m = x2d.shape[0]
    tiles_per_group = m // tm // mod5.shape[2]
    return pl.pallas_call(
        _norm_kernel,
        out_shape=jax.ShapeDtypeStruct((m, D_MODEL), BF16),
        grid=(m // tm,),
        in_specs=[pl.BlockSpec((tm, D_MODEL), lambda i: (i, 0)),
                  _mod_spec(mod5, layer, MOD_SCALE, tiles_per_group),
                  _mod_spec(mod5, layer, MOD_SHIFT, tiles_per_group),
                  pl.BlockSpec((None, 1, D_MODEL), lambda i: (layer, 0, 0))],
        out_specs=pl.BlockSpec((tm, D_MODEL), lambda i: (i, 0)),
        compiler_params=_cparams(("arbitrary",)),
        name="norm",
    )(x2d, mod5, mod5, norm_g3)


def _inproj_kernel(h_ref, w_ref, wg_ref, proj_ref, gates_ref):
    @pl.when(pl.program_id(1) == 0)
    def _():
        gates_ref[...] = jnp.dot(h_ref[...], wg_ref[...].astype(BF16), preferred_element_type=F32)

    proj_ref[...] = lax.dot_general(h_ref[...], w_ref[...], (((1,), (1,)), ((), ())),
                                    preferred_element_type=F32)


UNPACK_PITCH = 72


def _unpack_kernel(x_hbm, o_ref, buf_a, buf_b, sem, *, depth, k_tiles, tn):
    i = pl.program_id(0)
    rows_per_col = k_tiles * depth

    def copy(tile, n, buf, slot):
        return pltpu.make_async_copy(x_hbm.at[tile * tn + n], buf.at[pl.ds(n * UNPACK_PITCH, rows_per_col)],
                                     sem.at[slot])

    def start_all(tile, buf, slot):
        def body(n, carry):
            copy(tile, n, buf, slot).start()
            return carry
        lax.fori_loop(0, tn, body, 0, unroll=8)

    def wait_all(tile, buf, slot):
        def body(n, carry):
            copy(tile, n, buf, slot).wait()
            return carry
        lax.fori_loop(0, tn, body, 0, unroll=8)

    def convert(buf, half):
        for l in range(depth):
            for kt in range(k_tiles):
                rows = pl.ds(kt * depth + l, tn, stride=UNPACK_PITCH)
                o_ref[l, half * tn:(half + 1) * tn, kt * LANES:(kt + 1) * LANES] = buf[rows, :].astype(BF16)

    @pl.when(i == 0)
    def _():
        start_all(0, buf_a, 0)

    start_all(2 * i + 1, buf_b, 1)
    wait_all(2 * i, buf_a, 0)
    convert(buf_a, 0)

    @pl.when(i + 1 < pl.num_programs(0))
    def _():
        start_all(2 * i + 2, buf_a, 0)

    wait_all(2 * i + 1, buf_b, 1)
    convert(buf_b, 1)


def _unpack_w_in(w_in):
    depth, d, d_in = w_in.shape
    k_tiles = d // LANES
    tn = LANES
    assert D_MAIN % (2 * tn) == 0 and k_tiles * depth <= UNPACK_PITCH
    cols = w_in.reshape(depth, k_tiles, LANES, d_in).transpose(3, 1, 0, 2).reshape(d_in, k_tiles * depth, LANES)
    return pl.pallas_call(
        functools.partial(_unpack_kernel, depth=depth, k_tiles=k_tiles, tn=tn),
        out_shape=jax.ShapeDtypeStruct((depth, D_MAIN, d), BF16),
        grid=(D_MAIN // (2 * tn),),
        in_specs=[pl.BlockSpec(memory_space=pl.ANY)],
        out_specs=pl.BlockSpec((depth, 2 * tn, d), lambda i: (0, i, 0)),
        scratch_shapes=[pltpu.VMEM((tn * UNPACK_PITCH, LANES), F32),
                        pltpu.VMEM((tn * UNPACK_PITCH, LANES), F32),
                        pltpu.SemaphoreType.DMA((2,))],
        compiler_params=_cparams(("arbitrary",)),
        name="unpack_w_in",
    )(cols)


def _inproj(h2d, w_in_t, w_gate, layer, tm):
    m = h2d.shape[0]
    tn = 768
    return pl.pallas_call(
        _inproj_kernel,
        out_shape=(jax.ShapeDtypeStruct((m, D_MAIN), F32), jax.ShapeDtypeStruct((m, LANES), F32)),
        grid=(m // tm, D_MAIN // tn),
        in_specs=[pl.BlockSpec((tm, D_MODEL), lambda i, j: (i, 0)),
                  pl.BlockSpec((None, tn, D_MODEL), lambda i, j: (layer, j, 0)),
                  pl.BlockSpec((None, D_MODEL, LANES), lambda i, j: (layer, 0, 0))],
        out_specs=(pl.BlockSpec((tm, tn), lambda i, j: (i, j)),
                   pl.BlockSpec((tm, LANES), lambda i, j: (i, 0))),
        compiler_params=_cparams(("arbitrary", "arbitrary")),
        name="inproj",
    )(h2d, w_in_t, w_gate)


def _outproj_kernel(ya_ref, yb_ref, yc_ref, x_ref, gate_ref, w_ref, *rest, emit_next):
    y = jnp.dot(ya_ref[...], w_ref[0:D_A, :], preferred_element_type=F32)
    y += jnp.dot(yb_ref[...], w_ref[D_A:D_A + D_B, :], preferred_element_type=F32)
    y += jnp.dot(yc_ref[...], w_ref[D_A + D_B:, :], preferred_element_type=F32)
    x_new = x_ref[...] + gate_ref[...] * y
    if emit_next:
        sc_ref, sh_ref, g_ref, o_ref, h_ref = rest
        h_ref[...] = _norm_mod(x_new, g_ref[...], sc_ref[...], sh_ref[...])
    else:
        (o_ref,) = rest
    o_ref[...] = x_new


def _outproj(ya, yb, yc, x2d, mod5, norm_g3, w_out_bf, layer, tm, emit_next):
    m = x2d.shape[0]
    tiles_per_group = m // tm // mod5.shape[2]
    row = lambda width: pl.BlockSpec((tm, width), lambda i: (i, 0))
    in_specs = [row(D_A), row(D_B), row(D_C), row(D_MODEL),
                _mod_spec(mod5, layer, MOD_GATE, tiles_per_group),
                pl.BlockSpec((None, D_MODEL, D_MODEL), lambda i: (layer, 0, 0))]
    args = [ya, yb, yc, x2d, mod5, w_out_bf]
    out_shape = [jax.ShapeDtypeStruct((m, D_MODEL), F32)]
    out_specs = [row(D_MODEL)]
    if emit_next:
        in_specs += [_mod_spec(mod5, layer + 1, MOD_SCALE, tiles_per_group),
                     _mod_spec(mod5, layer + 1, MOD_SHIFT, tiles_per_group),
                     pl.BlockSpec((None, 1, D_MODEL), lambda i: (layer + 1, 0, 0))]
        args += [mod5, mod5, norm_g3]
        out_shape.append(jax.ShapeDtypeStruct((m, D_MODEL), BF16))
        out_specs.append(row(D_MODEL))
    out = pl.pallas_call(
        functools.partial(_outproj_kernel, emit_next=emit_next),
        out_shape=tuple(out_shape),
        grid=(m // tm,),
        in_specs=in_specs,
        out_specs=tuple(out_specs),
        compiler_params=_cparams(("arbitrary",)),
        name="outproj",
    )(*args)
    return out if emit_next else (out[0], None)


def _head_mean_matrix():
    row = lax.broadcasted_iota(jnp.int32, (LANES, LANES), 0)
    col = lax.broadcasted_iota(jnp.int32, (LANES, LANES), 1)
    same_head = (row < HEAD_DIM_A) == (col < HEAD_DIM_A)
    return jnp.where(same_head, 1.0 / HEAD_DIM_A, 0.0).astype(BF16)


def _head_norm(x, g, mean_mat):
    x2 = x * x
    hi = x2.astype(BF16)
    lo = (x2 - hi.astype(F32)).astype(BF16)
    ms = (jnp.dot(hi, mean_mat, preferred_element_type=F32)
          + jnp.dot(lo, mean_mat, preferred_element_type=F32))
    return x * lax.rsqrt(ms + EPS) * g


def _attn_prompt_kernel(q_ref, k_ref, v_ref, z_ref, gq_ref, gk_ref, bias_ref, pk_all_hbm, pv_all_hbm,
                        y_ref, pk_ref, pv_ref,
                        qn_s, kp_s, vp_s, ones_s, o_s, l_s, m_s, *, seq, keep, unroll):
    norm_rows = 512
    mean_mat = _head_mean_matrix()

    kp_s[0:PAD_A, :] = jnp.zeros((PAD_A, LANES), F32)
    vp_s[0:PAD_A, :] = jnp.zeros((PAD_A, LANES), F32)
    ones_s[...] = jnp.ones(ones_s.shape, BF16)

    def norm_chunk(c, carry):
        r0 = pl.multiple_of(c * norm_rows, norm_rows)
        rows = pl.ds(r0, norm_rows)
        qn_s[rows, :] = _head_norm(q_ref[rows, :], gq_ref[...], mean_mat) * (HEAD_DIM_A ** -0.5 * LOG2E)
        kp_s[pl.ds(PAD_A + r0, norm_rows), :] = _head_norm(k_ref[rows, :], gk_ref[...], mean_mat)
        vp_s[pl.ds(PAD_A + r0, norm_rows), :] = v_ref[rows, :]
        return carry
    lax.fori_loop(0, seq // norm_rows, norm_chunk, 0, unroll=True)

    for c in range(keep // LANES):
        src = slice(PAD_A + seq - keep + c * LANES, PAD_A + seq - keep + (c + 1) * LANES)
        pk_ref[:, c * LANES:(c + 1) * LANES] = kp_s[src, :].T
        pv_ref[:, c * LANES:(c + 1) * LANES] = vp_s[src, :].T

    left_b = lax.broadcasted_iota(jnp.int32, (BLK_A, LANES), 1) < HEAD_DIM_A
    left_k = lax.broadcasted_iota(jnp.int32, (2 * BLK_A, LANES), 1) < HEAD_DIM_A

    for p, (win, dil) in enumerate(PATTERNS):
        blocks_per_residue = seq // (dil * BLK_A)

        def block(i, carry, p=p, dil=dil, blocks_per_residue=blocks_per_residue):
            r = i // blocks_per_residue
            n = i % blocks_per_residue
            q_start = r + n * (BLK_A * dil)
            k_start = PAD_A + q_start - BLK_A * dil
            if dil == 1:
                q_rows = pl.ds(pl.multiple_of(q_start, BLK_A), BLK_A)
                k_rows = pl.ds(pl.multiple_of(k_start, BLK_A), 2 * BLK_A)
            else:
                q_rows = pl.ds(q_start, BLK_A, stride=dil)
                k_rows = pl.ds(k_start, 2 * BLK_A, stride=dil)
            first = jnp.where(n == 0, 1, 0)
            q = qn_s[q_rows, :]
            k = kp_s[k_rows, :].astype(BF16)
            v = vp_s[k_rows, :].astype(BF16)
            rhs = jnp.concatenate([v, ones_s[...]], axis=1)
            q2 = jnp.concatenate([jnp.where(left_b, q, 0.0), jnp.where(left_b, 0.0, q)], axis=0)
            s2 = _bdot_nt(q2, k)
            es, ms = [], []
            for h in range(HPAIR):
                s = s2[h * BLK_A:(h + 1) * BLK_A, :] + bias_ref[h, p, first]
                mh = jnp.max(s, axis=-1, keepdims=True)
                es.append(jnp.exp2(s - mh).astype(BF16))
                ms.append(mh)
            res = jnp.dot(jnp.concatenate(es, axis=0), rhs, preferred_element_type=F32)
            o_s[p, q_rows, :] = jnp.where(left_b, res[0:BLK_A, 0:LANES], res[BLK_A:, 0:LANES])
            l_s[p, q_rows, :] = jnp.where(left_b, res[0:BLK_A, LANES:], res[BLK_A:, LANES:])
            m_s[p, q_rows, :] = jnp.where(left_b, ms[0], ms[1])
            return carry
        lax.fori_loop(0, seq // BLK_A, block, 0, unroll=unroll)

    def out_chunk(c, carry):
        rows = pl.ds(pl.multiple_of(c * norm_rows, norm_rows), norm_rows)
        m = [m_s[p, rows, :] for p in range(N_PAT)]
        top = functools.reduce(jnp.maximum, m)
        w = [jnp.exp2(x - top) for x in m]
        num = functools.reduce(lambda a, b: a + b, [w[p] * o_s[p, rows, :] for p in range(N_PAT)])
        den = functools.reduce(lambda a, b: a + b, [w[p] * l_s[p, rows, :] for p in range(N_PAT)])
        y_ref[rows, :] = (num / den * _silu(z_ref[rows, :])).astype(BF16)
        return carry
    lax.fori_loop(0, seq // norm_rows, out_chunk, 0)


def _attn_prompt(proj, gq_t, gk_t, bias_p, pk_all, pv_all, layer):
    b, seq, _ = proj.shape
    keep = pk_all.shape[-1]
    assert seq % (BLK_A * PATTERNS[-1][1]) == 0 and keep == min(WIN_MAX, seq)
    col = lambda base: pl.BlockSpec((None, seq, LANES), lambda bi, hp: (bi, 0, base + hp))
    vec = pl.BlockSpec((None, 1, LANES), lambda bi, hp: (layer, 0, hp))
    window = pl.BlockSpec((None, None, LANES, keep), lambda bi, hp: (layer, bi, hp, 0))
    untouched = pl.BlockSpec(memory_space=pl.ANY)
    return pl.pallas_call(
        functools.partial(_attn_prompt_kernel, seq=seq, keep=keep, unroll=seq // BLK_A),
        out_shape=(jax.ShapeDtypeStruct((b, seq, D_A), BF16),
                   jax.ShapeDtypeStruct(pk_all.shape, F32),
                   jax.ShapeDtypeStruct(pv_all.shape, F32)),
        grid=(b, H_A // HPAIR),
        in_specs=[col(COL_Q), col(COL_K), col(COL_V), col(COL_Z), vec, vec,
                  pl.BlockSpec((HPAIR, N_PAT, 2, BLK_A, 2 * BLK_A), lambda bi, hp: (hp, 0, 0, 0, 0)),
                  untouched, untouched],
        out_specs=(pl.BlockSpec((None, seq, LANES), lambda bi, hp: (bi, 0, hp)), window, window),
        input_output_aliases={7: 1, 8: 2},
        scratch_shapes=[pltpu.VMEM((seq, LANES), F32),
                        pltpu.VMEM((PAD_A + seq, LANES), F32),
                        pltpu.VMEM((PAD_A + seq, LANES), F32),
                        pltpu.VMEM((2 * BLK_A, LANES), BF16),
                        pltpu.VMEM((N_PAT, seq, LANES), F32),
                        pltpu.VMEM((N_PAT, seq, LANES), F32),
                        pltpu.VMEM((N_PAT, seq, LANES), F32)],
        compiler_params=_cparams(("arbitrary", "arbitrary")),
        name="attn_prompt",
    )(proj, proj, proj, proj, gq_t, gk_t, bias_p, pk_all, pv_all)


def _attn_sample_kernel(q_ref, k_ref, v_ref, z_ref, ck_ref, cv_ref, gq_ref, gk_ref, bias_ref,
                        y_ref, nk_ref, nv_ref, *, n_past, t_new):
    q_all = q_ref[...]
    k_all = k_ref[...]
    v_all = v_ref[...]
    z_all = z_ref[...]
    nv_ref[...] = v_all
    pad_rows = jnp.zeros((LANES - t_new, HEAD_DIM_A), F32)

    def norm(x, g):
        ms = jnp.sum(x * x, axis=-1, keepdims=True) * (1.0 / HEAD_DIM_A)
        return x * lax.rsqrt(ms + EPS) * g

    for h in range(H_A):
        cols = slice(h * HEAD_DIM_A, (h + 1) * HEAD_DIM_A)
        qn = (norm(q_all[:, cols], gq_ref[:, cols]) * (HEAD_DIM_A ** -0.5)).astype(BF16)
        kn = norm(k_all[:, cols], gk_ref[:, cols])
        nk_ref[:, cols] = kn
        k_new = jnp.concatenate([kn, pad_rows], axis=0).astype(BF16)
        v_new = jnp.concatenate([v_all[:, cols], pad_rows], axis=0).astype(BF16)
        k_t = ck_ref[h].astype(BF16)
        v_t = cv_ref[h].astype(BF16)
        s_c = jnp.dot(qn, k_t, preferred_element_type=F32)
        s_n = _bdot_nt(qn, k_new)
        sc = [s_c + bias_ref[h, p, :, 0:n_past] for p in range(N_PAT)]
        sn = [s_n + bias_ref[h, p, :, n_past:] for p in range(N_PAT)]
        m = functools.reduce(jnp.maximum, [jnp.max(x, axis=-1, keepdims=True) for x in sc + sn])
        e_c = functools.reduce(lambda a, b: a + b, [jnp.exp(x - m) for x in sc])
        e_n = functools.reduce(lambda a, b: a + b, [jnp.exp(x - m) for x in sn])
        den = jnp.sum(e_c, axis=-1, keepdims=True) + jnp.sum(e_n, axis=-1, keepdims=True)
        o = _bdot_nt(e_c, v_t) + jnp.dot(e_n.astype(BF16), v_new, preferred_element_type=F32)
        y_ref[:, cols] = (o / den * _silu(z_all[:, cols])).astype(BF16)


def _attn_sample(proj_s, cache_kt, cache_vt, gq, gk, bias_s, layer):
    bd, t_new, _ = proj_s.shape
    n_past = cache_kt.shape[-1]
    col = lambda base: pl.BlockSpec((None, t_new, D_A), lambda bi: (bi, 0, base))
    vec = pl.BlockSpec((None, 1, D_A), lambda bi: (layer, 0, 0))
    cache = pl.BlockSpec((None, None, H_A, HEAD_DIM_A, n_past), lambda bi: (layer, bi, 0, 0, 0))
    out = pl.BlockSpec((None, t_new, D_A), lambda bi: (bi, 0, 0))
    return pl.pallas_call(
        functools.partial(_attn_sample_kernel, n_past=n_past, t_new=t_new),
        out_shape=(jax.ShapeDtypeStruct((bd, t_new, D_A), BF16),
                   jax.ShapeDtypeStruct((bd, t_new, D_A), F32),
                   jax.ShapeDtypeStruct((bd, t_new, D_A), F32)),
        grid=(bd,),
        in_specs=[col(0), col(1), col(2), col(3), cache, cache, vec, vec,
                  pl.BlockSpec((H_A, N_PAT, t_new, n_past + LANES), lambda bi: (0, 0, 0, 0))],
        out_specs=(out, out, out),
        compiler_params=_cparams(("arbitrary",)),
        name="attn_sample",
    )(proj_s, proj_s, proj_s, proj_s, cache_kt, cache_vt, gq, gk, bias_s)


def _sgu_prompt_kernel(u_ref, v_ref, z_ref, g_ref, w_ref, bt_ref, y_ref, *, chunks):
    row = lax.broadcasted_iota(jnp.int32, (CHUNK_B, CHUNK_B), 0)
    colm = lax.broadcasted_iota(jnp.int32, (CHUNK_B, CHUNK_B), 1)
    tril = row >= colm
    ws = [jnp.where(tril, w_ref[g], 0.0).astype(BF16) for g in range(G_B)]
    for c in range(chunks):
        rows = slice(c * CHUNK_B, (c + 1) * CHUNK_B)
        v = v_ref[rows, :]
        ms = jnp.sum(v * v, axis=-1, keepdims=True) * (1.0 / D_B)
        vn = (v * lax.rsqrt(ms + EPS) * g_ref[...]).astype(BF16)
        for g in range(G_B):
            cols = slice(g * C_B, (g + 1) * C_B)
            mix = jnp.dot(ws[g], vn[:, cols], preferred_element_type=F32) + bt_ref[:, g:g + 1]
            y_ref[rows, cols] = (u_ref[rows, cols] * mix * _silu(z_ref[rows, cols])).astype(BF16)


def _sgu_prompt(proj, sgu_g3, sgu_w, sgu_bt, layer):
    b, seq, _ = proj.shape
    ts = 1024
    col = lambda base: pl.BlockSpec((None, ts, D_B), lambda bi, i: (bi, i, base))
    return pl.pallas_call(
        functools.partial(_sgu_prompt_kernel, chunks=ts // CHUNK_B),
        out_shape=jax.ShapeDtypeStruct((b, seq, D_B), BF16),
        grid=(b, seq // ts),
        in_specs=[col(COL_UB), col(COL_VB), col(COL_ZB),
                  pl.BlockSpec((None, 1, D_B), lambda bi, i: (layer, 0, 0)),
                  pl.BlockSpec((None, G_B, CHUNK_B, CHUNK_B), lambda bi, i: (layer, 0, 0, 0)),
                  pl.BlockSpec((None, CHUNK_B, G_B), lambda bi, i: (layer, 0, 0))],
        out_specs=pl.BlockSpec((None, ts, D_B), lambda bi, i: (bi, i, 0)),
        compiler_params=_cparams(("arbitrary", "arbitrary")),
        name="sgu_prompt",
    )(proj, proj, proj, sgu_g3, sgu_w, sgu_bt)


def _sgu_sample_kernel(u_ref, v_ref, z_ref, g_ref, wexp_ref, bexp_ref, y_ref, vn_ref, *, t_new):
    v = v_ref[...]
    ms = jnp.sum(v * v, axis=-1, keepdims=True) * (1.0 / D_B)
    vn = v * lax.rsqrt(ms + EPS) * g_ref[...]
    vn_ref[...] = vn
    mix = bexp_ref[...]
    for s in range(t_new):
        mix = mix + wexp_ref[s] * vn[s:s + 1, :]
    y_ref[...] = (u_ref[...] * mix * _silu(z_ref[...])).astype(BF16)


def _sgu_sample(proj_s, sgu_g3, wexp, bexp, layer):
    bd, t_new, _ = proj_s.shape
    col = lambda base: pl.BlockSpec((None, t_new, D_B), lambda bi: (bi, 0, base))
    return pl.pallas_call(
        functools.partial(_sgu_sample_kernel, t_new=t_new),
        out_shape=(jax.ShapeDtypeStruct((bd, t_new, D_B), BF16),
                   jax.ShapeDtypeStruct((bd, t_new, D_B), F32)),
        grid=(bd,),
        in_specs=[col(COL_UB), col(COL_VB), col(COL_ZB),
                  pl.BlockSpec((None, 1, D_B), lambda bi: (layer, 0, 0)),
                  pl.BlockSpec((None, t_new, t_new, D_B), lambda bi: (layer, 0, 0, 0)),
                  pl.BlockSpec((None, t_new, D_B), lambda bi: (layer, 0, 0))],
        out_specs=(pl.BlockSpec((None, t_new, D_B), lambda bi: (bi, 0, 0)),
                   pl.BlockSpec((None, t_new, D_B), lambda bi: (bi, 0, 0))),
        compiler_params=_cparams(("arbitrary",)),
        name="sgu_sample",
    )(proj_s, proj_s, proj_s, sgu_g3, wexp, bexp)


def _log_sigmoid(x):
    return jnp.minimum(x, 0.0) - jnp.log1p(jnp.exp(-jnp.abs(x)))


MLSTM_SEQS_PER_STEP = 2
MLSTM_ROW0 = SUBLANES
MLSTM_CARRY = CONV_W - 1


def _mlstm_init(cbuf_ref, c0_ref, n0_ref, m0_ref, xp_s, vp_s, gp_s, caug_s, m_s):
    xp_s[...] = jnp.zeros(xp_s.shape, F32)
    vp_s[...] = jnp.zeros(vp_s.shape, F32)
    gp_s[...] = jnp.zeros(gp_s.shape, F32)
    xp_s[MLSTM_ROW0 - MLSTM_CARRY:MLSTM_ROW0, :] = cbuf_ref[...]
    n0 = n0_ref[...]
    n0_cols = jnp.concatenate([n0, jnp.zeros((DH_C - n0.shape[0], DH_C), F32)], axis=0).T
    for h in range(H_C):
        caug_s[h, :, 0:DH_C] = c0_ref[h]
        caug_s[h, :, DH_C:] = jnp.broadcast_to(n0_cols[:, h:h + 1], (DH_C, DH_C))
    m_s[...] = m0_ref[...]


def _mlstm_final(new_carry, conv_out_ref, c_out_ref, n_out_ref, m_out_ref, caug_s, m_s):
    conv_out_ref[...] = new_carry
    lane = lax.broadcasted_iota(jnp.int32, (DH_C, LANES), 1)
    n_cols = jnp.zeros((DH_C, LANES), F32)
    for h in range(H_C):
        c_out_ref[h] = caug_s[h, :, 0:DH_C]
        n_cols = jnp.where(lane == h, caug_s[h, :, DH_C:], n_cols)
    n_out_ref[...] = n_cols.T[0:SUBLANES, :]
    m_out_ref[...] = m_s[...]


def _mlstm_chunk(qk_ref, v_ref, o_ref, z_ref, gt_ref, cw_ref, cb_ref, gb_ref, hn_ref, y_ref,
                 xp_s, vp_s, gp_s, caug_s, m_s, *, t_rows):
    L = CHUNK_C
    base = MLSTM_ROW0
    carry_rows = MLSTM_CARRY
    row_l = lax.broadcasted_iota(jnp.int32, (L, LANES), 0)
    lane_l = lax.broadcasted_iota(jnp.int32, (L, LANES), 1)
    zero_rows = jnp.zeros((L - SUBLANES, LANES), F32)

    xp_s[base:base + t_rows, :] = qk_ref[...]
    vp_s[0:t_rows, :] = v_ref[...]
    gp_s[0:t_rows, :] = gt_ref[...]

    acc = cb_ref[...] + cw_ref[CONV_W - 1:CONV_W, :] * xp_s[base:base + L, :]
    for j in range(CONV_W - 1):
        off = base - carry_rows + j
        acc = acc + cw_ref[j:j + 1, :] * xp_s[off:off + L, :]
    qk = _silu(acc)
    new_carry = xp_s[base + t_rows - carry_rows:base + t_rows, :]
    xp_s[base - carry_rows:base, :] = new_carry

    gates_t = (gp_s[...] + gb_ref[...]).T
    lane_t = lax.broadcasted_iota(jnp.int32, (SUBLANES, L), 1)
    real_t = lane_t < t_rows
    i_t = jnp.where(real_t, gates_t[0:SUBLANES, :], NEG)
    logf_t = jnp.where(real_t, _log_sigmoid(gates_t[SUBLANES:2 * SUBLANES, :]), 0.0)
    upper = (row_l <= lane_l).astype(BF16)
    f1 = logf_t.astype(BF16)
    r1 = logf_t - f1.astype(F32)
    f2 = r1.astype(BF16)
    f3 = (r1 - f2.astype(F32)).astype(BF16)
    b_t = (jnp.dot(f1, upper, preferred_element_type=F32) + jnp.dot(f2, upper, preferred_element_type=F32)
           + jnp.dot(f3, upper, preferred_element_type=F32))
    m_prev = m_s[...]
    c_t = i_t - b_t
    pm = c_t
    shift = 1
    while shift < L:
        pm = jnp.maximum(pm, jnp.where(lane_t >= shift, pltpu.roll(pm, shift, axis=1), NEG))
        shift *= 2
    d_t = -jnp.maximum(m_prev, pm)
    wi_t = jnp.exp(m_prev + d_t)
    em_t = jnp.exp(-(b_t - d_t))
    bl = jnp.broadcast_to(b_t[:, L - 1:L], (SUBLANES, L))
    g_t = bl - b_t + i_t
    m_new = jnp.maximum(bl + m_prev, jnp.broadcast_to(jnp.max(g_t, axis=1, keepdims=True), (SUBLANES, L)))
    ws_t = jnp.exp(g_t - m_new)
    wc = jnp.exp(bl + m_prev - m_new)
    m_s[...] = m_new
    stats = jnp.concatenate([d_t, wi_t, em_t, ws_t, zero_rows[0:L - 4 * SUBLANES]], axis=0).T

    causal = row_l >= lane_l
    ones_blk = jnp.ones((L, DH_C), BF16)
    v_all = vp_s[...]
    o_all = o_ref[...]
    z_all = z_ref[...]

    for h in range(H_C):
        cols = slice(h * DH_C, (h + 1) * DH_C)
        q = qk[:, cols].astype(BF16)
        k = qk[:, D_C + h * DH_C:D_C + (h + 1) * DH_C] * (DH_C ** -0.5)
        v_aug = jnp.concatenate([v_all[:, cols].astype(BF16), ones_blk], axis=1)
        d_col = stats[:, h:h + 1]
        wi_col = stats[:, SUBLANES + h:SUBLANES + h + 1]
        em_col = stats[:, 2 * SUBLANES + h:2 * SUBLANES + h + 1]
        ws_col = stats[:, 3 * SUBLANES + h:3 * SUBLANES + h + 1]
        c_prev = caug_s[h]

        a = jnp.exp(jnp.where(causal, d_col + c_t[h:h + 1, :], NEG)) * _bdot_nt(q, k)
        r1_ = jnp.dot(a.astype(BF16), v_aug, preferred_element_type=F32)
        r2_ = jnp.dot(q, c_prev.astype(BF16), preferred_element_type=F32)
        num = r1_[:, 0:DH_C] + wi_col * r2_[:, 0:DH_C]
        den = r1_[:, DH_C:] + wi_col * r2_[:, DH_C:]
        hh = num / jnp.maximum(jnp.abs(den), em_col)

        wc_row = jnp.concatenate([wc[h:h + 1, :], wc[h:h + 1, :]], axis=1)
        caug_s[h] = wc_row * c_prev + _bdot_tn(k * ws_col, v_aug)

        ms = jnp.sum(hh * hh, axis=-1, keepdims=True) * (1.0 / DH_C)
        hn = hh * lax.rsqrt(ms + EPS) * hn_ref[:, cols]
        z = z_all[:, cols]
        gate = z / ((1.0 + jnp.exp(-o_all[:, cols])) * (1.0 + jnp.exp(-z)))
        y_ref[:, cols] = (hn[0:t_rows, :] * gate).astype(BF16)
    return new_carry


def _mlstm_kernel(*refs, t_rows, nb):
    (qk_b, v_b, o_b, z_b, gt_b, cbuf_b, c0_b, n0_b, m0_b, cw_ref, cb_ref, gb_ref, hn_ref,
     y_b, conv_out_b, c_out_b, n_out_b, m_out_b, xp_b, vp_b, gp_b, caug_b, m_b) = refs
    ci = pl.program_id(1)
    seqs = range(nb)

    @pl.when(ci == 0)
    def _():
        for bi in seqs:
            _mlstm_init(cbuf_b.at[bi], c0_b.at[bi], n0_b.at[bi], m0_b.at[bi],
                        xp_b.at[bi], vp_b.at[bi], gp_b.at[bi], caug_b.at[bi], m_b.at[bi])

    carries = [_mlstm_chunk(qk_b.at[bi], v_b.at[bi], o_b.at[bi], z_b.at[bi], gt_b.at[bi],
                            cw_ref, cb_ref, gb_ref, hn_ref, y_b.at[bi],
                            xp_b.at[bi], vp_b.at[bi], gp_b.at[bi], caug_b.at[bi], m_b.at[bi], t_rows=t_rows)
               for bi in seqs]

    @pl.when(ci == pl.num_programs(1) - 1)
    def _():
        for bi in seqs:
            _mlstm_final(carries[bi], conv_out_b.at[bi], c_out_b.at[bi], n_out_b.at[bi], m_out_b.at[bi],
                         caug_b.at[bi], m_b.at[bi])


def _mlstm(proj, gates, conv_buf, c0, n0, m0, state_layer, layer, conv_w, conv_b3, gate_bias3, hn_g3, t_rows):
    b, seq, _ = proj.shape
    nchunks = seq // t_rows
    hp = SUBLANES
    nb = MLSTM_SEQS_PER_STEP
    assert b % nb == 0
    col = lambda base, width: pl.BlockSpec((nb, t_rows, width), lambda bi, ci: (bi, ci, base))
    param = lambda shape: pl.BlockSpec((None,) + shape, lambda bi, ci: (layer, 0, 0))
    per_b3 = lambda shape: pl.BlockSpec((nb,) + shape, lambda bi, ci: (bi, 0, 0))
    per_b4 = lambda shape: pl.BlockSpec((nb,) + shape, lambda bi, ci: (bi, 0, 0, 0))
    state = lambda shape: pl.BlockSpec((None, nb) + shape,
                                       lambda bi, ci: (state_layer, bi) + (0,) * len(shape))
    return pl.pallas_call(
        functools.partial(_mlstm_kernel, t_rows=t_rows, nb=nb),
        out_shape=(jax.ShapeDtypeStruct((b, seq, D_C), BF16),
                   jax.ShapeDtypeStruct((b, CONV_W - 1, 2 * D_C), F32),
                   jax.ShapeDtypeStruct((b, H_C, DH_C, DH_C), F32),
                   jax.ShapeDtypeStruct((b, hp, DH_C), F32),
                   jax.ShapeDtypeStruct((b, hp, LANES), F32)),
        grid=(b // nb, nchunks),
        in_specs=[col(COL_QK, 2 * D_C), col(COL_VC, D_C), col(COL_OC, D_C), col(COL_ZC, D_C),
                  pl.BlockSpec((nb, t_rows, LANES), lambda bi, ci: (bi, ci, 0)),
                  state((CONV_W - 1, 2 * D_C)), state((H_C, DH_C, DH_C)), state((H_C, DH_C)),
                  state((hp, LANES)),
                  param((CONV_W, 2 * D_C)), param((1, 2 * D_C)), param((1, LANES)), param((1, D_C))],
        out_specs=(pl.BlockSpec((nb, t_rows, D_C), lambda bi, ci: (bi, ci, 0)),
                   per_b3((CONV_W - 1, 2 * D_C)), per_b4((H_C, DH_C, DH_C)), per_b3((hp, DH_C)),
                   per_b3((hp, LANES))),
        scratch_shapes=[pltpu.VMEM((nb, MLSTM_ROW0 + CHUNK_C, 2 * D_C), F32),
                        pltpu.VMEM((nb, CHUNK_C, D_C), F32),
                        pltpu.VMEM((nb, CHUNK_C, LANES), F32),
                        pltpu.VMEM((nb, H_C, DH_C, 2 * DH_C), F32),
                        pltpu.VMEM((nb, hp, LANES), F32)],
        compiler_params=_cparams(("arbitrary", "arbitrary")),
        name="mlstm",
    )(proj, proj, proj, proj, gates, conv_buf, c0, n0, m0, conv_w, conv_b3, gate_bias3, hn_g3)


def _gate_lanes(i_part, f_part):
    lead = i_part.shape[:-1]
    gap = jnp.zeros(lead + (SUBLANES - H_C,), i_part.dtype)
    tail = jnp.zeros(lead + (LANES - 2 * SUBLANES,), i_part.dtype)
    return jnp.concatenate([i_part, gap, f_part, gap, tail], axis=-1)


def _pad_heads(a):
    pad = [(0, 0)] * a.ndim
    pad[1] = (0, SUBLANES - a.shape[1])
    return jnp.pad(a, pad)


def _mixers(proj, gates, lw, layer, attn_fn, sgu_fn, conv_buf, c0, n0, m0, state_layer, t_rows):
    ya, nk, nv = attn_fn(proj)
    sgu_out = sgu_fn(proj)
    yc, nconv, c_new, n_new, m_new = _mlstm(proj, gates, conv_buf, c0, n0, m0, state_layer, layer,
                                            lw["conv_w"], lw["conv_b"], lw["gate_bias"], lw["hn_g"], t_rows)
    return ya, sgu_out, yc, nk, nv, nconv, c_new, n_new[:, :H_C], m_new[:, :H_C, 0]


def kernel(x_prompt, x_sample, c_prompt, c_sample, cache_k_win, cache_v_win, state_conv, state_C, state_n, state_m, rel_bias, norm_g, ada_w, ada_b, w_in, qn_g, kn_g, sgu_g, sgu_w, sgu_b, conv_w, conv_b, f_bias, i_bias, hn_g, w_out):
    depth = w_in.shape[0]
    bp, seq, _ = x_prompt.shape
    bd, t_new, _ = x_sample.shape
    n_past = cache_k_win.shape[2]
    assert t_new == SUBLANES and n_past % LANES == 0 and seq % 1024 == 0

    w_in_t = _unpack_w_in(w_in)
    w_gate = _gate_lanes(w_in[:, :, D_MAIN:D_MAIN + H_C], w_in[:, :, D_MAIN + H_C:])
    w_out_bf = w_out.astype(BF16)

    rel_t = rel_bias.T
    bias_p = _expand_bias(rel_t, jnp.asarray(_prompt_bucket_table()), 64, LOG2E)
    bias_p = bias_p.reshape(H_A, N_PAT, 2, BLK_A, 2 * BLK_A)
    bias_s = _expand_bias(rel_t, jnp.asarray(_sample_bucket_table(n_past, t_new)), SUBLANES, 1.0)
    bias_s = bias_s.reshape(H_A, N_PAT, t_new, n_past + LANES)

    n_c = bp + bd
    c_all = jnp.pad(jnp.concatenate([c_prompt, c_sample], axis=0), ((0, 2 * SUBLANES - n_c), (0, 0)))
    mod = _ada_mod(c_all, ada_w, ada_b)

    cache_kt = jnp.transpose(cache_k_win, (0, 1, 3, 4, 2))
    cache_vt = jnp.transpose(cache_v_win, (0, 1, 3, 4, 2))
    gq_t = jnp.tile(qn_g, (1, H_A))[:, None, :]
    gk_t = jnp.tile(kn_g, (1, H_A))[:, None, :]
    lw = {"conv_w": conv_w, "conv_b": conv_b[:, None, :], "gate_bias": _gate_lanes(i_bias, f_bias)[:, None, :],
          "hn_g": hn_g[:, None, :]}
    sg3 = sgu_g[:, None, :]
    sgu_bt = jnp.transpose(sgu_b, (0, 2, 1))
    zeros_conv = jnp.zeros((1, bp, CONV_W - 1, 2 * D_C), F32)
    zeros_c = jnp.zeros((1, bp, H_C, DH_C, DH_C), F32)
    zeros_n = jnp.zeros((1, bp, H_C, DH_C), F32)
    zeros_m = jnp.zeros((1, bp, SUBLANES, LANES), F32)
    m0_all = jnp.broadcast_to(_pad_heads(state_m.reshape(depth * bd, H_C))[:, :, None],
                              (depth * bd, SUBLANES, LANES)).reshape(depth, bd, SUBLANES, LANES)
    keep = min(WIN_MAX, seq)
    pk_all = jnp.zeros((depth, bp, D_A, keep), F32)
    pv_all = jnp.zeros((depth, bp, D_A, keep), F32)
    w8 = sgu_w[:, :, :t_new, :t_new] * jnp.tril(jnp.ones((t_new, t_new), F32))
    wexp = jnp.repeat(jnp.transpose(w8, (0, 3, 2, 1)), C_B, axis=3)
    bexp = jnp.repeat(jnp.transpose(sgu_b[:, :, :t_new], (0, 2, 1)), C_B, axis=2)

    xp = x_prompt.reshape(bp * seq, D_MODEL)
    xs = x_sample.reshape(bd * t_new, D_MODEL)
    outs = [[] for _ in range(13)]
    mod4 = jnp.transpose(mod.reshape(depth, mod.shape[1], 3, D_MODEL), (0, 2, 1, 3))
    mod_p = mod4[:, :, :bp, None, :]
    mod_s = jnp.repeat(mod4[:, :, bp:n_c], t_new, axis=2)[:, :, None]
    norm_g3 = norm_g[:, None, :]
    hp = _norm(xp, mod_p, norm_g3, 0, 512)
    hs = _norm(xs, mod_s, norm_g3, 0, bd * t_new)
    for l in range(depth):
        last = l == depth - 1

        proj, gates = _inproj(hp, w_in_t, w_gate, l, 2048)
        proj = proj.reshape(bp, seq, D_MAIN)
        gates = gates.reshape(bp, seq, LANES)
        ya, yb, yc, pk_all, pv_all, ncv, nc_, nn_, nm = _mixers(
            proj, gates, lw, l,
            lambda pr: _attn_prompt(pr, gq_t, gk_t, bias_p, pk_all, pv_all, l),
            lambda pr: _sgu_prompt(pr, sg3, sgu_w, sgu_bt, l),
            zeros_conv, zeros_c, zeros_n, zeros_m, 0, CHUNK_C)
        xp, hp = _outproj(ya.reshape(bp * seq, D_A), yb.reshape(bp * seq, D_B), yc.reshape(bp * seq, D_C),
                          xp, mod_p, norm_g3, w_out_bf, l, 512, not last)
        for i, a in enumerate((ncv, nc_, nn_, nm)):
            outs[2 + i].append(a)

        proj_s, gates_s = _inproj(hs, w_in_t, w_gate, l, bd * t_new)
        proj_s = proj_s.reshape(bd, t_new, D_MAIN)
        gates_s = gates_s.reshape(bd, t_new, LANES)
        ya, sgu_out, yc, nk, nv, ncv, nc_, nn_, nm = _mixers(
            proj_s, gates_s, lw, l,
            lambda pr: _attn_sample(pr, cache_kt, cache_vt, gq_t, gk_t, bias_s, l),
            lambda pr: _sgu_sample(pr, sg3, wexp, bexp, l),
            state_conv, state_C, state_n, m0_all, l, t_new)
        yb, vn = sgu_out
        xs, hs = _outproj(ya.reshape(bd * t_new, D_A), yb.reshape(bd * t_new, D_B), yc.reshape(bd * t_new, D_C),
                          xs, mod_s, norm_g3, w_out_bf, l, bd * t_new, not last)
        for i, a in enumerate((nk.reshape(bd, t_new, H_A, HEAD_DIM_A), nv.reshape(bd, t_new, H_A, HEAD_DIM_A),
                               vn, ncv, nc_, nn_, nm)):
            outs[6 + i].append(a)

    stacked = [jnp.stack(o) for o in outs[2:]]
    p_k, p_v = (jnp.transpose(a.reshape(depth, bp, H_A, HEAD_DIM_A, keep), (0, 1, 4, 2, 3)) for a in (pk_all, pv_all))
    return (xp.reshape(bp, seq, D_MODEL), xs.reshape(bd, t_new, D_MODEL), p_k, p_v, *stacked)
```

```python
import functools
import math

import numpy as np
import jax
import jax.numpy as jnp
from jax import lax
from jax.experimental import pallas as pl
from jax.experimental.pallas import tpu as pltpu

F32 = jnp.float32
BF16 = jnp.bfloat16

D_MODEL = 2048
HEAD_DIM_A = 64
D_A = 768
H_A = 12
D_B = 512
G_B = 4
C_B = 128
CHUNK_B = 128
D_C = 768
DH_C = 128
H_C = 6
CHUNK_C = 128
CONV_W = 4
PATTERNS = ((128, 1), (512, 4), (2048, 16))
N_PAT = len(PATTERNS)
WIN_MAX = 2048
BLK_A = 128
N_BUCKETS = 32
MAX_DIST = 2048
EPS = 1e-6
D_MAIN = 4 * D_A + 3 * D_B + 2 * D_C + 3 * D_C
D_IN = D_MAIN + 2 * H_C

LANES = 128
SUBLANES = 8
VMEM_LIMIT = 56 * 1024 * 1024

NEG = -1e30
LOG2E = 1.4426950408889634
PAD_A = BLK_A * PATTERNS[-1][1]
HPAIR = LANES // HEAD_DIM_A

COL_Q, COL_K, COL_V, COL_Z = 0, D_A // LANES, 2 * D_A // LANES, 3 * D_A // LANES
COL_UB, COL_VB, COL_ZB = 4 * D_A // D_B, 4 * D_A // D_B + 1, 4 * D_A // D_B + 2
COL_QK = (4 * D_A + 3 * D_B) // (2 * D_C)
COL_VC = (4 * D_A + 3 * D_B + 2 * D_C) // D_C
COL_OC, COL_ZC = COL_VC + 1, COL_VC + 2


def _cparams(sem):
    return pltpu.CompilerParams(dimension_semantics=sem, vmem_limit_bytes=VMEM_LIMIT)


def _silu(x):
    return x * jax.nn.sigmoid(x)


def _bdot(a, b):
    return jnp.dot(a.astype(BF16), b.astype(BF16), preferred_element_type=F32)


def _bdot_nt(a, b):
    return lax.dot_general(a.astype(BF16), b.astype(BF16), (((1,), (1,)), ((), ())),
                           preferred_element_type=F32)


def _bdot_tn(a, b):
    return lax.dot_general(a.astype(BF16), b.astype(BF16), (((0,), (0,)), ((), ())),
                           preferred_element_type=F32)


def _bucket_np(dist):
    max_exact = N_BUCKETS // 2
    df = np.maximum(dist, 1).astype(np.float32)
    large = max_exact + (np.log(df / np.float32(max_exact)) / np.float32(math.log(MAX_DIST / max_exact))
                         * np.float32(N_BUCKETS - max_exact)).astype(np.int32)
    return np.where(dist < max_exact, dist, np.minimum(large, N_BUCKETS - 1)).astype(np.int32)


def _prompt_bucket_table():
    qi = np.arange(BLK_A)[:, None]
    ki = np.arange(2 * BLK_A)[None, :]
    j = qi + BLK_A - ki
    out = []
    for win, dil in PATTERNS:
        n_back = win // dil
        band = (j >= 0) & (j <= n_back)
        b = _bucket_np(np.clip(j, 0, n_back) * dil)
        out.append(np.where(band, b, -1))
        out.append(np.where(band & (ki >= BLK_A), b, -1))
    return np.stack(out).reshape(N_PAT * 2 * BLK_A, 2 * BLK_A).astype(np.int32)


def _sample_bucket_table(n_past, t_new):
    c = np.arange(n_past + LANES)[None, :]
    t = np.arange(t_new)[:, None]
    delta = n_past + t - c
    out = []
    for win, dil in PATTERNS:
        valid = (c < n_past + t_new) & (delta >= 0) & (delta % dil == 0) & (delta // dil <= win // dil)
        out.append(np.where(valid, _bucket_np(np.maximum(delta, 0)), -1))
    return np.stack(out).reshape(N_PAT * t_new, n_past + LANES).astype(np.int32)


def _bias_kernel(rb_ref, idx_ref, out_ref, *, scale, row_chunk):
    h = pl.program_id(0)

    def chunk(c, carry):
        rows = pl.ds(pl.multiple_of(c * row_chunk, row_chunk), row_chunk)
        idx = idx_ref[rows, :]
        out = jnp.full(idx.shape, NEG, F32)
        for b in range(N_BUCKETS):
            out = jnp.where(idx == b, rb_ref[h, b] * scale, out)
        out_ref[rows, :] = out
        return carry
    lax.fori_loop(0, idx_ref.shape[0] // row_chunk, chunk, 0)


def _expand_bias(rel_bias_t, idx, row_chunk, scale):
    rows, cols = idx.shape
    return pl.pallas_call(
        functools.partial(_bias_kernel, scale=scale, row_chunk=row_chunk),
        out_shape=jax.ShapeDtypeStruct((H_A, rows, cols), F32),
        grid=(H_A,),
        in_specs=[pl.BlockSpec(memory_space=pltpu.SMEM),
                  pl.BlockSpec((rows, cols), lambda h: (0, 0))],
        out_specs=pl.BlockSpec((None, rows, cols), lambda h: (h, 0, 0)),
        compiler_params=_cparams(("arbitrary",)),
        name="bias_expand",
    )(rel_bias_t, idx)


def _ada_kernel(c_ref, w_ref, b_ref, o_ref):
    c = c_ref[...]
    a = _silu(c)
    w = w_ref[...]
    a_hi = a.astype(BF16)
    a_lo = (a - a_hi.astype(F32)).astype(BF16)
    w_hi = w.astype(BF16)
    w_lo = (w - w_hi.astype(F32)).astype(BF16)
    acc = jnp.dot(a_hi, w_hi, preferred_element_type=F32)
    acc += jnp.dot(a_hi, w_lo, preferred_element_type=F32)
    acc += jnp.dot(a_lo, w_hi, preferred_element_type=F32)
    o_ref[...] = acc + b_ref[...]


def _ada_mod(c_all, ada_w, ada_b):
    depth = ada_w.shape[0]
    rows = c_all.shape[0]
    tn = 768
    n = 3 * D_MODEL
    return pl.pallas_call(
        _ada_kernel,
        out_shape=jax.ShapeDtypeStruct((depth, rows, n), F32),
        grid=(depth, n // tn),
        in_specs=[pl.BlockSpec((rows, D_MODEL), lambda l, j: (0, 0)),
                  pl.BlockSpec((None, D_MODEL, tn), lambda l, j: (l, 0, j)),
                  pl.BlockSpec((None, 1, tn), lambda l, j: (l, 0, j))],
        out_specs=pl.BlockSpec((None, rows, tn), lambda l, j: (l, 0, j)),
        compiler_params=_cparams(("arbitrary", "arbitrary")),
        name="ada_mod",
    )(c_all, ada_w, ada_b.reshape(depth, 1, n))


def _norm_mod(x, g, sc, sh):
    ms = jnp.sum(x * x, axis=-1, keepdims=True) * (1.0 / D_MODEL)
    return ((x * lax.rsqrt(ms + EPS) * g) * (1.0 + sc) + sh).astype(BF16)


def _norm_kernel(x_ref, sc_ref, sh_ref, g_ref, h_ref):
    h_ref[...] = _norm_mod(x_ref[...], g_ref[...], sc_ref[...], sh_ref[...])


MOD_SHIFT, MOD_SCALE, MOD_GATE = 0, 1, 2


def _mod_spec(mod5, layer, which, tiles_per_group):
    return pl.BlockSpec((None, None, None, mod5.shape[3], D_MODEL),
                        lambda i: (layer, which, i // tiles_per_group, 0, 0))


def _norm(x2d, mod5, norm_g3, layer, tm):
    m = x2d.shape[0]
    tiles_per_group = m // tm // mod5.shape[2]
    return pl.pallas_call(
        _norm_kernel,
        out_shape=jax.ShapeDtypeStruct((m, D_MODEL), BF16),
        grid=(m // tm,),
        in_specs=[pl.BlockSpec((tm, D_MODEL), lambda i: (i, 0)),
                  _mod_spec(mod5, layer, MOD_SCALE, tiles_per_group),
                  _mod_spec(mod5, layer, MOD_SHIFT, tiles_per_group),
                  pl.BlockSpec((None, 1, D_MODEL), lambda i: (layer, 0, 0))],
        out_specs=pl.BlockSpec((tm, D_MODEL), lambda i: (i, 0)),
        compiler_params=_cparams(("arbitrary",)),
        name="norm",
    )(x2d, mod5, mod5, norm_g3)


INPROJ_TN = 768
INPROJ_TN_SAMPLE = 2816


def _inproj_kernel(h_ref, w_ref, wg_ref, proj_ref, gates_ref):
    @pl.when(pl.program_id(1) == 0)
    def _():
        gates_ref[...] = jnp.dot(h_ref[...], wg_ref[...].astype(BF16), preferred_element_type=F32)

    proj_ref[...] = lax.dot_general(h_ref[...], w_ref[...], (((1,), (1,)), ((), ())),
                                    preferred_element_type=F32)


UNPACK_PITCH = 72


def _unpack_kernel(x_hbm, o_ref, buf_a, buf_b, sem, *, depth, k_tiles, tn):
    i = pl.program_id(0)
    rows_per_col = k_tiles * depth

    def copy(tile, n, buf, slot):
        return pltpu.make_async_copy(x_hbm.at[tile * tn + n], buf.at[pl.ds(n * UNPACK_PITCH, rows_per_col)],
                                     sem.at[slot])

    def start_all(tile, buf, slot):
        def body(n, carry):
            copy(tile, n, buf, slot).start()
            return carry
        lax.fori_loop(0, tn, body, 0, unroll=8)

    def wait_all(tile, buf, slot):
        def body(n, carry):
            copy(tile, n, buf, slot).wait()
            return carry
        lax.fori_loop(0, tn, body, 0, unroll=8)

    def convert(buf, half):
        for l in range(depth):
            for kt in range(k_tiles):
                rows = pl.ds(kt * depth + l, tn, stride=UNPACK_PITCH)
                o_ref[l, half * tn:(half + 1) * tn, kt * LANES:(kt + 1) * LANES] = buf[rows, :].astype(BF16)

    @pl.when(i == 0)
    def _():
        start_all(0, buf_a, 0)

    start_all(2 * i + 1, buf_b, 1)
    wait_all(2 * i, buf_a, 0)
    convert(buf_a, 0)

    @pl.when(i + 1 < pl.num_programs(0))
    def _():
        start_all(2 * i + 2, buf_a, 0)

    wait_all(2 * i + 1, buf_b, 1)
    convert(buf_b, 1)


def _unpack_w_in(w_in):
    depth, d, d_in = w_in.shape
    k_tiles = d // LANES
    tn = LANES
    assert D_MAIN % (2 * tn) == 0 and k_tiles * depth <= UNPACK_PITCH
    cols = w_in.reshape(depth, k_tiles, LANES, d_in).transpose(3, 1, 0, 2).reshape(d_in, k_tiles * depth, LANES)
    return pl.pallas_call(
        functools.partial(_unpack_kernel, depth=depth, k_tiles=k_tiles, tn=tn),
        out_shape=jax.ShapeDtypeStruct((depth, D_MAIN, d), BF16),
        grid=(D_MAIN // (2 * tn),),
        in_specs=[pl.BlockSpec(memory_space=pl.ANY)],
        out_specs=pl.BlockSpec((depth, 2 * tn, d), lambda i: (0, i, 0)),
        scratch_shapes=[pltpu.VMEM((tn * UNPACK_PITCH, LANES), F32),
                        pltpu.VMEM((tn * UNPACK_PITCH, LANES), F32),
                        pltpu.SemaphoreType.DMA((2,))],
        compiler_params=_cparams(("arbitrary",)),
        name="unpack_w_in",
    )(cols)


def _inproj(h2d, w_in_t, w_gate, layer, tm, tn=INPROJ_TN):
    m = h2d.shape[0]
    return pl.pallas_call(
        _inproj_kernel,
        out_shape=(jax.ShapeDtypeStruct((m, D_MAIN), F32), jax.ShapeDtypeStruct((m, LANES), F32)),
        grid=(m // tm, D_MAIN // tn),
        in_specs=[pl.BlockSpec((tm, D_MODEL), lambda i, j: (i, 0)),
                  pl.BlockSpec((None, tn, D_MODEL), lambda i, j: (layer, j, 0)),
                  pl.BlockSpec((None, D_MODEL, LANES), lambda i, j: (layer, 0, 0))],
        out_specs=(pl.BlockSpec((tm, tn), lambda i, j: (i, j)),
                   pl.BlockSpec((tm, LANES), lambda i, j: (i, 0))),
        compiler_params=_cparams(("arbitrary", "arbitrary")),
        name="inproj",
    )(h2d, w_in_t, w_gate)


def _outproj_kernel(ya_ref, yb_ref, yc_ref, x_ref, gate_ref, w_ref, *rest, emit_next):
    y = jnp.dot(ya_ref[...], w_ref[0:D_A, :], preferred_element_type=F32)
    y += jnp.dot(yb_ref[...], w_ref[D_A:D_A + D_B, :], preferred_element_type=F32)
    y += jnp.dot(yc_ref[...], w_ref[D_A + D_B:, :], preferred_element_type=F32)
    x_new = x_ref[...] + gate_ref[...] * y
    if emit_next:
        sc_ref, sh_ref, g_ref, o_ref, h_ref = rest
        h_ref[...] = _norm_mod(x_new, g_ref[...], sc_ref[...], sh_ref[...])
    else:
        (o_ref,) = rest
    o_ref[...] = x_new


def _outproj(ya, yb, yc, x2d, mod5, norm_g3, w_out_bf, layer, tm, emit_next):
    m = x2d.shape[0]
    tiles_per_group = m // tm // mod5.shape[2]
    row = lambda width: pl.BlockSpec((tm, width), lambda i: (i, 0))
    in_specs = [row(D_A), row(D_B), row(D_C), row(D_MODEL),
                _mod_spec(mod5, layer, MOD_GATE, tiles_per_group),
                pl.BlockSpec((None, D_MODEL, D_MODEL), lambda i: (layer, 0, 0))]
    args = [ya, yb, yc, x2d, mod5, w_out_bf]
    out_shape = [jax.ShapeDtypeStruct((m, D_MODEL), F32)]
    out_specs = [row(D_MODEL)]
    if emit_next:
        in_specs += [_mod_spec(mod5, layer + 1, MOD_SCALE, tiles_per_group),
                     _mod_spec(mod5, layer + 1, MOD_SHIFT, tiles_per_group),
                     pl.BlockSpec((None, 1, D_MODEL), lambda i: (layer + 1, 0, 0))]
        args += [mod5, mod5, norm_g3]
        out_shape.append(jax.ShapeDtypeStruct((m, D_MODEL), BF16))
        out_specs.append(row(D_MODEL))
    out = pl.pallas_call(
        functools.partial(_outproj_kernel, emit_next=emit_next),
        out_shape=tuple(out_shape),
        grid=(m // tm,),
        in_specs=in_specs,
        out_specs=tuple(out_specs),
        compiler_params=_cparams(("arbitrary",)),
        name="outproj",
    )(*args)
    return out if emit_next else (out[0], None)


def _head_mean_matrix():
    row = lax.broadcasted_iota(jnp.int32, (LANES, LANES), 0)
    col = lax.broadcasted_iota(jnp.int32, (LANES, LANES), 1)
    same_head = (row < HEAD_DIM_A) == (col < HEAD_DIM_A)
    return jnp.where(same_head, 1.0 / HEAD_DIM_A, 0.0).astype(BF16)


def _head_norm(x, g, mean_mat):
    x2 = x * x
    hi = x2.astype(BF16)
    lo = (x2 - hi.astype(F32)).astype(BF16)
    ms = (jnp.dot(hi, mean_mat, preferred_element_type=F32)
          + jnp.dot(lo, mean_mat, preferred_element_type=F32))
    return x * lax.rsqrt(ms + EPS) * g


def _attn_prompt_kernel(q_ref, k_ref, v_ref, z_ref, gq_ref, gk_ref, bias_ref, pk_all_hbm, pv_all_hbm,
                        y_ref, pk_ref, pv_ref,
                        qn_s, kp_s, vp_s, ones_s, o_s, l_s, m_s, *, seq, keep, unroll):
    norm_rows = 512
    mean_mat = _head_mean_matrix()

    kp_s[0:PAD_A, :] = jnp.zeros((PAD_A, LANES), F32)
    vp_s[0:PAD_A, :] = jnp.zeros((PAD_A, LANES), F32)
    ones_s[...] = jnp.ones(ones_s.shape, BF16)

    def norm_chunk(c, carry):
        r0 = pl.multiple_of(c * norm_rows, norm_rows)
        rows = pl.ds(r0, norm_rows)
        qn_s[rows, :] = _head_norm(q_ref[rows, :], gq_ref[...], mean_mat) * (HEAD_DIM_A ** -0.5 * LOG2E)
        kp_s[pl.ds(PAD_A + r0, norm_rows), :] = _head_norm(k_ref[rows, :], gk_ref[...], mean_mat)
        vp_s[pl.ds(PAD_A + r0, norm_rows), :] = v_ref[rows, :]
        return carry
    lax.fori_loop(0, seq // norm_rows, norm_chunk, 0, unroll=True)

    for c in range(keep // LANES):
        src = slice(PAD_A + seq - keep + c * LANES, PAD_A + seq - keep + (c + 1) * LANES)
        pk_ref[:, c * LANES:(c + 1) * LANES] = kp_s[src, :].T
        pv_ref[:, c * LANES:(c + 1) * LANES] = vp_s[src, :].T

    left_b = lax.broadcasted_iota(jnp.int32, (BLK_A, LANES), 1) < HEAD_DIM_A
    left_k = lax.broadcasted_iota(jnp.int32, (2 * BLK_A, LANES), 1) < HEAD_DIM_A

    for p, (win, dil) in enumerate(PATTERNS):
        blocks_per_residue = seq // (dil * BLK_A)

        def block(i, carry, p=p, dil=dil, blocks_per_residue=blocks_per_residue):
            r = i // blocks_per_residue
            n = i % blocks_per_residue
            q_start = r + n * (BLK_A * dil)
            k_start = PAD_A + q_start - BLK_A * dil
            if dil == 1:
                q_rows = pl.ds(pl.multiple_of(q_start, BLK_A), BLK_A)
                k_rows = pl.ds(pl.multiple_of(k_start, BLK_A), 2 * BLK_A)
            else:
                q_rows = pl.ds(q_start, BLK_A, stride=dil)
                k_rows = pl.ds(k_start, 2 * BLK_A, stride=dil)
            first = jnp.where(n == 0, 1, 0)
            q = qn_s[q_rows, :]
            k = kp_s[k_rows, :].astype(BF16)
            v = vp_s[k_rows, :].astype(BF16)
            rhs = jnp.concatenate([v, ones_s[...]], axis=1)
            q2 = jnp.concatenate([jnp.where(left_b, q, 0.0), jnp.where(left_b, 0.0, q)], axis=0)
            s2 = _bdot_nt(q2, k)
            es, ms = [], []
            for h in range(HPAIR):
                s = s2[h * BLK_A:(h + 1) * BLK_A, :] + bias_ref[h, p, first]
                mh = jnp.max(s, axis=-1, keepdims=True)
                es.append(jnp.exp2(s - mh).astype(BF16))
                ms.append(mh)
            res = jnp.dot(jnp.concatenate(es, axis=0), rhs, preferred_element_type=F32)
            o_s[p, q_rows, :] = jnp.where(left_b, res[0:BLK_A, 0:LANES], res[BLK_A:, 0:LANES])
            l_s[p, q_rows, :] = jnp.where(left_b, res[0:BLK_A, LANES:], res[BLK_A:, LANES:])
            m_s[p, q_rows, :] = jnp.where(left_b, ms[0], ms[1])
            return carry
        lax.fori_loop(0, seq // BLK_A, block, 0, unroll=unroll)

    def out_chunk(c, carry):
        rows = pl.ds(pl.multiple_of(c * norm_rows, norm_rows), norm_rows)
        m = [m_s[p, rows, :] for p in range(N_PAT)]
        top = functools.reduce(jnp.maximum, m)
        w = [jnp.exp2(x - top) for x in m]
        num = functools.reduce(lambda a, b: a + b, [w[p] * o_s[p, rows, :] for p in range(N_PAT)])
        den = functools.reduce(lambda a, b: a + b, [w[p] * l_s[p, rows, :] for p in range(N_PAT)])
        y_ref[rows, :] = (num / den * _silu(z_ref[rows, :])).astype(BF16)
        return carry
    lax.fori_loop(0, seq // norm_rows, out_chunk, 0)


def _attn_prompt(proj, gq_t, gk_t, bias_p, pk_all, pv_all, layer):
    b, seq, _ = proj.shape
    keep = pk_all.shape[-1]
    assert seq % (BLK_A * PATTERNS[-1][1]) == 0 and keep == min(WIN_MAX, seq)
    col = lambda base: pl.BlockSpec((None, seq, LANES), lambda bi, hp: (bi, 0, base + hp))
    vec = pl.BlockSpec((None, 1, LANES), lambda bi, hp: (layer, 0, hp))
    window = pl.BlockSpec((None, None, LANES, keep), lambda bi, hp: (layer, bi, hp, 0))
    untouched = pl.BlockSpec(memory_space=pl.ANY)
    return pl.pallas_call(
        functools.partial(_attn_prompt_kernel, seq=seq, keep=keep, unroll=seq // BLK_A),
        out_shape=(jax.ShapeDtypeStruct((b, seq, D_A), BF16),
                   jax.ShapeDtypeStruct(pk_all.shape, F32),
                   jax.ShapeDtypeStruct(pv_all.shape, F32)),
        grid=(b, H_A // HPAIR),
        in_specs=[col(COL_Q), col(COL_K), col(COL_V), col(COL_Z), vec, vec,
                  pl.BlockSpec((HPAIR, N_PAT, 2, BLK_A, 2 * BLK_A), lambda bi, hp: (hp, 0, 0, 0, 0)),
                  untouched, untouched],
        out_specs=(pl.BlockSpec((None, seq, LANES), lambda bi, hp: (bi, 0, hp)), window, window),
        input_output_aliases={7: 1, 8: 2},
        scratch_shapes=[pltpu.VMEM((seq, LANES), F32),
                        pltpu.VMEM((PAD_A + seq, LANES), F32),
                        pltpu.VMEM((PAD_A + seq, LANES), F32),
                        pltpu.VMEM((2 * BLK_A, LANES), BF16),
                        pltpu.VMEM((N_PAT, seq, LANES), F32),
                        pltpu.VMEM((N_PAT, seq, LANES), F32),
                        pltpu.VMEM((N_PAT, seq, LANES), F32)],
        compiler_params=_cparams(("arbitrary", "arbitrary")),
        name="attn_prompt",
    )(proj, proj, proj, proj, gq_t, gk_t, bias_p, pk_all, pv_all)


def _attn_sample_kernel(q_ref, k_ref, v_ref, z_ref, ck_ref, cv_ref, gq_ref, gk_ref, bias_ref,
                        y_ref, nk_ref, nv_ref, *, n_past, t_new):
    q_all = q_ref[...]
    k_all = k_ref[...]
    v_all = v_ref[...]
    z_all = z_ref[...]
    nv_ref[...] = v_all
    pad_rows = jnp.zeros((LANES - t_new, HEAD_DIM_A), F32)

    def norm(x, g):
        ms = jnp.sum(x * x, axis=-1, keepdims=True) * (1.0 / HEAD_DIM_A)
        return x * lax.rsqrt(ms + EPS) * g

    for h in range(H_A):
        cols = slice(h * HEAD_DIM_A, (h + 1) * HEAD_DIM_A)
        qn = (norm(q_all[:, cols], gq_ref[:, cols]) * (HEAD_DIM_A ** -0.5)).astype(BF16)
        kn = norm(k_all[:, cols], gk_ref[:, cols])
        nk_ref[:, cols] = kn
        k_new = jnp.concatenate([kn, pad_rows], axis=0).astype(BF16)
        v_new = jnp.concatenate([v_all[:, cols], pad_rows], axis=0).astype(BF16)
        k_t = ck_ref[h].astype(BF16)
        v_t = cv_ref[h].astype(BF16)
        s_c = jnp.dot(qn, k_t, preferred_element_type=F32)
        s_n = _bdot_nt(qn, k_new)
        sc = [s_c + bias_ref[h, p, :, 0:n_past] for p in range(N_PAT)]
        sn = [s_n + bias_ref[h, p, :, n_past:] for p in range(N_PAT)]
        m = functools.reduce(jnp.maximum, [jnp.max(x, axis=-1, keepdims=True) for x in sc + sn])
        e_c = functools.reduce(lambda a, b: a + b, [jnp.exp(x - m) for x in sc])
        e_n = functools.reduce(lambda a, b: a + b, [jnp.exp(x - m) for x in sn])
        den = jnp.sum(e_c, axis=-1, keepdims=True) + jnp.sum(e_n, axis=-1, keepdims=True)
        o = _bdot_nt(e_c, v_t) + jnp.dot(e_n.astype(BF16), v_new, preferred_element_type=F32)
        y_ref[:, cols] = (o / den * _silu(z_all[:, cols])).astype(BF16)


def _attn_sample(proj_s, cache_kt, cache_vt, gq, gk, bias_s, layer):
    bd, t_new, _ = proj_s.shape
    n_past = cache_kt.shape[-1]
    col = lambda base: pl.BlockSpec((None, t_new, D_A), lambda bi: (bi, 0, base))
    vec = pl.BlockSpec((None, 1, D_A), lambda bi: (layer, 0, 0))
    cache = pl.BlockSpec((None, None, H_A, HEAD_DIM_A, n_past), lambda bi: (layer, bi, 0, 0, 0))
    out = pl.BlockSpec((None, t_new, D_A), lambda bi: (bi, 0, 0))
    return pl.pallas_call(
        functools.partial(_attn_sample_kernel, n_past=n_past, t_new=t_new),
        out_shape=(jax.ShapeDtypeStruct((bd, t_new, D_A), BF16),
                   jax.ShapeDtypeStruct((bd, t_new, D_A), F32),
                   jax.ShapeDtypeStruct((bd, t_new, D_A), F32)),
        grid=(bd,),
        in_specs=[col(0), col(1), col(2), col(3), cache, cache, vec, vec,
                  pl.BlockSpec((H_A, N_PAT, t_new, n_past + LANES), lambda bi: (0, 0, 0, 0))],
        out_specs=(out, out, out),
        compiler_params=_cparams(("arbitrary",)),
        name="attn_sample",
    )(proj_s, proj_s, proj_s, proj_s, cache_kt, cache_vt, gq, gk, bias_s)


def _sgu_prompt_kernel(u_ref, v_ref, z_ref, g_ref, w_ref, bt_ref, y_ref, *, chunks):
    row = lax.broadcasted_iota(jnp.int32, (CHUNK_B, CHUNK_B), 0)
    colm = lax.broadcasted_iota(jnp.int32, (CHUNK_B, CHUNK_B), 1)
    tril = row >= colm
    ws = [jnp.where(tril, w_ref[g], 0.0).astype(BF16) for g in range(G_B)]
    for c in range(chunks):
        rows = slice(c * CHUNK_B, (c + 1) * CHUNK_B)
        v = v_ref[rows, :]
        ms = jnp.sum(v * v, axis=-1, keepdims=True) * (1.0 / D_B)
        vn = (v * lax.rsqrt(ms + EPS) * g_ref[...]).astype(BF16)
        for g in range(G_B):
            cols = slice(g * C_B, (g + 1) * C_B)
            mix = jnp.dot(ws[g], vn[:, cols], preferred_element_type=F32) + bt_ref[:, g:g + 1]
            y_ref[rows, cols] = (u_ref[rows, cols] * mix * _silu(z_ref[rows, cols])).astype(BF16)


def _sgu_prompt(proj, sgu_g3, sgu_w, sgu_bt, layer):
    b, seq, _ = proj.shape
    ts = 1024
    col = lambda base: pl.BlockSpec((None, ts, D_B), lambda bi, i: (bi, i, base))
    return pl.pallas_call(
        functools.partial(_sgu_prompt_kernel, chunks=ts // CHUNK_B),
        out_shape=jax.ShapeDtypeStruct((b, seq, D_B), BF16),
        grid=(b, seq // ts),
        in_specs=[col(COL_UB), col(COL_VB), col(COL_ZB),
                  pl.BlockSpec((None, 1, D_B), lambda bi, i: (layer, 0, 0)),
                  pl.BlockSpec((None, G_B, CHUNK_B, CHUNK_B), lambda bi, i: (layer, 0, 0, 0)),
                  pl.BlockSpec((None, CHUNK_B, G_B), lambda bi, i: (layer, 0, 0))],
        out_specs=pl.BlockSpec((None, ts, D_B), lambda bi, i: (bi, i, 0)),
        compiler_params=_cparams(("arbitrary", "arbitrary")),
        name="sgu_prompt",
    )(proj, proj, proj, sgu_g3, sgu_w, sgu_bt)


def _sgu_sample_kernel(u_ref, v_ref, z_ref, g_ref, wexp_ref, bexp_ref, y_ref, vn_ref, *, t_new):
    v = v_ref[...]
    ms = jnp.sum(v * v, axis=-1, keepdims=True) * (1.0 / D_B)
    vn = v * lax.rsqrt(ms + EPS) * g_ref[...]
    vn_ref[...] = vn
    mix = bexp_ref[...]
    for s in range(t_new):
        mix = mix + wexp_ref[s] * vn[s:s + 1, :]
    y_ref[...] = (u_ref[...] * mix * _silu(z_ref[...])).astype(BF16)


def _sgu_sample(proj_s, sgu_g3, wexp, bexp, layer):
    bd, t_new, _ = proj_s.shape
    col = lambda base: pl.BlockSpec((None, t_new, D_B), lambda bi: (bi, 0, base))
    return pl.pallas_call(
        functools.partial(_sgu_sample_kernel, t_new=t_new),
        out_shape=(jax.ShapeDtypeStruct((bd, t_new, D_B), BF16),
                   jax.ShapeDtypeStruct((bd, t_new, D_B), F32)),
        grid=(bd,),
        in_specs=[col(COL_UB), col(COL_VB), col(COL_ZB),
                  pl.BlockSpec((None, 1, D_B), lambda bi: (layer, 0, 0)),
                  pl.BlockSpec((None, t_new, t_new, D_B), lambda bi: (layer, 0, 0, 0)),
                  pl.BlockSpec((None, t_new, D_B), lambda bi: (layer, 0, 0))],
        out_specs=(pl.BlockSpec((None, t_new, D_B), lambda bi: (bi, 0, 0)),
                   pl.BlockSpec((None, t_new, D_B), lambda bi: (bi, 0, 0))),
        compiler_params=_cparams(("arbitrary",)),
        name="sgu_sample",
    )(proj_s, proj_s, proj_s, sgu_g3, wexp, bexp)


def _log_sigmoid(x):
    return jnp.minimum(x, 0.0) - jnp.log1p(jnp.exp(-jnp.abs(x)))


MLSTM_SEQS_PER_STEP = 2
MLSTM_ROW0 = SUBLANES
MLSTM_CARRY = CONV_W - 1


def _mlstm_init(cbuf_ref, c0_ref, n0_ref, m0_ref, xp_s, vp_s, gp_s, caug_s, m_s):
    xp_s[...] = jnp.zeros(xp_s.shape, F32)
    vp_s[...] = jnp.zeros(vp_s.shape, F32)
    gp_s[...] = jnp.zeros(gp_s.shape, F32)
    xp_s[MLSTM_ROW0 - MLSTM_CARRY:MLSTM_ROW0, :] = cbuf_ref[...]
    n0 = n0_ref[...]
    n0_cols = jnp.concatenate([n0, jnp.zeros((DH_C - n0.shape[0], DH_C), F32)], axis=0).T
    for h in range(H_C):
        caug_s[h, :, 0:DH_C] = c0_ref[h]
        caug_s[h, :, DH_C:] = jnp.broadcast_to(n0_cols[:, h:h + 1], (DH_C, DH_C))
    m_s[...] = m0_ref[...]


def _mlstm_final(new_carry, conv_out_ref, c_out_ref, n_out_ref, m_out_ref, caug_s, m_s):
    conv_out_ref[...] = new_carry
    lane = lax.broadcasted_iota(jnp.int32, (DH_C, LANES), 1)
    n_cols = jnp.zeros((DH_C, LANES), F32)
    for h in range(H_C):
        c_out_ref[h] = caug_s[h, :, 0:DH_C]
        n_cols = jnp.where(lane == h, caug_s[h, :, DH_C:], n_cols)
    n_out_ref[...] = n_cols.T[0:SUBLANES, :]
    m_out_ref[...] = m_s[...]


def _mlstm_chunk(qk_ref, v_ref, o_ref, z_ref, gt_ref, cw_ref, cb_ref, gb_ref, hn_ref, y_ref,
                 xp_s, vp_s, gp_s, caug_s, m_s, *, t_rows):
    L = CHUNK_C
    base = MLSTM_ROW0
    carry_rows = MLSTM_CARRY
    row_l = lax.broadcasted_iota(jnp.int32, (L, LANES), 0)
    lane_l = lax.broadcasted_iota(jnp.int32, (L, LANES), 1)
    zero_rows = jnp.zeros((L - SUBLANES, LANES), F32)

    xp_s[base:base + t_rows, :] = qk_ref[...]
    vp_s[0:t_rows, :] = v_ref[...]
    gp_s[0:t_rows, :] = gt_ref[...]

    live = -(-t_rows // SUBLANES) * SUBLANES
    acc = cb_ref[...] + cw_ref[CONV_W - 1:CONV_W, :] * xp_s[base:base + live, :]
    for j in range(CONV_W - 1):
        off = base - carry_rows + j
        acc = acc + cw_ref[j:j + 1, :] * xp_s[off:off + live, :]
    qk = _silu(acc)
    if live < L:
        qk = jnp.concatenate([qk, jnp.zeros((L - live, qk.shape[1]), F32)], axis=0)
    new_carry = xp_s[base + t_rows - carry_rows:base + t_rows, :]
    xp_s[base - carry_rows:base, :] = new_carry

    gates_t = (gp_s[...] + gb_ref[...]).T
    lane_t = lax.broadcasted_iota(jnp.int32, (SUBLANES, L), 1)
    real_t = lane_t < t_rows
    i_t = jnp.where(real_t, gates_t[0:SUBLANES, :], NEG)
    logf_t = jnp.where(real_t, _log_sigmoid(gates_t[SUBLANES:2 * SUBLANES, :]), 0.0)
    upper = (row_l <= lane_l).astype(BF16)
    f1 = logf_t.astype(BF16)
    r1 = logf_t - f1.astype(F32)
    f2 = r1.astype(BF16)
    f3 = (r1 - f2.astype(F32)).astype(BF16)
    b_t = (jnp.dot(f1, upper, preferred_element_type=F32) + jnp.dot(f2, upper, preferred_element_type=F32)
           + jnp.dot(f3, upper, preferred_element_type=F32))
    m_prev = m_s[...]
    c_t = i_t - b_t
    pm = c_t
    shift = 1
    while shift < L:
        pm = jnp.maximum(pm, jnp.where(lane_t >= shift, pltpu.roll(pm, shift, axis=1), NEG))
        shift *= 2
    d_t = -jnp.maximum(m_prev, pm)
    wi_t = jnp.exp(m_prev + d_t)
    em_t = jnp.exp(-(b_t - d_t))
    bl = jnp.broadcast_to(b_t[:, L - 1:L], (SUBLANES, L))
    g_t = bl - b_t + i_t
    m_new = jnp.maximum(bl + m_prev, jnp.broadcast_to(jnp.max(g_t, axis=1, keepdims=True), (SUBLANES, L)))
    ws_t = jnp.exp(g_t - m_new)
    wc = jnp.exp(bl + m_prev - m_new)
    m_s[...] = m_new
    stats = jnp.concatenate([d_t, wi_t, em_t, ws_t, zero_rows[0:L - 4 * SUBLANES]], axis=0).T

    causal = row_l >= lane_l
    ones_blk = jnp.ones((L, DH_C), BF16)
    v_all = vp_s[...]
    o_all = o_ref[...]
    z_all = z_ref[...]

    for h in range(H_C):
        cols = slice(h * DH_C, (h + 1) * DH_C)
        q = qk[:, cols].astype(BF16)
        k = qk[:, D_C + h * DH_C:D_C + (h + 1) * DH_C] * (DH_C ** -0.5)
        v_aug = jnp.concatenate([v_all[:, cols].astype(BF16), ones_blk], axis=1)
        d_col = stats[:, h:h + 1]
        wi_col = stats[:, SUBLANES + h:SUBLANES + h + 1]
        em_col = stats[:, 2 * SUBLANES + h:2 * SUBLANES + h + 1]
        ws_col = stats[:, 3 * SUBLANES + h:3 * SUBLANES + h + 1]
        c_prev = caug_s[h]

        a = jnp.exp(jnp.where(causal, d_col + c_t[h:h + 1, :], NEG)) * _bdot_nt(q, k)
        r1_ = jnp.dot(a.astype(BF16), v_aug, preferred_element_type=F32)
        r2_ = jnp.dot(q, c_prev.astype(BF16), preferred_element_type=F32)
        num = r1_[:, 0:DH_C] + wi_col * r2_[:, 0:DH_C]
        den = r1_[:, DH_C:] + wi_col * r2_[:, DH_C:]
        hh = num / jnp.maximum(jnp.abs(den), em_col)

        wc_row = jnp.concatenate([wc[h:h + 1, :], wc[h:h + 1, :]], axis=1)
        caug_s[h] = wc_row * c_prev + _bdot_tn(k * ws_col, v_aug)

        ms = jnp.sum(hh * hh, axis=-1, keepdims=True) * (1.0 / DH_C)
        hn = hh * lax.rsqrt(ms + EPS) * hn_ref[:, cols]
        z = z_all[:, cols]
        gate = z / ((1.0 + jnp.exp(-o_all[:, cols])) * (1.0 + jnp.exp(-z)))
        y_ref[:, cols] = (hn[0:t_rows, :] * gate).astype(BF16)
    return new_carry


def _mlstm_kernel(*refs, t_rows, nb):
    (qk_b, v_b, o_b, z_b, gt_b, cbuf_b, c0_b, n0_b, m0_b, cw_ref, cb_ref, gb_ref, hn_ref,
     y_b, conv_out_b, c_out_b, n_out_b, m_out_b, xp_b, vp_b, gp_b, caug_b, m_b) = refs
    ci = pl.program_id(1)
    seqs = range(nb)

    @pl.when(ci == 0)
    def _():
        for bi in seqs:
            _mlstm_init(cbuf_b.at[bi], c0_b.at[bi], n0_b.at[bi], m0_b.at[bi],
                        xp_b.at[bi], vp_b.at[bi], gp_b.at[bi], caug_b.at[bi], m_b.at[bi])

    carries = [_mlstm_chunk(qk_b.at[bi], v_b.at[bi], o_b.at[bi], z_b.at[bi], gt_b.at[bi],
                            cw_ref, cb_ref, gb_ref, hn_ref, y_b.at[bi],
                            xp_b.at[bi], vp_b.at[bi], gp_b.at[bi], caug_b.at[bi], m_b.at[bi], t_rows=t_rows)
               for bi in seqs]

    @pl.when(ci == pl.num_programs(1) - 1)
    def _():
        for bi in seqs:
            _mlstm_final(carries[bi], conv_out_b.at[bi], c_out_b.at[bi], n_out_b.at[bi], m_out_b.at[bi],
                         caug_b.at[bi], m_b.at[bi])


def _mlstm(proj, gates, conv_buf, c0, n0, m0, state_layer, layer, conv_w, conv_b3, gate_bias3, hn_g3, t_rows):
    b, seq, _ = proj.shape
    nchunks = seq // t_rows
    hp = SUBLANES
    nb = MLSTM_SEQS_PER_STEP
    assert b % nb == 0
    col = lambda base, width: pl.BlockSpec((nb, t_rows, width), lambda bi, ci: (bi, ci, base))
    param = lambda shape: pl.BlockSpec((None,) + shape, lambda bi, ci: (layer, 0, 0))
    per_b3 = lambda shape: pl.BlockSpec((nb,) + shape, lambda bi, ci: (bi, 0, 0))
    per_b4 = lambda shape: pl.BlockSpec((nb,) + shape, lambda bi, ci: (bi, 0, 0, 0))
    state = lambda shape: pl.BlockSpec((None, nb) + shape,
                                       lambda bi, ci: (state_layer, bi) + (0,) * len(shape))
    return pl.pallas_call(
        functools.partial(_mlstm_kernel, t_rows=t_rows, nb=nb),
        out_shape=(jax.ShapeDtypeStruct((b, seq, D_C), BF16),
                   jax.ShapeDtypeStruct((b, CONV_W - 1, 2 * D_C), F32),
                   jax.ShapeDtypeStruct((b, H_C, DH_C, DH_C), F32),
                   jax.ShapeDtypeStruct((b, hp, DH_C), F32),
                   jax.ShapeDtypeStruct((b, hp, LANES), F32)),
        grid=(b // nb, nchunks),
        in_specs=[col(COL_QK, 2 * D_C), col(COL_VC, D_C), col(COL_OC, D_C), col(COL_ZC, D_C),
                  pl.BlockSpec((nb, t_rows, LANES), lambda bi, ci: (bi, ci, 0)),
                  state((CONV_W - 1, 2 * D_C)), state((H_C, DH_C, DH_C)), state((H_C, DH_C)),
                  state((hp, LANES)),
                  param((CONV_W, 2 * D_C)), param((1, 2 * D_C)), param((1, LANES)), param((1, D_C))],
        out_specs=(pl.BlockSpec((nb, t_rows, D_C), lambda bi, ci: (bi, ci, 0)),
                   per_b3((CONV_W - 1, 2 * D_C)), per_b4((H_C, DH_C, DH_C)), per_b3((hp, DH_C)),
                   per_b3((hp, LANES))),
        scratch_shapes=[pltpu.VMEM((nb, MLSTM_ROW0 + CHUNK_C, 2 * D_C), F32),
                        pltpu.VMEM((nb, CHUNK_C, D_C), F32),
                        pltpu.VMEM((nb, CHUNK_C, LANES), F32),
                        pltpu.VMEM((nb, H_C, DH_C, 2 * DH_C), F32),
                        pltpu.VMEM((nb, hp, LANES), F32)],
        compiler_params=_cparams(("arbitrary", "arbitrary")),
        name="mlstm",
    )(proj, proj, proj, proj, gates, conv_buf, c0, n0, m0, conv_w, conv_b3, gate_bias3, hn_g3)


def _gate_lanes(i_part, f_part):
    lead = i_part.shape[:-1]
    gap = jnp.zeros(lead + (SUBLANES - H_C,), i_part.dtype)
    tail = jnp.zeros(lead + (LANES - 2 * SUBLANES,), i_part.dtype)
    return jnp.concatenate([i_part, gap, f_part, gap, tail], axis=-1)


def _pad_heads(a):
    pad = [(0, 0)] * a.ndim
    pad[1] = (0, SUBLANES - a.shape[1])
    return jnp.pad(a, pad)


def _mixers(proj, gates, lw, layer, attn_fn, sgu_fn, conv_buf, c0, n0, m0, state_layer, t_rows):
    ya, nk, nv = attn_fn(proj)
    sgu_out = sgu_fn(proj)
    yc, nconv, c_new, n_new, m_new = _mlstm(proj, gates, conv_buf, c0, n0, m0, state_layer, layer,
                                            lw["conv_w"], lw["conv_b"], lw["gate_bias"], lw["hn_g"], t_rows)
    return ya, sgu_out, yc, nk, nv, nconv, c_new, n_new[:, :H_C], m_new[:, :H_C, 0]


def kernel(x_prompt, x_sample, c_prompt, c_sample, cache_k_win, cache_v_win, state_conv, state_C, state_n, state_m, rel_bias, norm_g, ada_w, ada_b, w_in, qn_g, kn_g, sgu_g, sgu_w, sgu_b, conv_w, conv_b, f_bias, i_bias, hn_g, w_out):
    depth = w_in.shape[0]
    bp, seq, _ = x_prompt.shape
    bd, t_new, _ = x_sample.shape
    n_past = cache_k_win.shape[2]
    assert t_new == SUBLANES and n_past % LANES == 0 and seq % 1024 == 0

    w_in_t = _unpack_w_in(w_in)
    w_gate = _gate_lanes(w_in[:, :, D_MAIN:D_MAIN + H_C], w_in[:, :, D_MAIN + H_C:])
    w_out_bf = w_out.astype(BF16)

    rel_t = rel_bias.T
    bias_p = _expand_bias(rel_t, jnp.asarray(_prompt_bucket_table()), 64, LOG2E)
    bias_p = bias_p.reshape(H_A, N_PAT, 2, BLK_A, 2 * BLK_A)
    bias_s = _expand_bias(rel_t, jnp.asarray(_sample_bucket_table(n_past, t_new)), SUBLANES, 1.0)
    bias_s = bias_s.reshape(H_A, N_PAT, t_new, n_past + LANES)

    n_c = bp + bd
    c_all = jnp.pad(jnp.concatenate([c_prompt, c_sample], axis=0), ((0, 2 * SUBLANES - n_c), (0, 0)))
    mod = _ada_mod(c_all, ada_w, ada_b)

    cache_kt = jnp.transpose(cache_k_win, (0, 1, 3, 4, 2))
    cache_vt = jnp.transpose(cache_v_win, (0, 1, 3, 4, 2))
    gq_t = jnp.tile(qn_g, (1, H_A))[:, None, :]
    gk_t = jnp.tile(kn_g, (1, H_A))[:, None, :]
    lw = {"conv_w": conv_w, "conv_b": conv_b[:, None, :], "gate_bias": _gate_lanes(i_bias, f_bias)[:, None, :],
          "hn_g": hn_g[:, None, :]}
    sg3 = sgu_g[:, None, :]
    sgu_bt = jnp.transpose(sgu_b, (0, 2, 1))
    zeros_conv = jnp.zeros((1, bp, CONV_W - 1, 2 * D_C), F32)
    zeros_c = jnp.zeros((1, bp, H_C, DH_C, DH_C), F32)
    zeros_n = jnp.zeros((1, bp, H_C, DH_C), F32)
    zeros_m = jnp.zeros((1, bp, SUBLANES, LANES), F32)
    m0_all = jnp.broadcast_to(_pad_heads(state_m.reshape(depth * bd, H_C))[:, :, None],
                              (depth * bd, SUBLANES, LANES)).reshape(depth, bd, SUBLANES, LANES)
    keep = min(WIN_MAX, seq)
    pk_all = jnp.zeros((depth, bp, D_A, keep), F32)
    pv_all = jnp.zeros((depth, bp, D_A, keep), F32)
    w8 = sgu_w[:, :, :t_new, :t_new] * jnp.tril(jnp.ones((t_new, t_new), F32))
    wexp = jnp.repeat(jnp.transpose(w8, (0, 3, 2, 1)), C_B, axis=3)
    bexp = jnp.repeat(jnp.transpose(sgu_b[:, :, :t_new], (0, 2, 1)), C_B, axis=2)

    xp = x_prompt.reshape(bp * seq, D_MODEL)
    xs = x_sample.reshape(bd * t_new, D_MODEL)
    outs = [[] for _ in range(13)]
    mod4 = jnp.transpose(mod.reshape(depth, mod.shape[1], 3, D_MODEL), (0, 2, 1, 3))
    mod_p = mod4[:, :, :bp, None, :]
    mod_s = jnp.repeat(mod4[:, :, bp:n_c], t_new, axis=2)[:, :, None]
    norm_g3 = norm_g[:, None, :]
    hp = _norm(xp, mod_p, norm_g3, 0, 512)
    hs = _norm(xs, mod_s, norm_g3, 0, bd * t_new)
    for l in range(depth):
        last = l == depth - 1

        proj, gates = _inproj(hp, w_in_t, w_gate, l, 2048)
        proj = proj.reshape(bp, seq, D_MAIN)
        gates = gates.reshape(bp, seq, LANES)
        ya, yb, yc, pk_all, pv_all, ncv, nc_, nn_, nm = _mixers(
            proj, gates, lw, l,
            lambda pr: _attn_prompt(pr, gq_t, gk_t, bias_p, pk_all, pv_all, l),
            lambda pr: _sgu_prompt(pr, sg3, sgu_w, sgu_bt, l),
            zeros_conv, zeros_c, zeros_n, zeros_m, 0, CHUNK_C)
        xp, hp = _outproj(ya.reshape(bp * seq, D_A), yb.reshape(bp * seq, D_B), yc.reshape(bp * seq, D_C),
                          xp, mod_p, norm_g3, w_out_bf, l, 512, not last)
        for i, a in enumerate((ncv, nc_, nn_, nm)):
            outs[2 + i].append(a)

        proj_s, gates_s = _inproj(hs, w_in_t, w_gate, l, bd * t_new, INPROJ_TN_SAMPLE)
        proj_s = proj_s.reshape(bd, t_new, D_MAIN)
        gates_s = gates_s.reshape(bd, t_new, LANES)
        ya, sgu_out, yc, nk, nv, ncv, nc_, nn_, nm = _mixers(
            proj_s, gates_s, lw, l,
            lambda pr: _attn_sample(pr, cache_kt, cache_vt, gq_t, gk_t, bias_s, l),
            lambda pr: _sgu_sample(pr, sg3, wexp, bexp, l),
            state_conv, state_C, state_n, m0_all, l, t_new)
        yb, vn = sgu_out
        xs, hs = _outproj(ya.reshape(bd * t_new, D_A), yb.reshape(bd * t_new, D_B), yc.reshape(bd * t_new, D_C),
                          xs, mod_s, norm_g3, w_out_bf, l, bd * t_new, not last)
        for i, a in enumerate((nk.reshape(bd, t_new, H_A, HEAD_DIM_A), nv.reshape(bd, t_new, H_A, HEAD_DIM_A),
                               vn, ncv, nc_, nn_, nm)):
            outs[6 + i].append(a)

    stacked = [jnp.stack(o) for o in outs[2:]]
    p_k, p_v = (jnp.transpose(a.reshape(depth, bp, H_A, HEAD_DIM_A, keep), (0, 1, 4, 2, 3)) for a in (pk_all, pv_all))
    return (xp.reshape(bp, seq, D_MODEL), xs.reshape(bd, t_new, D_MODEL), p_k, p_v, *stacked)
```

```python
import functools
import math

import numpy as np
import jax
import jax.numpy as jnp
from jax import lax
from jax.experimental import pallas as pl
from jax.experimental.pallas import tpu as pltpu

F32 = jnp.float32
BF16 = jnp.bfloat16

D_MODEL = 2048
HEAD_DIM_A = 64
D_A = 768
H_A = 12
D_B = 512
G_B = 4
C_B = 128
CHUNK_B = 128
D_C = 768
DH_C = 128
H_C = 6
CHUNK_C = 128
CONV_W = 4
PATTERNS = ((128, 1), (512, 4), (2048, 16))
N_PAT = len(PATTERNS)
WIN_MAX = 2048
BLK_A = 128
N_BUCKETS = 32
MAX_DIST = 2048
EPS = 1e-6
D_MAIN = 4 * D_A + 3 * D_B + 2 * D_C + 3 * D_C
D_IN = D_MAIN + 2 * H_C

LANES = 128
SUBLANES = 8
VMEM_LIMIT = 56 * 1024 * 1024

NEG = -1e30
LOG2E = 1.4426950408889634
PAD_A = BLK_A * PATTERNS[-1][1]
HPAIR = LANES // HEAD_DIM_A

COL_Q, COL_K, COL_V, COL_Z = 0, D_A // LANES, 2 * D_A // LANES, 3 * D_A // LANES
COL_UB, COL_VB, COL_ZB = 4 * D_A // D_B, 4 * D_A // D_B + 1, 4 * D_A // D_B + 2
COL_QK = (4 * D_A + 3 * D_B) // (2 * D_C)
COL_VC = (4 * D_A + 3 * D_B + 2 * D_C) // D_C
COL_OC, COL_ZC = COL_VC + 1, COL_VC + 2


def _cparams(sem):
    return pltpu.CompilerParams(dimension_semantics=sem, vmem_limit_bytes=VMEM_LIMIT)


def _silu(x):
    return x * jax.nn.sigmoid(x)


def _bdot_nt(a, b):
    return lax.dot_general(a.astype(BF16), b.astype(BF16), (((1,), (1,)), ((), ())),
                           preferred_element_type=F32)


def _bdot_tn(a, b):
    return lax.dot_general(a.astype(BF16), b.astype(BF16), (((0,), (0,)), ((), ())),
                           preferred_element_type=F32)


def _bucket_np(dist):
    max_exact = N_BUCKETS // 2
    df = np.maximum(dist, 1).astype(np.float32)
    large = max_exact + (np.log(df / np.float32(max_exact)) / np.float32(math.log(MAX_DIST / max_exact))
                         * np.float32(N_BUCKETS - max_exact)).astype(np.int32)
    return np.where(dist < max_exact, dist, np.minimum(large, N_BUCKETS - 1)).astype(np.int32)


def _prompt_bucket_table():
    qi = np.arange(BLK_A)[:, None]
    ki = np.arange(2 * BLK_A)[None, :]
    j = qi + BLK_A - ki
    out = []
    for win, dil in PATTERNS:
        n_back = win // dil
        band = (j >= 0) & (j <= n_back)
        b = _bucket_np(np.clip(j, 0, n_back) * dil)
        out.append(np.where(band, b, -1))
        out.append(np.where(band & (ki >= BLK_A), b, -1))
    return np.stack(out).reshape(N_PAT * 2 * BLK_A, 2 * BLK_A).astype(np.int32)


def _sample_bucket_table(n_past, t_new):
    c = np.arange(n_past + LANES)[None, :]
    t = np.arange(t_new)[:, None]
    delta = n_past + t - c
    out = []
    for win, dil in PATTERNS:
        valid = (c < n_past + t_new) & (delta >= 0) & (delta % dil == 0) & (delta // dil <= win // dil)
        out.append(np.where(valid, _bucket_np(np.maximum(delta, 0)), -1))
    return np.stack(out).reshape(N_PAT * t_new, n_past + LANES).astype(np.int32)


def _bias_kernel(rb_ref, idx_ref, out_ref, *, scale, row_chunk):
    h = pl.program_id(0)

    def chunk(c, carry):
        rows = pl.ds(pl.multiple_of(c * row_chunk, row_chunk), row_chunk)
        idx = idx_ref[rows, :]
        out = jnp.full(idx.shape, NEG, F32)
        for b in range(N_BUCKETS):
            out = jnp.where(idx == b, rb_ref[h, b] * scale, out)
        out_ref[rows, :] = out
        return carry
    lax.fori_loop(0, idx_ref.shape[0] // row_chunk, chunk, 0)


def _expand_bias(rel_bias_t, idx, row_chunk, scale):
    rows, cols = idx.shape
    return pl.pallas_call(
        functools.partial(_bias_kernel, scale=scale, row_chunk=row_chunk),
        out_shape=jax.ShapeDtypeStruct((H_A, rows, cols), F32),
        grid=(H_A,),
        in_specs=[pl.BlockSpec(memory_space=pltpu.SMEM),
                  pl.BlockSpec((rows, cols), lambda h: (0, 0))],
        out_specs=pl.BlockSpec((None, rows, cols), lambda h: (h, 0, 0)),
        compiler_params=_cparams(("arbitrary",)),
        name="bias_expand",
    )(rel_bias_t, idx)


def _ada_kernel(c_ref, w_ref, b_ref, o_ref):
    c = c_ref[...]
    a = _silu(c)
    w = w_ref[...]
    a_hi = a.astype(BF16)
    a_lo = (a - a_hi.astype(F32)).astype(BF16)
    w_hi = w.astype(BF16)
    w_lo = (w - w_hi.astype(F32)).astype(BF16)
    acc = jnp.dot(a_hi, w_hi, preferred_element_type=F32)
    acc += jnp.dot(a_hi, w_lo, preferred_element_type=F32)
    acc += jnp.dot(a_lo, w_hi, preferred_element_type=F32)
    o_ref[...] = acc + b_ref[...]


def _ada_mod(c_all, ada_w, ada_b):
    depth = ada_w.shape[0]
    rows = c_all.shape[0]
    tn = 768
    n = 3 * D_MODEL
    return pl.pallas_call(
        _ada_kernel,
        out_shape=jax.ShapeDtypeStruct((depth, rows, n), F32),
        grid=(depth, n // tn),
        in_specs=[pl.BlockSpec((rows, D_MODEL), lambda l, j: (0, 0)),
                  pl.BlockSpec((None, D_MODEL, tn), lambda l, j: (l, 0, j)),
                  pl.BlockSpec((None, 1, tn), lambda l, j: (l, 0, j))],
        out_specs=pl.BlockSpec((None, rows, tn), lambda l, j: (l, 0, j)),
        compiler_params=_cparams(("arbitrary", "arbitrary")),
        name="ada_mod",
    )(c_all, ada_w, ada_b.reshape(depth, 1, n))


def _norm_mod(x, g, sc, sh):
    ms = jnp.sum(x * x, axis=-1, keepdims=True) * (1.0 / D_MODEL)
    return ((x * lax.rsqrt(ms + EPS) * g) * (1.0 + sc) + sh).astype(BF16)


def _norm_kernel(x_ref, sc_ref, sh_ref, g_ref, h_ref):
    h_ref[...] = _norm_mod(x_ref[...], g_ref[...], sc_ref[...], sh_ref[...])


MOD_SHIFT, MOD_SCALE, MOD_GATE = 0, 1, 2


def _mod_spec(mod5, layer, which, tiles_per_group):
    return pl.BlockSpec((None, None, None, mod5.shape[3], D_MODEL),
                        lambda i: (layer, which, i // tiles_per_group, 0, 0))


def _norm(x2d, mod5, norm_g3, layer, tm):
    m = x2d.shape[0]
    tiles_per_group = m // tm // mod5.shape[2]
    return pl.pallas_call(
        _norm_kernel,
        out_shape=jax.ShapeDtypeStruct((m, D_MODEL), BF16),
        grid=(m // tm,),
        in_specs=[pl.BlockSpec((tm, D_MODEL), lambda i: (i, 0)),
                  _mod_spec(mod5, layer, MOD_SCALE, tiles_per_group),
                  _mod_spec(mod5, layer, MOD_SHIFT, tiles_per_group),
                  pl.BlockSpec((None, 1, D_MODEL), lambda i: (layer, 0, 0))],
        out_specs=pl.BlockSpec((tm, D_MODEL), lambda i: (i, 0)),
        compiler_params=_cparams(("arbitrary",)),
        name="norm",
    )(x2d, mod5, mod5, norm_g3)


INPROJ_TN = 768
INPROJ_TN_SAMPLE = 4224


def _inproj_kernel(h_ref, w_ref, wg_ref, proj_ref, gates_ref):
    @pl.when(pl.program_id(1) == 0)
    def _():
        gates_ref[...] = jnp.dot(h_ref[...], wg_ref[...].astype(BF16), preferred_element_type=F32)

    proj_ref[...] = lax.dot_general(h_ref[...], w_ref[...], (((1,), (1,)), ((), ())),
                                    preferred_element_type=F32)


UNPACK_PITCH = 72


def _unpack_kernel(x_hbm, o_ref, buf_a, buf_b, sem, *, depth, k_tiles, tn):
    i = pl.program_id(0)
    rows_per_col = k_tiles * depth

    def copy(tile, n, buf, slot):
        return pltpu.make_async_copy(x_hbm.at[tile * tn + n], buf.at[pl.ds(n * UNPACK_PITCH, rows_per_col)],
                                     sem.at[slot])

    def start_all(tile, buf, slot):
        def body(n, carry):
            copy(tile, n, buf, slot).start()
            return carry
        lax.fori_loop(0, tn, body, 0, unroll=8)

    def wait_all(tile, buf, slot):
        def body(n, carry):
            copy(tile, n, buf, slot).wait()
            return carry
        lax.fori_loop(0, tn, body, 0, unroll=8)

    def convert(buf, half):
        for l in range(depth):
            for kt in range(k_tiles):
                rows = pl.ds(kt * depth + l, tn, stride=UNPACK_PITCH)
                o_ref[l, half * tn:(half + 1) * tn, kt * LANES:(kt + 1) * LANES] = buf[rows, :].astype(BF16)

    @pl.when(i == 0)
    def _():
        start_all(0, buf_a, 0)

    start_all(2 * i + 1, buf_b, 1)
    wait_all(2 * i, buf_a, 0)
    convert(buf_a, 0)

    @pl.when(i + 1 < pl.num_programs(0))
    def _():
        start_all(2 * i + 2, buf_a, 0)

    wait_all(2 * i + 1, buf_b, 1)
    convert(buf_b, 1)


def _unpack_w_in(w_in):
    depth, d, d_in = w_in.shape
    k_tiles = d // LANES
    tn = LANES
    assert D_MAIN % (2 * tn) == 0 and k_tiles * depth <= UNPACK_PITCH
    cols = w_in.reshape(depth, k_tiles, LANES, d_in).transpose(3, 1, 0, 2).reshape(d_in, k_tiles * depth, LANES)
    return pl.pallas_call(
        functools.partial(_unpack_kernel, depth=depth, k_tiles=k_tiles, tn=tn),
        out_shape=jax.ShapeDtypeStruct((depth, D_MAIN, d), BF16),
        grid=(D_MAIN // (2 * tn),),
        in_specs=[pl.BlockSpec(memory_space=pl.ANY)],
        out_specs=pl.BlockSpec((depth, 2 * tn, d), lambda i: (0, i, 0)),
        scratch_shapes=[pltpu.VMEM((tn * UNPACK_PITCH, LANES), F32),
                        pltpu.VMEM((tn * UNPACK_PITCH, LANES), F32),
                        pltpu.SemaphoreType.DMA((2,))],
        compiler_params=_cparams(("arbitrary",)),
        name="unpack_w_in",
    )(cols)


def _inproj(h2d, w_in_t, w_gate, layer, tm, tn=INPROJ_TN):
    m = h2d.shape[0]
    return pl.pallas_call(
        _inproj_kernel,
        out_shape=(jax.ShapeDtypeStruct((m, D_MAIN), F32), jax.ShapeDtypeStruct((m, LANES), F32)),
        grid=(m // tm, D_MAIN // tn),
        in_specs=[pl.BlockSpec((tm, D_MODEL), lambda i, j: (i, 0)),
                  pl.BlockSpec((None, tn, D_MODEL), lambda i, j: (layer, j, 0)),
                  pl.BlockSpec((None, D_MODEL, LANES), lambda i, j: (layer, 0, 0))],
        out_specs=(pl.BlockSpec((tm, tn), lambda i, j: (i, j)),
                   pl.BlockSpec((tm, LANES), lambda i, j: (i, 0))),
        compiler_params=_cparams(("arbitrary", "arbitrary")),
        name="inproj",
    )(h2d, w_in_t, w_gate)


def _outproj_kernel(ya_ref, yb_ref, yc_ref, x_ref, gate_ref, w_ref, *rest, emit_next):
    y = jnp.dot(ya_ref[...], w_ref[0:D_A, :], preferred_element_type=F32)
    y += jnp.dot(yb_ref[...], w_ref[D_A:D_A + D_B, :], preferred_element_type=F32)
    y += jnp.dot(yc_ref[...], w_ref[D_A + D_B:, :], preferred_element_type=F32)
    x_new = x_ref[...] + gate_ref[...] * y
    if emit_next:
        sc_ref, sh_ref, g_ref, o_ref, h_ref = rest
        h_ref[...] = _norm_mod(x_new, g_ref[...], sc_ref[...], sh_ref[...])
    else:
        (o_ref,) = rest
    o_ref[...] = x_new


def _outproj(ya, yb, yc, x2d, mod5, norm_g3, w_out_bf, layer, tm, emit_next):
    m = x2d.shape[0]
    tiles_per_group = m // tm // mod5.shape[2]
    row = lambda width: pl.BlockSpec((tm, width), lambda i: (i, 0))
    in_specs = [row(D_A), row(D_B), row(D_C), row(D_MODEL),
                _mod_spec(mod5, layer, MOD_GATE, tiles_per_group),
                pl.BlockSpec((None, D_MODEL, D_MODEL), lambda i: (layer, 0, 0))]
    args = [ya, yb, yc, x2d, mod5, w_out_bf]
    out_shape = [jax.ShapeDtypeStruct((m, D_MODEL), F32)]
    out_specs = [row(D_MODEL)]
    if emit_next:
        in_specs += [_mod_spec(mod5, layer + 1, MOD_SCALE, tiles_per_group),
                     _mod_spec(mod5, layer + 1, MOD_SHIFT, tiles_per_group),
                     pl.BlockSpec((None, 1, D_MODEL), lambda i: (layer + 1, 0, 0))]
        args += [mod5, mod5, norm_g3]
        out_shape.append(jax.ShapeDtypeStruct((m, D_MODEL), BF16))
        out_specs.append(row(D_MODEL))
    out = pl.pallas_call(
        functools.partial(_outproj_kernel, emit_next=emit_next),
        out_shape=tuple(out_shape),
        grid=(m // tm,),
        in_specs=in_specs,
        out_specs=tuple(out_specs),
        compiler_params=_cparams(("arbitrary",)),
        name="outproj",
    )(*args)
    return out if emit_next else (out[0], None)


def _head_mean_matrix():
    row = lax.broadcasted_iota(jnp.int32, (LANES, LANES), 0)
    col = lax.broadcasted_iota(jnp.int32, (LANES, LANES), 1)
    same_head = (row < HEAD_DIM_A) == (col < HEAD_DIM_A)
    return jnp.where(same_head, 1.0 / HEAD_DIM_A, 0.0).astype(BF16)


def _head_norm(x, g, mean_mat):
    x2 = x * x
    hi = x2.astype(BF16)
    lo = (x2 - hi.astype(F32)).astype(BF16)
    ms = (jnp.dot(hi, mean_mat, preferred_element_type=F32)
          + jnp.dot(lo, mean_mat, preferred_element_type=F32))
    return x * lax.rsqrt(ms + EPS) * g


def _attn_prompt_kernel(q_ref, k_ref, v_ref, z_ref, gq_ref, gk_ref, bias_ref, pk_all_hbm, pv_all_hbm,
                        y_ref, pk_ref, pv_ref,
                        qn_s, kp_s, vp_s, ones_s, o_s, l_s, m_s, *, seq, keep, unroll):
    norm_rows = 512
    mean_mat = _head_mean_matrix()

    kp_s[0:PAD_A, :] = jnp.zeros((PAD_A, LANES), F32)
    vp_s[0:PAD_A, :] = jnp.zeros((PAD_A, LANES), F32)
    ones_s[...] = jnp.ones(ones_s.shape, BF16)

    def norm_chunk(c, carry):
        r0 = pl.multiple_of(c * norm_rows, norm_rows)
        rows = pl.ds(r0, norm_rows)
        qn_s[rows, :] = _head_norm(q_ref[rows, :], gq_ref[...], mean_mat) * (HEAD_DIM_A ** -0.5 * LOG2E)
        kp_s[pl.ds(PAD_A + r0, norm_rows), :] = _head_norm(k_ref[rows, :], gk_ref[...], mean_mat)
        vp_s[pl.ds(PAD_A + r0, norm_rows), :] = v_ref[rows, :]
        return carry
    lax.fori_loop(0, seq // norm_rows, norm_chunk, 0, unroll=True)

    for c in range(keep // LANES):
        src = slice(PAD_A + seq - keep + c * LANES, PAD_A + seq - keep + (c + 1) * LANES)
        pk_ref[:, c * LANES:(c + 1) * LANES] = kp_s[src, :].T
        pv_ref[:, c * LANES:(c + 1) * LANES] = vp_s[src, :].T

    left_b = lax.broadcasted_iota(jnp.int32, (BLK_A, LANES), 1) < HEAD_DIM_A

    for p, (win, dil) in enumerate(PATTERNS):
        blocks_per_residue = seq // (dil * BLK_A)

        def block(i, carry, p=p, dil=dil, blocks_per_residue=blocks_per_residue):
            r = i // blocks_per_residue
            n = i % blocks_per_residue
            q_start = r + n * (BLK_A * dil)
            k_start = PAD_A + q_start - BLK_A * dil
            if dil == 1:
                q_rows = pl.ds(pl.multiple_of(q_start, BLK_A), BLK_A)
                k_rows = pl.ds(pl.multiple_of(k_start, BLK_A), 2 * BLK_A)
            else:
                q_rows = pl.ds(q_start, BLK_A, stride=dil)
                k_rows = pl.ds(k_start, 2 * BLK_A, stride=dil)
            first = jnp.where(n == 0, 1, 0)
            q = qn_s[q_rows, :]
            k = kp_s[k_rows, :].astype(BF16)
            v = vp_s[k_rows, :].astype(BF16)
            rhs = jnp.concatenate([v, ones_s[...]], axis=1)
            q2 = jnp.concatenate([jnp.where(left_b, q, 0.0), jnp.where(left_b, 0.0, q)], axis=0)
            s2 = _bdot_nt(q2, k)
            es, ms = [], []
            for h in range(HPAIR):
                s = s2[h * BLK_A:(h + 1) * BLK_A, :] + bias_ref[h, p, first]
                mh = jnp.max(s, axis=-1, keepdims=True)
                es.append(jnp.exp2(s - mh).astype(BF16))
                ms.append(mh)
            res = jnp.dot(jnp.concatenate(es, axis=0), rhs, preferred_element_type=F32)
            o_s[p, q_rows, :] = jnp.where(left_b, res[0:BLK_A, 0:LANES], res[BLK_A:, 0:LANES])
            l_s[p, q_rows, :] = jnp.where(left_b, res[0:BLK_A, LANES:], res[BLK_A:, LANES:])
            m_s[p, q_rows, :] = jnp.where(left_b, ms[0], ms[1])
            return carry
        lax.fori_loop(0, seq // BLK_A, block, 0, unroll=unroll)

    def out_chunk(c, carry):
        rows = pl.ds(pl.multiple_of(c * norm_rows, norm_rows), norm_rows)
        m = [m_s[p, rows, :] for p in range(N_PAT)]
        top = functools.reduce(jnp.maximum, m)
        w = [jnp.exp2(x - top) for x in m]
        num = functools.reduce(lambda a, b: a + b, [w[p] * o_s[p, rows, :] for p in range(N_PAT)])
        den = functools.reduce(lambda a, b: a + b, [w[p] * l_s[p, rows, :] for p in range(N_PAT)])
        z = z_ref[rows, :]
        y_ref[rows, :] = (num * z / (den * (1.0 + jnp.exp(-z)))).astype(BF16)
        return carry
    lax.fori_loop(0, seq // norm_rows, out_chunk, 0)


def _attn_prompt(proj, gq_t, gk_t, bias_p, pk_all, pv_all, layer):
    b, seq, _ = proj.shape
    keep = pk_all.shape[-1]
    assert seq % (BLK_A * PATTERNS[-1][1]) == 0 and keep == min(WIN_MAX, seq)
    col = lambda base: pl.BlockSpec((None, seq, LANES), lambda bi, hp: (bi, 0, base + hp))
    vec = pl.BlockSpec((None, 1, LANES), lambda bi, hp: (layer, 0, hp))
    window = pl.BlockSpec((None, None, LANES, keep), lambda bi, hp: (layer, bi, hp, 0))
    untouched = pl.BlockSpec(memory_space=pl.ANY)
    return pl.pallas_call(
        functools.partial(_attn_prompt_kernel, seq=seq, keep=keep, unroll=seq // BLK_A),
        out_shape=(jax.ShapeDtypeStruct((b, seq, D_A), BF16),
                   jax.ShapeDtypeStruct(pk_all.shape, F32),
                   jax.ShapeDtypeStruct(pv_all.shape, F32)),
        grid=(b, H_A // HPAIR),
        in_specs=[col(COL_Q), col(COL_K), col(COL_V), col(COL_Z), vec, vec,
                  pl.BlockSpec((HPAIR, N_PAT, 2, BLK_A, 2 * BLK_A), lambda bi, hp: (hp, 0, 0, 0, 0)),
                  untouched, untouched],
        out_specs=(pl.BlockSpec((None, seq, LANES), lambda bi, hp: (bi, 0, hp)), window, window),
        input_output_aliases={7: 1, 8: 2},
        scratch_shapes=[pltpu.VMEM((seq, LANES), F32),
                        pltpu.VMEM((PAD_A + seq, LANES), F32),
                        pltpu.VMEM((PAD_A + seq, LANES), F32),
                        pltpu.VMEM((2 * BLK_A, LANES), BF16),
                        pltpu.VMEM((N_PAT, seq, LANES), F32),
                        pltpu.VMEM((N_PAT, seq, LANES), F32),
                        pltpu.VMEM((N_PAT, seq, LANES), F32)],
        compiler_params=_cparams(("arbitrary", "arbitrary")),
        name="attn_prompt",
    )(proj, proj, proj, proj, gq_t, gk_t, bias_p, pk_all, pv_all)


def _attn_sample_kernel(q_ref, k_ref, v_ref, z_ref, ck_ref, cv_ref, gq_ref, gk_ref, bias_ref,
                        y_ref, nk_ref, nv_ref, *, n_past, t_new):
    q_all = q_ref[...]
    k_all = k_ref[...]
    v_all = v_ref[...]
    z_all = z_ref[...]
    nv_ref[...] = v_all
    pad_rows = jnp.zeros((LANES - t_new, HEAD_DIM_A), F32)

    def norm(x, g):
        ms = jnp.sum(x * x, axis=-1, keepdims=True) * (1.0 / HEAD_DIM_A)
        return x * lax.rsqrt(ms + EPS) * g

    for h in range(H_A):
        cols = slice(h * HEAD_DIM_A, (h + 1) * HEAD_DIM_A)
        qn = (norm(q_all[:, cols], gq_ref[:, cols]) * (HEAD_DIM_A ** -0.5)).astype(BF16)
        kn = norm(k_all[:, cols], gk_ref[:, cols])
        nk_ref[:, cols] = kn
        k_new = jnp.concatenate([kn, pad_rows], axis=0).astype(BF16)
        v_new = jnp.concatenate([v_all[:, cols], pad_rows], axis=0).astype(BF16)
        k_t = ck_ref[h].astype(BF16)
        v_t = cv_ref[h].astype(BF16)
        s_c = jnp.dot(qn, k_t, preferred_element_type=F32)
        s_n = _bdot_nt(qn, k_new)
        sc = [s_c + bias_ref[h, p, :, 0:n_past] for p in range(N_PAT)]
        sn = [s_n + bias_ref[h, p, :, n_past:] for p in range(N_PAT)]
        m = functools.reduce(jnp.maximum, [jnp.max(x, axis=-1, keepdims=True) for x in sc + sn])
        e_c = functools.reduce(lambda a, b: a + b, [jnp.exp(x - m) for x in sc])
        e_n = functools.reduce(lambda a, b: a + b, [jnp.exp(x - m) for x in sn])
        den = jnp.sum(e_c, axis=-1, keepdims=True) + jnp.sum(e_n, axis=-1, keepdims=True)
        o = _bdot_nt(e_c, v_t) + jnp.dot(e_n.astype(BF16), v_new, preferred_element_type=F32)
        y_ref[:, cols] = (o / den * _silu(z_all[:, cols])).astype(BF16)


def _attn_sample(proj_s, cache_kt, cache_vt, gq, gk, bias_s, layer):
    bd, t_new, _ = proj_s.shape
    n_past = cache_kt.shape[-1]
    col = lambda base: pl.BlockSpec((None, t_new, D_A), lambda bi: (bi, 0, base))
    vec = pl.BlockSpec((None, 1, D_A), lambda bi: (layer, 0, 0))
    cache = pl.BlockSpec((None, None, H_A, HEAD_DIM_A, n_past), lambda bi: (layer, bi, 0, 0, 0))
    out = pl.BlockSpec((None, t_new, D_A), lambda bi: (bi, 0, 0))
    return pl.pallas_call(
        functools.partial(_attn_sample_kernel, n_past=n_past, t_new=t_new),
        out_shape=(jax.ShapeDtypeStruct((bd, t_new, D_A), BF16),
                   jax.ShapeDtypeStruct((bd, t_new, D_A), F32),
                   jax.ShapeDtypeStruct((bd, t_new, D_A), F32)),
        grid=(bd,),
        in_specs=[col(0), col(1), col(2), col(3), cache, cache, vec, vec,
                  pl.BlockSpec((H_A, N_PAT, t_new, n_past + LANES), lambda bi: (0, 0, 0, 0))],
        out_specs=(out, out, out),
        compiler_params=_cparams(("arbitrary",)),
        name="attn_sample",
    )(proj_s, proj_s, proj_s, proj_s, cache_kt, cache_vt, gq, gk, bias_s)


def _sgu_prompt_kernel(u_ref, v_ref, z_ref, g_ref, w_ref, bt_ref, y_ref, *, chunks):
    row = lax.broadcasted_iota(jnp.int32, (CHUNK_B, CHUNK_B), 0)
    colm = lax.broadcasted_iota(jnp.int32, (CHUNK_B, CHUNK_B), 1)
    tril = row >= colm
    ws = [jnp.where(tril, w_ref[g], 0.0).astype(BF16) for g in range(G_B)]
    for c in range(chunks):
        rows = slice(c * CHUNK_B, (c + 1) * CHUNK_B)
        v = v_ref[rows, :]
        ms = jnp.sum(v * v, axis=-1, keepdims=True) * (1.0 / D_B)
        vn = (v * lax.rsqrt(ms + EPS) * g_ref[...]).astype(BF16)
        for g in range(G_B):
            cols = slice(g * C_B, (g + 1) * C_B)
            mix = jnp.dot(ws[g], vn[:, cols], preferred_element_type=F32) + bt_ref[:, g:g + 1]
            y_ref[rows, cols] = (u_ref[rows, cols] * mix * _silu(z_ref[rows, cols])).astype(BF16)


def _sgu_prompt(proj, sgu_g3, sgu_w, sgu_bt, layer):
    b, seq, _ = proj.shape
    ts = 1024
    col = lambda base: pl.BlockSpec((None, ts, D_B), lambda bi, i: (bi, i, base))
    return pl.pallas_call(
        functools.partial(_sgu_prompt_kernel, chunks=ts // CHUNK_B),
        out_shape=jax.ShapeDtypeStruct((b, seq, D_B), BF16),
        grid=(b, seq // ts),
        in_specs=[col(COL_UB), col(COL_VB), col(COL_ZB),
                  pl.BlockSpec((None, 1, D_B), lambda bi, i: (layer, 0, 0)),
                  pl.BlockSpec((None, G_B, CHUNK_B, CHUNK_B), lambda bi, i: (layer, 0, 0, 0)),
                  pl.BlockSpec((None, CHUNK_B, G_B), lambda bi, i: (layer, 0, 0))],
        out_specs=pl.BlockSpec((None, ts, D_B), lambda bi, i: (bi, i, 0)),
        compiler_params=_cparams(("arbitrary", "arbitrary")),
        name="sgu_prompt",
    )(proj, proj, proj, sgu_g3, sgu_w, sgu_bt)


def _sgu_sample_kernel(u_ref, v_ref, z_ref, g_ref, wexp_ref, bexp_ref, y_ref, vn_ref, *, t_new):
    v = v_ref[...]
    ms = jnp.sum(v * v, axis=-1, keepdims=True) * (1.0 / D_B)
    vn = v * lax.rsqrt(ms + EPS) * g_ref[...]
    vn_ref[...] = vn
    mix = bexp_ref[...]
    for s in range(t_new):
        mix = mix + wexp_ref[s] * vn[s:s + 1, :]
    y_ref[...] = (u_ref[...] * mix * _silu(z_ref[...])).astype(BF16)


def _sgu_sample(proj_s, sgu_g3, wexp, bexp, layer):
    bd, t_new, _ = proj_s.shape
    col = lambda base: pl.BlockSpec((None, t_new, D_B), lambda bi: (bi, 0, base))
    return pl.pallas_call(
        functools.partial(_sgu_sample_kernel, t_new=t_new),
        out_shape=(jax.ShapeDtypeStruct((bd, t_new, D_B), BF16),
                   jax.ShapeDtypeStruct((bd, t_new, D_B), F32)),
        grid=(bd,),
        in_specs=[col(COL_UB), col(COL_VB), col(COL_ZB),
                  pl.BlockSpec((None, 1, D_B), lambda bi: (layer, 0, 0)),
                  pl.BlockSpec((None, t_new, t_new, D_B), lambda bi: (layer, 0, 0, 0)),
                  pl.BlockSpec((None, t_new, D_B), lambda bi: (layer, 0, 0))],
        out_specs=(pl.BlockSpec((None, t_new, D_B), lambda bi: (bi, 0, 0)),
                   pl.BlockSpec((None, t_new, D_B), lambda bi: (bi, 0, 0))),
        compiler_params=_cparams(("arbitrary",)),
        name="sgu_sample",
    )(proj_s, proj_s, proj_s, sgu_g3, wexp, bexp)


def _log_sigmoid(x):
    return jnp.minimum(x, 0.0) - jnp.log1p(jnp.exp(-jnp.abs(x)))


MLSTM_SEQS_PER_STEP = 2
MLSTM_ROW0 = SUBLANES
MLSTM_CARRY = CONV_W - 1


def _mlstm_init(cbuf_ref, c0_ref, n0_ref, m0_ref, xp_s, vp_s, gp_s, caug_s, m_s):
    xp_s[...] = jnp.zeros(xp_s.shape, F32)
    vp_s[...] = jnp.zeros(vp_s.shape, F32)
    gp_s[...] = jnp.zeros(gp_s.shape, F32)
    xp_s[MLSTM_ROW0 - MLSTM_CARRY:MLSTM_ROW0, :] = cbuf_ref[...]
    n0 = n0_ref[...]
    n0_cols = jnp.concatenate([n0, jnp.zeros((DH_C - n0.shape[0], DH_C), F32)], axis=0).T
    for h in range(H_C):
        caug_s[h, :, 0:DH_C] = c0_ref[h]
        caug_s[h, :, DH_C:] = jnp.broadcast_to(n0_cols[:, h:h + 1], (DH_C, DH_C))
    m_s[...] = m0_ref[...]


def _mlstm_final(new_carry, conv_out_ref, c_out_ref, n_out_ref, m_out_ref, caug_s, m_s):
    conv_out_ref[...] = new_carry
    lane = lax.broadcasted_iota(jnp.int32, (DH_C, LANES), 1)
    n_cols = jnp.zeros((DH_C, LANES), F32)
    for h in range(H_C):
        c_out_ref[h] = caug_s[h, :, 0:DH_C]
        n_cols = jnp.where(lane == h, caug_s[h, :, DH_C:], n_cols)
    n_out_ref[...] = n_cols.T[0:SUBLANES, :]
    m_out_ref[...] = m_s[...]


def _mlstm_chunk(qk_ref, v_ref, o_ref, z_ref, gt_ref, cw_ref, cb_ref, gb_ref, hn_ref, y_ref,
                 xp_s, vp_s, gp_s, caug_s, m_s, *, t_rows):
    L = CHUNK_C
    base = MLSTM_ROW0
    carry_rows = MLSTM_CARRY
    row_l = lax.broadcasted_iota(jnp.int32, (L, LANES), 0)
    lane_l = lax.broadcasted_iota(jnp.int32, (L, LANES), 1)
    zero_rows = jnp.zeros((L - SUBLANES, LANES), F32)

    xp_s[base:base + t_rows, :] = qk_ref[...]
    vp_s[0:t_rows, :] = v_ref[...]
    gp_s[0:t_rows, :] = gt_ref[...]

    live = -(-t_rows // SUBLANES) * SUBLANES
    acc = cb_ref[...] + cw_ref[CONV_W - 1:CONV_W, :] * xp_s[base:base + live, :]
    for j in range(CONV_W - 1):
        off = base - carry_rows + j
        acc = acc + cw_ref[j:j + 1, :] * xp_s[off:off + live, :]
    qk = _silu(acc)
    if live < L:
        qk = jnp.concatenate([qk, jnp.zeros((L - live, qk.shape[1]), F32)], axis=0)
    new_carry = xp_s[base + t_rows - carry_rows:base + t_rows, :]
    xp_s[base - carry_rows:base, :] = new_carry

    gates_t = (gp_s[...] + gb_ref[...]).T
    lane_t = lax.broadcasted_iota(jnp.int32, (SUBLANES, L), 1)
    real_t = lane_t < t_rows
    i_t = jnp.where(real_t, gates_t[0:SUBLANES, :], NEG)
    logf_t = jnp.where(real_t, _log_sigmoid(gates_t[SUBLANES:2 * SUBLANES, :]), 0.0)
    upper = (row_l <= lane_l).astype(BF16)
    f1 = logf_t.astype(BF16)
    r1 = logf_t - f1.astype(F32)
    f2 = r1.astype(BF16)
    f3 = (r1 - f2.astype(F32)).astype(BF16)
    b_t = (jnp.dot(f1, upper, preferred_element_type=F32) + jnp.dot(f2, upper, preferred_element_type=F32)
           + jnp.dot(f3, upper, preferred_element_type=F32))
    m_prev = m_s[...]
    c_t = i_t - b_t
    pm = c_t
    shift = 1
    while shift < L:
        pm = jnp.maximum(pm, jnp.where(lane_t >= shift, pltpu.roll(pm, shift, axis=1), NEG))
        shift *= 2
    d_t = -jnp.maximum(m_prev, pm)
    wi_t = jnp.exp(m_prev + d_t)
    em_t = jnp.exp(-(b_t - d_t))
    bl = jnp.broadcast_to(b_t[:, L - 1:L], (SUBLANES, L))
    g_t = bl - b_t + i_t
    m_new = jnp.maximum(bl + m_prev, jnp.broadcast_to(jnp.max(g_t, axis=1, keepdims=True), (SUBLANES, L)))
    ws_t = jnp.exp(g_t - m_new)
    wc = jnp.exp(bl + m_prev - m_new)
    m_s[...] = m_new
    stats = jnp.concatenate([d_t, wi_t, em_t, ws_t, zero_rows[0:L - 4 * SUBLANES]], axis=0).T

    causal = row_l >= lane_l
    ones_blk = jnp.ones((L, DH_C), BF16)
    v_all = vp_s[...]
    o_all = o_ref[...]
    z_all = z_ref[...]

    for h in range(H_C):
        cols = slice(h * DH_C, (h + 1) * DH_C)
        q = qk[:, cols].astype(BF16)
        k = qk[:, D_C + h * DH_C:D_C + (h + 1) * DH_C] * (DH_C ** -0.5)
        v_aug = jnp.concatenate([v_all[:, cols].astype(BF16), ones_blk], axis=1)
        d_col = stats[:, h:h + 1]
        wi_col = stats[:, SUBLANES + h:SUBLANES + h + 1]
        em_col = stats[:, 2 * SUBLANES + h:2 * SUBLANES + h + 1]
        ws_col = stats[:, 3 * SUBLANES + h:3 * SUBLANES + h + 1]
        c_prev = caug_s[h]

        a = jnp.exp(jnp.where(causal, d_col + c_t[h:h + 1, :], NEG)) * _bdot_nt(q, k)
        r1_ = jnp.dot(a.astype(BF16), v_aug, preferred_element_type=F32)
        r2_ = jnp.dot(q, c_prev.astype(BF16), preferred_element_type=F32)
        num = r1_[:, 0:DH_C] + wi_col * r2_[:, 0:DH_C]
        den = r1_[:, DH_C:] + wi_col * r2_[:, DH_C:]
        hh = num / jnp.maximum(jnp.abs(den), em_col)

        wc_row = jnp.concatenate([wc[h:h + 1, :], wc[h:h + 1, :]], axis=1)
        caug_s[h] = wc_row * c_prev + _bdot_tn(k * ws_col, v_aug)

        ms = jnp.sum(hh * hh, axis=-1, keepdims=True) * (1.0 / DH_C)
        hn = hh * lax.rsqrt(ms + EPS) * hn_ref[:, cols]
        z = z_all[:, cols]
        gate = z / ((1.0 + jnp.exp(-o_all[:, cols])) * (1.0 + jnp.exp(-z)))
        y_ref[:, cols] = (hn[0:t_rows, :] * gate).astype(BF16)
    return new_carry


def _mlstm_kernel(*refs, t_rows, nb):
    (qk_b, v_b, o_b, z_b, gt_b, cbuf_b, c0_b, n0_b, m0_b, cw_ref, cb_ref, gb_ref, hn_ref,
     y_b, conv_out_b, c_out_b, n_out_b, m_out_b, xp_b, vp_b, gp_b, caug_b, m_b) = refs
    ci = pl.program_id(1)
    seqs = range(nb)

    @pl.when(ci == 0)
    def _():
        for bi in seqs:
            _mlstm_init(cbuf_b.at[bi], c0_b.at[bi], n0_b.at[bi], m0_b.at[bi],
                        xp_b.at[bi], vp_b.at[bi], gp_b.at[bi], caug_b.at[bi], m_b.at[bi])

    carries = [_mlstm_chunk(qk_b.at[bi], v_b.at[bi], o_b.at[bi], z_b.at[bi], gt_b.at[bi],
                            cw_ref, cb_ref, gb_ref, hn_ref, y_b.at[bi],
                            xp_b.at[bi], vp_b.at[bi], gp_b.at[bi], caug_b.at[bi], m_b.at[bi], t_rows=t_rows)
               for bi in seqs]

    @pl.when(ci == pl.num_programs(1) - 1)
    def _():
        for bi in seqs:
            _mlstm_final(carries[bi], conv_out_b.at[bi], c_out_b.at[bi], n_out_b.at[bi], m_out_b.at[bi],
                         caug_b.at[bi], m_b.at[bi])


def _mlstm(proj, gates, conv_buf, c0, n0, m0, state_layer, layer, conv_w, conv_b3, gate_bias3, hn_g3, t_rows):
    b, seq, _ = proj.shape
    nchunks = seq // t_rows
    hp = SUBLANES
    nb = MLSTM_SEQS_PER_STEP
    assert b % nb == 0
    col = lambda base, width: pl.BlockSpec((nb, t_rows, width), lambda bi, ci: (bi, ci, base))
    param = lambda shape: pl.BlockSpec((None,) + shape, lambda bi, ci: (layer, 0, 0))
    per_b3 = lambda shape: pl.BlockSpec((nb,) + shape, lambda bi, ci: (bi, 0, 0))
    per_b4 = lambda shape: pl.BlockSpec((nb,) + shape, lambda bi, ci: (bi, 0, 0, 0))
    state = lambda shape: pl.BlockSpec((None, nb) + shape,
                                       lambda bi, ci: (state_layer, bi) + (0,) * len(shape))
    return pl.pallas_call(
        functools.partial(_mlstm_kernel, t_rows=t_rows, nb=nb),
        out_shape=(jax.ShapeDtypeStruct((b, seq, D_C), BF16),
                   jax.ShapeDtypeStruct((b, CONV_W - 1, 2 * D_C), F32),
                   jax.ShapeDtypeStruct((b, H_C, DH_C, DH_C), F32),
                   jax.ShapeDtypeStruct((b, hp, DH_C), F32),
                   jax.ShapeDtypeStruct((b, hp, LANES), F32)),
        grid=(b // nb, nchunks),
        in_specs=[col(COL_QK, 2 * D_C), col(COL_VC, D_C), col(COL_OC, D_C), col(COL_ZC, D_C),
                  pl.BlockSpec((nb, t_rows, LANES), lambda bi, ci: (bi, ci, 0)),
                  state((CONV_W - 1, 2 * D_C)), state((H_C, DH_C, DH_C)), state((H_C, DH_C)),
                  state((hp, LANES)),
                  param((CONV_W, 2 * D_C)), param((1, 2 * D_C)), param((1, LANES)), param((1, D_C))],
        out_specs=(pl.BlockSpec((nb, t_rows, D_C), lambda bi, ci: (bi, ci, 0)),
                   per_b3((CONV_W - 1, 2 * D_C)), per_b4((H_C, DH_C, DH_C)), per_b3((hp, DH_C)),
                   per_b3((hp, LANES))),
        scratch_shapes=[pltpu.VMEM((nb, MLSTM_ROW0 + CHUNK_C, 2 * D_C), F32),
                        pltpu.VMEM((nb, CHUNK_C, D_C), F32),
                        pltpu.VMEM((nb, CHUNK_C, LANES), F32),
                        pltpu.VMEM((nb, H_C, DH_C, 2 * DH_C), F32),
                        pltpu.VMEM((nb, hp, LANES), F32)],
        compiler_params=_cparams(("arbitrary", "arbitrary")),
        name="mlstm",
    )(proj, proj, proj, proj, gates, conv_buf, c0, n0, m0, conv_w, conv_b3, gate_bias3, hn_g3)


def _gate_lanes(i_part, f_part):
    lead = i_part.shape[:-1]
    gap = jnp.zeros(lead + (SUBLANES - H_C,), i_part.dtype)
    tail = jnp.zeros(lead + (LANES - 2 * SUBLANES,), i_part.dtype)
    return jnp.concatenate([i_part, gap, f_part, gap, tail], axis=-1)


def _pad_heads(a):
    pad = [(0, 0)] * a.ndim
    pad[1] = (0, SUBLANES - a.shape[1])
    return jnp.pad(a, pad)


def _mixers(proj, gates, lw, layer, attn_fn, sgu_fn, conv_buf, c0, n0, m0, state_layer, t_rows):
    ya, nk, nv = attn_fn(proj)
    sgu_out = sgu_fn(proj)
    yc, nconv, c_new, n_new, m_new = _mlstm(proj, gates, conv_buf, c0, n0, m0, state_layer, layer,
                                            lw["conv_w"], lw["conv_b"], lw["gate_bias"], lw["hn_g"], t_rows)
    return ya, sgu_out, yc, nk, nv, nconv, c_new, n_new[:, :H_C], m_new[:, :H_C, 0]


def kernel(x_prompt, x_sample, c_prompt, c_sample, cache_k_win, cache_v_win, state_conv, state_C, state_n, state_m, rel_bias, norm_g, ada_w, ada_b, w_in, qn_g, kn_g, sgu_g, sgu_w, sgu_b, conv_w, conv_b, f_bias, i_bias, hn_g, w_out):
    depth = w_in.shape[0]
    bp, seq, _ = x_prompt.shape
    bd, t_new, _ = x_sample.shape
    n_past = cache_k_win.shape[2]
    assert t_new == SUBLANES and n_past % LANES == 0 and seq % 1024 == 0

    w_in_t = _unpack_w_in(w_in)
    w_gate = _gate_lanes(w_in[:, :, D_MAIN:D_MAIN + H_C], w_in[:, :, D_MAIN + H_C:])
    w_out_bf = w_out.astype(BF16)

    rel_t = rel_bias.T
    bias_p = _expand_bias(rel_t, jnp.asarray(_prompt_bucket_table()), 64, LOG2E)
    bias_p = bias_p.reshape(H_A, N_PAT, 2, BLK_A, 2 * BLK_A)
    bias_s = _expand_bias(rel_t, jnp.asarray(_sample_bucket_table(n_past, t_new)), SUBLANES, 1.0)
    bias_s = bias_s.reshape(H_A, N_PAT, t_new, n_past + LANES)

    n_c = bp + bd
    c_all = jnp.pad(jnp.concatenate([c_prompt, c_sample], axis=0), ((0, 2 * SUBLANES - n_c), (0, 0)))
    mod = _ada_mod(c_all, ada_w, ada_b)

    cache_kt = jnp.transpose(cache_k_win, (0, 1, 3, 4, 2))
    cache_vt = jnp.transpose(cache_v_win, (0, 1, 3, 4, 2))
    gq_t = jnp.tile(qn_g, (1, H_A))[:, None, :]
    gk_t = jnp.tile(kn_g, (1, H_A))[:, None, :]
    lw = {"conv_w": conv_w, "conv_b": conv_b[:, None, :], "gate_bias": _gate_lanes(i_bias, f_bias)[:, None, :],
          "hn_g": hn_g[:, None, :]}
    sg3 = sgu_g[:, None, :]
    sgu_bt = jnp.transpose(sgu_b, (0, 2, 1))
    zeros_conv = jnp.zeros((1, bp, CONV_W - 1, 2 * D_C), F32)
    zeros_c = jnp.zeros((1, bp, H_C, DH_C, DH_C), F32)
    zeros_n = jnp.zeros((1, bp, H_C, DH_C), F32)
    zeros_m = jnp.zeros((1, bp, SUBLANES, LANES), F32)
    m0_all = jnp.broadcast_to(_pad_heads(state_m.reshape(depth * bd, H_C))[:, :, None],
                              (depth * bd, SUBLANES, LANES)).reshape(depth, bd, SUBLANES, LANES)
    keep = min(WIN_MAX, seq)
    pk_all = jnp.zeros((depth, bp, D_A, keep), F32)
    pv_all = jnp.zeros((depth, bp, D_A, keep), F32)
    w8 = sgu_w[:, :, :t_new, :t_new] * jnp.tril(jnp.ones((t_new, t_new), F32))
    wexp = jnp.repeat(jnp.transpose(w8, (0, 3, 2, 1)), C_B, axis=3)
    bexp = jnp.repeat(jnp.transpose(sgu_b[:, :, :t_new], (0, 2, 1)), C_B, axis=2)

    xp = x_prompt.reshape(bp * seq, D_MODEL)
    xs = x_sample.reshape(bd * t_new, D_MODEL)
    outs = [[] for _ in range(13)]
    mod4 = jnp.transpose(mod.reshape(depth, mod.shape[1], 3, D_MODEL), (0, 2, 1, 3))
    mod_p = mod4[:, :, :bp, None, :]
    mod_s = jnp.repeat(mod4[:, :, bp:n_c], t_new, axis=2)[:, :, None]
    norm_g3 = norm_g[:, None, :]
    hp = _norm(xp, mod_p, norm_g3, 0, 512)
    hs = _norm(xs, mod_s, norm_g3, 0, bd * t_new)
    for l in range(depth):
        last = l == depth - 1

        proj, gates = _inproj(hp, w_in_t, w_gate, l, 2048)
        proj = proj.reshape(bp, seq, D_MAIN)
        gates = gates.reshape(bp, seq, LANES)
        ya, yb, yc, pk_all, pv_all, ncv, nc_, nn_, nm = _mixers(
            proj, gates, lw, l,
            lambda pr: _attn_prompt(pr, gq_t, gk_t, bias_p, pk_all, pv_all, l),
            lambda pr: _sgu_prompt(pr, sg3, sgu_w, sgu_bt, l),
            zeros_conv, zeros_c, zeros_n, zeros_m, 0, CHUNK_C)
        xp, hp = _outproj(ya.reshape(bp * seq, D_A), yb.reshape(bp * seq, D_B), yc.reshape(bp * seq, D_C),
                          xp, mod_p, norm_g3, w_out_bf, l, 512, not last)
        for i, a in enumerate((ncv, nc_, nn_, nm)):
            outs[2 + i].append(a)

        proj_s, gates_s = _inproj(hs, w_in_t, w_gate, l, bd * t_new, INPROJ_TN_SAMPLE)
        proj_s = proj_s.reshape(bd, t_new, D_MAIN)
        gates_s = gates_s.reshape(bd, t_new, LANES)
        ya, sgu_out, yc, nk, nv, ncv, nc_, nn_, nm = _mixers(
            proj_s, gates_s, lw, l,
            lambda pr: _attn_sample(pr, cache_kt, cache_vt, gq_t, gk_t, bias_s, l),
            lambda pr: _sgu_sample(pr, sg3, wexp, bexp, l),
            state_conv, state_C, state_n, m0_all, l, t_new)
        yb, vn = sgu_out
        xs, hs = _outproj(ya.reshape(bd * t_new, D_A), yb.reshape(bd * t_new, D_B), yc.reshape(bd * t_new, D_C),
                          xs, mod_s, norm_g3, w_out_bf, l, bd * t_new, not last)
        for i, a in enumerate((nk.reshape(bd, t_new, H_A, HEAD_DIM_A), nv.reshape(bd, t_new, H_A, HEAD_DIM_A),
                               vn, ncv, nc_, nn_, nm)):
            outs[6 + i].append(a)

    stacked = [jnp.stack(o) for o in outs[2:]]
    p_k, p_v = (jnp.transpose(a.reshape(depth, bp, H_A, HEAD_DIM_A, keep), (0, 1, 4, 2, 3)) for a in (pk_all, pv_all))
    return (xp.reshape(bp, seq, D_MODEL), xs.reshape(bd, t_new, D_MODEL), p_k, p_v, *stacked)
```

```python
import functools
import math

import numpy as np
import jax
import jax.numpy as jnp
from jax import lax
from jax.experimental import pallas as pl
from jax.experimental.pallas import tpu as pltpu

F32 = jnp.float32
BF16 = jnp.bfloat16

D_MODEL = 2048
HEAD_DIM_A = 64
D_A = 768
H_A = 12
D_B = 512
G_B = 4
C_B = 128
CHUNK_B = 128
D_C = 768
DH_C = 128
H_C = 6
CHUNK_C = 128
CONV_W = 4
PATTERNS = ((128, 1), (512, 4), (2048, 16))
N_PAT = len(PATTERNS)
WIN_MAX = 2048
BLK_A = 128
N_BUCKETS = 32
MAX_DIST = 2048
EPS = 1e-6
D_MAIN = 4 * D_A + 3 * D_B + 2 * D_C + 3 * D_C
D_IN = D_MAIN + 2 * H_C

LANES = 128
SUBLANES = 8
VMEM_LIMIT = 56 * 1024 * 1024

NEG = -1e30
LOG2E = 1.4426950408889634
PAD_A = BLK_A * PATTERNS[-1][1]
HPAIR = LANES // HEAD_DIM_A

COL_Q, COL_K, COL_V, COL_Z = 0, D_A // LANES, 2 * D_A // LANES, 3 * D_A // LANES
COL_UB, COL_VB, COL_ZB = 4 * D_A // D_B, 4 * D_A // D_B + 1, 4 * D_A // D_B + 2
COL_QK = (4 * D_A + 3 * D_B) // (2 * D_C)
COL_VC = (4 * D_A + 3 * D_B + 2 * D_C) // D_C
COL_OC, COL_ZC = COL_VC + 1, COL_VC + 2


def _cparams(sem):
    return pltpu.CompilerParams(dimension_semantics=sem, vmem_limit_bytes=VMEM_LIMIT)


def _silu(x):
    return x * jax.nn.sigmoid(x)


def _bdot_nt(a, b):
    return lax.dot_general(a.astype(BF16), b.astype(BF16), (((1,), (1,)), ((), ())),
                           preferred_element_type=F32)


def _bdot_tn(a, b):
    return lax.dot_general(a.astype(BF16), b.astype(BF16), (((0,), (0,)), ((), ())),
                           preferred_element_type=F32)


def _bucket_np(dist):
    max_exact = N_BUCKETS // 2
    df = np.maximum(dist, 1).astype(np.float32)
    large = max_exact + (np.log(df / np.float32(max_exact)) / np.float32(math.log(MAX_DIST / max_exact))
                         * np.float32(N_BUCKETS - max_exact)).astype(np.int32)
    return np.where(dist < max_exact, dist, np.minimum(large, N_BUCKETS - 1)).astype(np.int32)


def _prompt_bucket_table():
    qi = np.arange(BLK_A)[:, None]
    ki = np.arange(2 * BLK_A)[None, :]
    j = qi + BLK_A - ki
    out = []
    for win, dil in PATTERNS:
        n_back = win // dil
        band = (j >= 0) & (j <= n_back)
        b = _bucket_np(np.clip(j, 0, n_back) * dil)
        out.append(np.where(band, b, -1))
        out.append(np.where(band & (ki >= BLK_A), b, -1))
    return np.stack(out).reshape(N_PAT * 2 * BLK_A, 2 * BLK_A).astype(np.int32)


def _sample_bucket_table(n_past, t_new):
    c = np.arange(n_past + LANES)[None, :]
    t = np.arange(t_new)[:, None]
    delta = n_past + t - c
    out = []
    for win, dil in PATTERNS:
        valid = (c < n_past + t_new) & (delta >= 0) & (delta % dil == 0) & (delta // dil <= win // dil)
        out.append(np.where(valid, _bucket_np(np.maximum(delta, 0)), -1))
    return np.stack(out).reshape(N_PAT * t_new, n_past + LANES).astype(np.int32)


def _bias_kernel(rb_ref, idx_ref, out_ref, *, scale, row_chunk):
    h = pl.program_id(0)

    def chunk(c, carry):
        rows = pl.ds(pl.multiple_of(c * row_chunk, row_chunk), row_chunk)
        idx = idx_ref[rows, :]
        out = jnp.full(idx.shape, NEG, F32)
        for b in range(N_BUCKETS):
            out = jnp.where(idx == b, rb_ref[h, b] * scale, out)
        out_ref[rows, :] = out
        return carry
    lax.fori_loop(0, idx_ref.shape[0] // row_chunk, chunk, 0)


def _expand_bias(rel_bias_t, idx, row_chunk, scale):
    rows, cols = idx.shape
    return pl.pallas_call(
        functools.partial(_bias_kernel, scale=scale, row_chunk=row_chunk),
        out_shape=jax.ShapeDtypeStruct((H_A, rows, cols), F32),
        grid=(H_A,),
        in_specs=[pl.BlockSpec(memory_space=pltpu.SMEM),
                  pl.BlockSpec((rows, cols), lambda h: (0, 0))],
        out_specs=pl.BlockSpec((None, rows, cols), lambda h: (h, 0, 0)),
        compiler_params=_cparams(("arbitrary",)),
        name="bias_expand",
    )(rel_bias_t, idx)


def _ada_kernel(c_ref, w_ref, b_ref, o_ref):
    c = c_ref[...]
    a = _silu(c)
    w = w_ref[...]
    a_hi = a.astype(BF16)
    a_lo = (a - a_hi.astype(F32)).astype(BF16)
    w_hi = w.astype(BF16)
    w_lo = (w - w_hi.astype(F32)).astype(BF16)
    acc = jnp.dot(a_hi, w_hi, preferred_element_type=F32)
    acc += jnp.dot(a_hi, w_lo, preferred_element_type=F32)
    acc += jnp.dot(a_lo, w_hi, preferred_element_type=F32)
    o_ref[...] = acc + b_ref[...]


def _ada_mod(c_all, ada_w, ada_b):
    depth = ada_w.shape[0]
    rows = c_all.shape[0]
    tn = 768
    n = 3 * D_MODEL
    return pl.pallas_call(
        _ada_kernel,
        out_shape=jax.ShapeDtypeStruct((depth, rows, n), F32),
        grid=(depth, n // tn),
        in_specs=[pl.BlockSpec((rows, D_MODEL), lambda l, j: (0, 0)),
                  pl.BlockSpec((None, D_MODEL, tn), lambda l, j: (l, 0, j)),
                  pl.BlockSpec((None, 1, tn), lambda l, j: (l, 0, j))],
        out_specs=pl.BlockSpec((None, rows, tn), lambda l, j: (l, 0, j)),
        compiler_params=_cparams(("arbitrary", "arbitrary")),
        name="ada_mod",
    )(c_all, ada_w, ada_b.reshape(depth, 1, n))


def _norm_mod(x, g, sc, sh):
    ms = jnp.sum(x * x, axis=-1, keepdims=True) * (1.0 / D_MODEL)
    return ((x * lax.rsqrt(ms + EPS) * g) * (1.0 + sc) + sh).astype(BF16)


def _norm_kernel(x_ref, sc_ref, sh_ref, g_ref, h_ref):
    h_ref[...] = _norm_mod(x_ref[...], g_ref[...], sc_ref[...], sh_ref[...])


MOD_SHIFT, MOD_SCALE, MOD_GATE = 0, 1, 2


def _mod_spec(mod5, layer, which, tiles_per_group):
    return pl.BlockSpec((None, None, None, mod5.shape[3], D_MODEL),
                        lambda i: (layer, which, i // tiles_per_group, 0, 0))


def _norm(x2d, mod5, norm_g3, layer, tm):
    m = x2d.shape[0]
    tiles_per_group = m // tm // mod5.shape[2]
    return pl.pallas_call(
        _norm_kernel,
        out_shape=jax.ShapeDtypeStruct((m, D_MODEL), BF16),
        grid=(m // tm,),
        in_specs=[pl.BlockSpec((tm, D_MODEL), lambda i: (i, 0)),
                  _mod_spec(mod5, layer, MOD_SCALE, tiles_per_group),
                  _mod_spec(mod5, layer, MOD_SHIFT, tiles_per_group),
                  pl.BlockSpec((None, 1, D_MODEL), lambda i: (layer, 0, 0))],
        out_specs=pl.BlockSpec((tm, D_MODEL), lambda i: (i, 0)),
        compiler_params=_cparams(("arbitrary",)),
        name="norm",
    )(x2d, mod5, mod5, norm_g3)


INPROJ_TN = 768
INPROJ_TN_SAMPLE = 4224


def _inproj_kernel(h_ref, w_ref, wg_ref, proj_ref, gates_ref):
    @pl.when(pl.program_id(1) == 0)
    def _():
        gates_ref[...] = jnp.dot(h_ref[...], wg_ref[...].astype(BF16), preferred_element_type=F32)

    proj_ref[...] = lax.dot_general(h_ref[...], w_ref[...], (((1,), (1,)), ((), ())),
                                    preferred_element_type=F32)


UNPACK_PITCH = 72


def _unpack_kernel(x_hbm, o_ref, buf_a, buf_b, sem, *, depth, k_tiles, tn):
    i = pl.program_id(0)
    rows_per_col = k_tiles * depth

    def copy(tile, n, buf, slot):
        return pltpu.make_async_copy(x_hbm.at[tile * tn + n], buf.at[pl.ds(n * UNPACK_PITCH, rows_per_col)],
                                     sem.at[slot])

    def start_all(tile, buf, slot):
        def body(n, carry):
            copy(tile, n, buf, slot).start()
            return carry
        lax.fori_loop(0, tn, body, 0, unroll=8)

    def wait_all(tile, buf, slot):
        def body(n, carry):
            copy(tile, n, buf, slot).wait()
            return carry
        lax.fori_loop(0, tn, body, 0, unroll=8)

    def convert(buf, half):
        for l in range(depth):
            for kt in range(k_tiles):
                rows = pl.ds(kt * depth + l, tn, stride=UNPACK_PITCH)
                o_ref[l, half * tn:(half + 1) * tn, kt * LANES:(kt + 1) * LANES] = buf[rows, :].astype(BF16)

    @pl.when(i == 0)
    def _():
        start_all(0, buf_a, 0)

    start_all(2 * i + 1, buf_b, 1)
    wait_all(2 * i, buf_a, 0)
    convert(buf_a, 0)

    @pl.when(i + 1 < pl.num_programs(0))
    def _():
        start_all(2 * i + 2, buf_a, 0)

    wait_all(2 * i + 1, buf_b, 1)
    convert(buf_b, 1)


def _unpack_w_in(w_in):
    depth, d, d_in = w_in.shape
    k_tiles = d // LANES
    tn = LANES
    assert D_MAIN % (2 * tn) == 0 and k_tiles * depth <= UNPACK_PITCH
    cols = w_in.reshape(depth, k_tiles, LANES, d_in).transpose(3, 1, 0, 2).reshape(d_in, k_tiles * depth, LANES)
    return pl.pallas_call(
        functools.partial(_unpack_kernel, depth=depth, k_tiles=k_tiles, tn=tn),
        out_shape=jax.ShapeDtypeStruct((depth, D_MAIN, d), BF16),
        grid=(D_MAIN // (2 * tn),),
        in_specs=[pl.BlockSpec(memory_space=pl.ANY)],
        out_specs=pl.BlockSpec((depth, 2 * tn, d), lambda i: (0, i, 0)),
        scratch_shapes=[pltpu.VMEM((tn * UNPACK_PITCH, LANES), F32),
                        pltpu.VMEM((tn * UNPACK_PITCH, LANES), F32),
                        pltpu.SemaphoreType.DMA((2,))],
        compiler_params=_cparams(("arbitrary",)),
        name="unpack_w_in",
    )(cols)


def _inproj(h2d, w_in_t, w_gate, layer, tm, tn=INPROJ_TN):
    m = h2d.shape[0]
    return pl.pallas_call(
        _inproj_kernel,
        out_shape=(jax.ShapeDtypeStruct((m, D_MAIN), F32), jax.ShapeDtypeStruct((m, LANES), F32)),
        grid=(m // tm, D_MAIN // tn),
        in_specs=[pl.BlockSpec((tm, D_MODEL), lambda i, j: (i, 0)),
                  pl.BlockSpec((None, tn, D_MODEL), lambda i, j: (layer, j, 0)),
                  pl.BlockSpec((None, D_MODEL, LANES), lambda i, j: (layer, 0, 0))],
        out_specs=(pl.BlockSpec((tm, tn), lambda i, j: (i, j)),
                   pl.BlockSpec((tm, LANES), lambda i, j: (i, 0))),
        compiler_params=_cparams(("arbitrary", "arbitrary")),
        name="inproj",
    )(h2d, w_in_t, w_gate)


def _outproj_kernel(ya_ref, yb_ref, yc_ref, x_ref, gate_ref, w_ref, *rest, emit_next):
    y = jnp.dot(ya_ref[...], w_ref[0:D_A, :], preferred_element_type=F32)
    y += jnp.dot(yb_ref[...], w_ref[D_A:D_A + D_B, :], preferred_element_type=F32)
    y += jnp.dot(yc_ref[...], w_ref[D_A + D_B:, :], preferred_element_type=F32)
    x_new = x_ref[...] + gate_ref[...] * y
    if emit_next:
        sc_ref, sh_ref, g_ref, o_ref, h_ref = rest
        h_ref[...] = _norm_mod(x_new, g_ref[...], sc_ref[...], sh_ref[...])
    else:
        (o_ref,) = rest
    o_ref[...] = x_new


def _outproj(ya, yb, yc, x2d, mod5, norm_g3, w_out_bf, layer, tm, emit_next):
    m = x2d.shape[0]
    tiles_per_group = m // tm // mod5.shape[2]
    row = lambda width: pl.BlockSpec((tm, width), lambda i: (i, 0))
    in_specs = [row(D_A), row(D_B), row(D_C), row(D_MODEL),
                _mod_spec(mod5, layer, MOD_GATE, tiles_per_group),
                pl.BlockSpec((None, D_MODEL, D_MODEL), lambda i: (layer, 0, 0))]
    args = [ya, yb, yc, x2d, mod5, w_out_bf]
    out_shape = [jax.ShapeDtypeStruct((m, D_MODEL), F32)]
    out_specs = [row(D_MODEL)]
    if emit_next:
        in_specs += [_mod_spec(mod5, layer + 1, MOD_SCALE, tiles_per_group),
                     _mod_spec(mod5, layer + 1, MOD_SHIFT, tiles_per_group),
                     pl.BlockSpec((None, 1, D_MODEL), lambda i: (layer + 1, 0, 0))]
        args += [mod5, mod5, norm_g3]
        out_shape.append(jax.ShapeDtypeStruct((m, D_MODEL), BF16))
        out_specs.append(row(D_MODEL))
    out = pl.pallas_call(
        functools.partial(_outproj_kernel, emit_next=emit_next),
        out_shape=tuple(out_shape),
        grid=(m // tm,),
        in_specs=in_specs,
        out_specs=tuple(out_specs),
        compiler_params=_cparams(("arbitrary",)),
        name="outproj",
    )(*args)
    return out if emit_next else (out[0], None)


def _head_mean_matrix():
    row = lax.broadcasted_iota(jnp.int32, (LANES, LANES), 0)
    col = lax.broadcasted_iota(jnp.int32, (LANES, LANES), 1)
    same_head = (row < HEAD_DIM_A) == (col < HEAD_DIM_A)
    return jnp.where(same_head, 1.0 / HEAD_DIM_A, 0.0).astype(BF16)


def _head_norm(x, g, mean_mat):
    x2 = x * x
    hi = x2.astype(BF16)
    lo = (x2 - hi.astype(F32)).astype(BF16)
    ms = (jnp.dot(hi, mean_mat, preferred_element_type=F32)
          + jnp.dot(lo, mean_mat, preferred_element_type=F32))
    return x * lax.rsqrt(ms + EPS) * g


def _attn_prompt_kernel(q_ref, k_ref, v_ref, z_ref, gq_ref, gk_ref, bias_ref, pk_all_hbm, pv_all_hbm,
                        y_ref, pk_ref, pv_ref,
                        qn_s, kp_s, vp_s, ones_s, o_s, l_s, m_s, *, seq, keep, unroll):
    norm_rows = 512
    mean_mat = _head_mean_matrix()

    kp_s[0:PAD_A, :] = jnp.zeros((PAD_A, LANES), F32)
    vp_s[0:PAD_A, :] = jnp.zeros((PAD_A, LANES), F32)
    ones_s[...] = jnp.ones(ones_s.shape, BF16)

    def norm_chunk(c, carry):
        r0 = pl.multiple_of(c * norm_rows, norm_rows)
        rows = pl.ds(r0, norm_rows)
        qn_s[rows, :] = _head_norm(q_ref[rows, :], gq_ref[...], mean_mat) * (HEAD_DIM_A ** -0.5 * LOG2E)
        kp_s[pl.ds(PAD_A + r0, norm_rows), :] = _head_norm(k_ref[rows, :], gk_ref[...], mean_mat)
        vp_s[pl.ds(PAD_A + r0, norm_rows), :] = v_ref[rows, :]
        return carry
    lax.fori_loop(0, seq // norm_rows, norm_chunk, 0, unroll=True)

    for c in range(keep // LANES):
        src = slice(PAD_A + seq - keep + c * LANES, PAD_A + seq - keep + (c + 1) * LANES)
        pk_ref[:, c * LANES:(c + 1) * LANES] = kp_s[src, :].T
        pv_ref[:, c * LANES:(c + 1) * LANES] = vp_s[src, :].T

    left_b = lax.broadcasted_iota(jnp.int32, (BLK_A, LANES), 1) < HEAD_DIM_A

    for p, (win, dil) in enumerate(PATTERNS):
        blocks_per_residue = seq // (dil * BLK_A)

        def block(i, carry, p=p, dil=dil, blocks_per_residue=blocks_per_residue):
            r = i // blocks_per_residue
            n = i % blocks_per_residue
            q_start = r + n * (BLK_A * dil)
            k_start = PAD_A + q_start - BLK_A * dil
            if dil == 1:
                q_rows = pl.ds(pl.multiple_of(q_start, BLK_A), BLK_A)
                k_rows = pl.ds(pl.multiple_of(k_start, BLK_A), 2 * BLK_A)
            else:
                q_rows = pl.ds(q_start, BLK_A, stride=dil)
                k_rows = pl.ds(k_start, 2 * BLK_A, stride=dil)
            first = jnp.where(n == 0, 1, 0)
            q = qn_s[q_rows, :]
            k = kp_s[k_rows, :].astype(BF16)
            v = vp_s[k_rows, :].astype(BF16)
            rhs = jnp.concatenate([v, ones_s[...]], axis=1)
            q2 = jnp.concatenate([jnp.where(left_b, q, 0.0), jnp.where(left_b, 0.0, q)], axis=0)
            s2 = _bdot_nt(q2, k)
            es, ms = [], []
            for h in range(HPAIR):
                s = s2[h * BLK_A:(h + 1) * BLK_A, :] + bias_ref[h, p, first]
                mh = jnp.max(s, axis=-1, keepdims=True)
                es.append(jnp.exp2(s - mh).astype(BF16))
                ms.append(mh)
            res = jnp.dot(jnp.concatenate(es, axis=0), rhs, preferred_element_type=F32)
            o_s[p, q_rows, :] = jnp.where(left_b, res[0:BLK_A, 0:LANES], res[BLK_A:, 0:LANES])
            l_s[p, q_rows, :] = jnp.where(left_b, res[0:BLK_A, LANES:], res[BLK_A:, LANES:])
            m_s[p, q_rows, :] = jnp.where(left_b, ms[0], ms[1])
            return carry
        lax.fori_loop(0, seq // BLK_A, block, 0, unroll=unroll)

    def out_chunk(c, carry):
        rows = pl.ds(pl.multiple_of(c * norm_rows, norm_rows), norm_rows)
        m = [m_s[p, rows, :] for p in range(N_PAT)]
        top = functools.reduce(jnp.maximum, m)
        w = [jnp.exp2(x - top) for x in m]
        num = functools.reduce(lambda a, b: a + b, [w[p] * o_s[p, rows, :] for p in range(N_PAT)])
        den = functools.reduce(lambda a, b: a + b, [w[p] * l_s[p, rows, :] for p in range(N_PAT)])
        z = z_ref[rows, :]
        y_ref[rows, :] = (num * z / (den * (1.0 + jnp.exp(-z)))).astype(BF16)
        return carry
    lax.fori_loop(0, seq // norm_rows, out_chunk, 0)


def _attn_prompt(proj, gq_t, gk_t, bias_p, pk_all, pv_all, layer):
    b, seq, _ = proj.shape
    keep = pk_all.shape[-1]
    assert seq % (BLK_A * PATTERNS[-1][1]) == 0 and keep == min(WIN_MAX, seq)
    col = lambda base: pl.BlockSpec((None, seq, LANES), lambda bi, hp: (bi, 0, base + hp))
    vec = pl.BlockSpec((None, 1, LANES), lambda bi, hp: (layer, 0, hp))
    window = pl.BlockSpec((None, None, LANES, keep), lambda bi, hp: (layer, bi, hp, 0))
    untouched = pl.BlockSpec(memory_space=pl.ANY)
    return pl.pallas_call(
        functools.partial(_attn_prompt_kernel, seq=seq, keep=keep, unroll=seq // BLK_A),
        out_shape=(jax.ShapeDtypeStruct((b, seq, D_A), BF16),
                   jax.ShapeDtypeStruct(pk_all.shape, F32),
                   jax.ShapeDtypeStruct(pv_all.shape, F32)),
        grid=(b, H_A // HPAIR),
        in_specs=[col(COL_Q), col(COL_K), col(COL_V), col(COL_Z), vec, vec,
                  pl.BlockSpec((HPAIR, N_PAT, 2, BLK_A, 2 * BLK_A), lambda bi, hp: (hp, 0, 0, 0, 0)),
                  untouched, untouched],
        out_specs=(pl.BlockSpec((None, seq, LANES), lambda bi, hp: (bi, 0, hp)), window, window),
        input_output_aliases={7: 1, 8: 2},
        scratch_shapes=[pltpu.VMEM((seq, LANES), F32),
                        pltpu.VMEM((PAD_A + seq, LANES), F32),
                        pltpu.VMEM((PAD_A + seq, LANES), F32),
                        pltpu.VMEM((2 * BLK_A, LANES), BF16),
                        pltpu.VMEM((N_PAT, seq, LANES), F32),
                        pltpu.VMEM((N_PAT, seq, LANES), F32),
                        pltpu.VMEM((N_PAT, seq, LANES), F32)],
        compiler_params=_cparams(("arbitrary", "arbitrary")),
        name="attn_prompt",
    )(proj, proj, proj, proj, gq_t, gk_t, bias_p, pk_all, pv_all)


def _attn_sample_kernel(q_ref, k_ref, v_ref, z_ref, ck_ref, cv_ref, gq_ref, gk_ref, bias_ref,
                        y_ref, nk_ref, nv_ref, *, n_past, t_new):
    q_all = q_ref[...]
    k_all = k_ref[...]
    v_all = v_ref[...]
    z_all = z_ref[...]
    nv_ref[...] = v_all
    pad_rows = jnp.zeros((LANES - t_new, HEAD_DIM_A), F32)

    def norm(x, g):
        ms = jnp.sum(x * x, axis=-1, keepdims=True) * (1.0 / HEAD_DIM_A)
        return x * lax.rsqrt(ms + EPS) * g

    for h in range(H_A):
        cols = slice(h * HEAD_DIM_A, (h + 1) * HEAD_DIM_A)
        qn = (norm(q_all[:, cols], gq_ref[:, cols]) * (HEAD_DIM_A ** -0.5)).astype(BF16)
        kn = norm(k_all[:, cols], gk_ref[:, cols])
        nk_ref[:, cols] = kn
        k_new = jnp.concatenate([kn, pad_rows], axis=0).astype(BF16)
        v_new = jnp.concatenate([v_all[:, cols], pad_rows], axis=0).astype(BF16)
        k_t = ck_ref[h].astype(BF16)
        v_t = cv_ref[h].astype(BF16)
        s_c = jnp.dot(qn, k_t, preferred_element_type=F32)
        s_n = _bdot_nt(qn, k_new)
        sc = [s_c + bias_ref[h, p, :, 0:n_past] for p in range(N_PAT)]
        sn = [s_n + bias_ref[h, p, :, n_past:] for p in range(N_PAT)]
        m = functools.reduce(jnp.maximum, [jnp.max(x, axis=-1, keepdims=True) for x in sc + sn])
        e_c = functools.reduce(lambda a, b: a + b, [jnp.exp(x - m) for x in sc])
        e_n = functools.reduce(lambda a, b: a + b, [jnp.exp(x - m) for x in sn])
        den = jnp.sum(e_c, axis=-1, keepdims=True) + jnp.sum(e_n, axis=-1, keepdims=True)
        o = _bdot_nt(e_c, v_t) + jnp.dot(e_n.astype(BF16), v_new, preferred_element_type=F32)
        y_ref[:, cols] = (o / den * _silu(z_all[:, cols])).astype(BF16)


def _attn_sample(proj_s, cache_kt, cache_vt, gq, gk, bias_s, layer):
    bd, t_new, _ = proj_s.shape
    n_past = cache_kt.shape[-1]
    col = lambda base: pl.BlockSpec((None, t_new, D_A), lambda bi: (bi, 0, base))
    vec = pl.BlockSpec((None, 1, D_A), lambda bi: (layer, 0, 0))
    cache = pl.BlockSpec((None, None, H_A, HEAD_DIM_A, n_past), lambda bi: (layer, bi, 0, 0, 0))
    out = pl.BlockSpec((None, t_new, D_A), lambda bi: (bi, 0, 0))
    return pl.pallas_call(
        functools.partial(_attn_sample_kernel, n_past=n_past, t_new=t_new),
        out_shape=(jax.ShapeDtypeStruct((bd, t_new, D_A), BF16),
                   jax.ShapeDtypeStruct((bd, t_new, D_A), F32),
                   jax.ShapeDtypeStruct((bd, t_new, D_A), F32)),
        grid=(bd,),
        in_specs=[col(0), col(1), col(2), col(3), cache, cache, vec, vec,
                  pl.BlockSpec((H_A, N_PAT, t_new, n_past + LANES), lambda bi: (0, 0, 0, 0))],
        out_specs=(out, out, out),
        compiler_params=_cparams(("arbitrary",)),
        name="attn_sample",
    )(proj_s, proj_s, proj_s, proj_s, cache_kt, cache_vt, gq, gk, bias_s)


def _sgu_prompt_kernel(u_ref, v_ref, z_ref, g_ref, w_ref, bt_ref, y_ref, *, chunks):
    row = lax.broadcasted_iota(jnp.int32, (CHUNK_B, CHUNK_B), 0)
    colm = lax.broadcasted_iota(jnp.int32, (CHUNK_B, CHUNK_B), 1)
    tril = row >= colm
    ws = [jnp.where(tril, w_ref[g], 0.0).astype(BF16) for g in range(G_B)]
    for c in range(chunks):
        rows = slice(c * CHUNK_B, (c + 1) * CHUNK_B)
        v = v_ref[rows, :]
        ms = jnp.sum(v * v, axis=-1, keepdims=True) * (1.0 / D_B)
        vn = (v * lax.rsqrt(ms + EPS) * g_ref[...]).astype(BF16)
        for g in range(G_B):
            cols = slice(g * C_B, (g + 1) * C_B)
            mix = jnp.dot(ws[g], vn[:, cols], preferred_element_type=F32) + bt_ref[:, g:g + 1]
            y_ref[rows, cols] = (u_ref[rows, cols] * mix * _silu(z_ref[rows, cols])).astype(BF16)


def _sgu_prompt(proj, sgu_g3, sgu_w, sgu_bt, layer):
    b, seq, _ = proj.shape
    ts = 1024
    col = lambda base: pl.BlockSpec((None, ts, D_B), lambda bi, i: (bi, i, base))
    return pl.pallas_call(
        functools.partial(_sgu_prompt_kernel, chunks=ts // CHUNK_B),
        out_shape=jax.ShapeDtypeStruct((b, seq, D_B), BF16),
        grid=(b, seq // ts),
        in_specs=[col(COL_UB), col(COL_VB), col(COL_ZB),
                  pl.BlockSpec((None, 1, D_B), lambda bi, i: (layer, 0, 0)),
                  pl.BlockSpec((None, G_B, CHUNK_B, CHUNK_B), lambda bi, i: (layer, 0, 0, 0)),
                  pl.BlockSpec((None, CHUNK_B, G_B), lambda bi, i: (layer, 0, 0))],
        out_specs=pl.BlockSpec((None, ts, D_B), lambda bi, i: (bi, i, 0)),
        compiler_params=_cparams(("arbitrary", "arbitrary")),
        name="sgu_prompt",
    )(proj, proj, proj, sgu_g3, sgu_w, sgu_bt)


def _sgu_sample_kernel(u_ref, v_ref, z_ref, g_ref, wexp_ref, bexp_ref, y_ref, vn_ref, *, t_new):
    v = v_ref[...]
    ms = jnp.sum(v * v, axis=-1, keepdims=True) * (1.0 / D_B)
    vn = v * lax.rsqrt(ms + EPS) * g_ref[...]
    vn_ref[...] = vn
    mix = bexp_ref[...]
    for s in range(t_new):
        mix = mix + wexp_ref[s] * vn[s:s + 1, :]
    y_ref[...] = (u_ref[...] * mix * _silu(z_ref[...])).astype(BF16)


def _sgu_sample(proj_s, sgu_g3, wexp, bexp, layer):
    bd, t_new, _ = proj_s.shape
    col = lambda base: pl.BlockSpec((None, t_new, D_B), lambda bi: (bi, 0, base))
    return pl.pallas_call(
        functools.partial(_sgu_sample_kernel, t_new=t_new),
        out_shape=(jax.ShapeDtypeStruct((bd, t_new, D_B), BF16),
                   jax.ShapeDtypeStruct((bd, t_new, D_B), F32)),
        grid=(bd,),
        in_specs=[col(COL_UB), col(COL_VB), col(COL_ZB),
                  pl.BlockSpec((None, 1, D_B), lambda bi: (layer, 0, 0)),
                  pl.BlockSpec((None, t_new, t_new, D_B), lambda bi: (layer, 0, 0, 0)),
                  pl.BlockSpec((None, t_new, D_B), lambda bi: (layer, 0, 0))],
        out_specs=(pl.BlockSpec((None, t_new, D_B), lambda bi: (bi, 0, 0)),
                   pl.BlockSpec((None, t_new, D_B), lambda bi: (bi, 0, 0))),
        compiler_params=_cparams(("arbitrary",)),
        name="sgu_sample",
    )(proj_s, proj_s, proj_s, sgu_g3, wexp, bexp)


def _log_sigmoid(x):
    return jnp.minimum(x, 0.0) - jnp.log1p(jnp.exp(-jnp.abs(x)))


MLSTM_SEQS_PER_STEP = 2
MLSTM_ROW0 = SUBLANES
MLSTM_CARRY = CONV_W - 1


def _mlstm_init(cbuf_ref, c0_ref, n0_ref, m0_ref, xp_s, vp_s, gp_s, caug_s, m_s):
    xp_s[...] = jnp.zeros(xp_s.shape, F32)
    vp_s[...] = jnp.zeros(vp_s.shape, F32)
    gp_s[...] = jnp.zeros(gp_s.shape, F32)
    xp_s[MLSTM_ROW0 - MLSTM_CARRY:MLSTM_ROW0, :] = cbuf_ref[...]
    n0 = n0_ref[...]
    n0_cols = jnp.concatenate([n0, jnp.zeros((DH_C - n0.shape[0], DH_C), F32)], axis=0).T
    for h in range(H_C):
        caug_s[h, :, 0:DH_C] = c0_ref[h]
        caug_s[h, :, DH_C:] = jnp.broadcast_to(n0_cols[:, h:h + 1], (DH_C, DH_C))
    m_s[...] = m0_ref[...]


def _mlstm_final(new_carry, conv_out_ref, c_out_ref, n_out_ref, m_out_ref, caug_s, m_s):
    conv_out_ref[...] = new_carry
    lane = lax.broadcasted_iota(jnp.int32, (DH_C, LANES), 1)
    n_cols = jnp.zeros((DH_C, LANES), F32)
    for h in range(H_C):
        c_out_ref[h] = caug_s[h, :, 0:DH_C]
        n_cols = jnp.where(lane == h, caug_s[h, :, DH_C:], n_cols)
    n_out_ref[...] = n_cols.T[0:SUBLANES, :]
    m_out_ref[...] = m_s[...]


def _mlstm_chunk(qk_ref, v_ref, o_ref, z_ref, gt_ref, cw_ref, cb_ref, gb_ref, hn_ref, y_ref,
                 xp_s, vp_s, gp_s, caug_s, m_s, *, t_rows):
    L = CHUNK_C
    base = MLSTM_ROW0
    carry_rows = MLSTM_CARRY
    row_l = lax.broadcasted_iota(jnp.int32, (L, LANES), 0)
    lane_l = lax.broadcasted_iota(jnp.int32, (L, LANES), 1)
    zero_rows = jnp.zeros((L - SUBLANES, LANES), F32)

    xp_s[base:base + t_rows, :] = qk_ref[...]
    vp_s[0:t_rows, :] = v_ref[...]
    gp_s[0:t_rows, :] = gt_ref[...]

    live = -(-t_rows // SUBLANES) * SUBLANES
    acc = cb_ref[...] + cw_ref[CONV_W - 1:CONV_W, :] * xp_s[base:base + live, :]
    for j in range(CONV_W - 1):
        off = base - carry_rows + j
        acc = acc + cw_ref[j:j + 1, :] * xp_s[off:off + live, :]
    qk = _silu(acc)
    if live < L:
        qk = jnp.concatenate([qk, jnp.zeros((L - live, qk.shape[1]), F32)], axis=0)
    new_carry = xp_s[base + t_rows - carry_rows:base + t_rows, :]
    xp_s[base - carry_rows:base, :] = new_carry

    gates_t = (gp_s[...] + gb_ref[...]).T
    lane_t = lax.broadcasted_iota(jnp.int32, (SUBLANES, L), 1)
    real_t = lane_t < t_rows
    i_t = jnp.where(real_t, gates_t[0:SUBLANES, :], NEG)
    logf_t = jnp.where(real_t, _log_sigmoid(gates_t[SUBLANES:2 * SUBLANES, :]), 0.0)
    upper = (row_l <= lane_l).astype(BF16)
    f1 = logf_t.astype(BF16)
    r1 = logf_t - f1.astype(F32)
    f2 = r1.astype(BF16)
    f3 = (r1 - f2.astype(F32)).astype(BF16)
    b_t = (jnp.dot(f1, upper, preferred_element_type=F32) + jnp.dot(f2, upper, preferred_element_type=F32)
           + jnp.dot(f3, upper, preferred_element_type=F32))
    m_prev = m_s[...]
    c_t = i_t - b_t
    pm = c_t
    shift = 1
    while shift < L:
        pm = jnp.maximum(pm, jnp.where(lane_t >= shift, pltpu.roll(pm, shift, axis=1), NEG))
        shift *= 2
    d_t = -jnp.maximum(m_prev, pm)
    wi_t = jnp.exp(m_prev + d_t)
    em_t = jnp.exp(-(b_t - d_t))
    bl = jnp.broadcast_to(b_t[:, L - 1:L], (SUBLANES, L))
    g_t = bl - b_t + i_t
    m_new = jnp.maximum(bl + m_prev, jnp.broadcast_to(jnp.max(g_t, axis=1, keepdims=True), (SUBLANES, L)))
    ws_t = jnp.exp(g_t - m_new)
    wc = jnp.exp(bl + m_prev - m_new)
    m_s[...] = m_new
    stats = jnp.concatenate([d_t, wi_t, em_t, ws_t, zero_rows[0:L - 4 * SUBLANES]], axis=0).T

    causal = row_l >= lane_l
    ones_blk = jnp.ones((L, DH_C), BF16)
    v_all = vp_s[...]
    o_all = o_ref[...]
    z_all = z_ref[...]

    for h in range(H_C):
        cols = slice(h * DH_C, (h + 1) * DH_C)
        q = qk[:, cols].astype(BF16)
        k = qk[:, D_C + h * DH_C:D_C + (h + 1) * DH_C] * (DH_C ** -0.5)
        v_aug = jnp.concatenate([v_all[:, cols].astype(BF16), ones_blk], axis=1)
        d_col = stats[:, h:h + 1]
        wi_col = stats[:, SUBLANES + h:SUBLANES + h + 1]
        em_col = stats[:, 2 * SUBLANES + h:2 * SUBLANES + h + 1]
        ws_col = stats[:, 3 * SUBLANES + h:3 * SUBLANES + h + 1]
        c_prev = caug_s[h]

        a = jnp.exp(jnp.where(causal, d_col + c_t[h:h + 1, :], NEG)) * _bdot_nt(q, k)
        r1_ = jnp.dot(a.astype(BF16), v_aug, preferred_element_type=F32)
        r2_ = jnp.dot(q, c_prev.astype(BF16), preferred_element_type=F32)
        num = r1_[:, 0:DH_C] + wi_col * r2_[:, 0:DH_C]
        den = r1_[:, DH_C:] + wi_col * r2_[:, DH_C:]
        hh = num / jnp.maximum(jnp.abs(den), em_col)

        wc_row = jnp.concatenate([wc[h:h + 1, :], wc[h:h + 1, :]], axis=1)
        caug_s[h] = wc_row * c_prev + _bdot_tn(k * ws_col, v_aug)

        ms = jnp.sum(hh * hh, axis=-1, keepdims=True) * (1.0 / DH_C)
        hn = hh * lax.rsqrt(ms + EPS) * hn_ref[:, cols]
        z = z_all[:, cols]
        gate = z / ((1.0 + jnp.exp(-o_all[:, cols])) * (1.0 + jnp.exp(-z)))
        y_ref[:, cols] = (hn[0:t_rows, :] * gate).astype(BF16)
    return new_carry


def _mlstm_kernel(*refs, t_rows, nb, with_sgu):
    if with_sgu:
        (qk_b, v_b, o_b, z_b, gt_b, cbuf_b, c0_b, n0_b, m0_b, cw_ref, cb_ref, gb_ref, hn_ref,
         su_b, sv_b, sz_b, sg_ref, sw_ref, sbt_ref,
         y_b, conv_out_b, c_out_b, n_out_b, m_out_b, yb_b, xp_b, vp_b, gp_b, caug_b, m_b) = refs
    else:
        (qk_b, v_b, o_b, z_b, gt_b, cbuf_b, c0_b, n0_b, m0_b, cw_ref, cb_ref, gb_ref, hn_ref,
         y_b, conv_out_b, c_out_b, n_out_b, m_out_b, xp_b, vp_b, gp_b, caug_b, m_b) = refs
    ci = pl.program_id(1)
    seqs = range(nb)

    @pl.when(ci == 0)
    def _():
        for bi in seqs:
            _mlstm_init(cbuf_b.at[bi], c0_b.at[bi], n0_b.at[bi], m0_b.at[bi],
                        xp_b.at[bi], vp_b.at[bi], gp_b.at[bi], caug_b.at[bi], m_b.at[bi])

    carries = [_mlstm_chunk(qk_b.at[bi], v_b.at[bi], o_b.at[bi], z_b.at[bi], gt_b.at[bi],
                            cw_ref, cb_ref, gb_ref, hn_ref, y_b.at[bi],
                            xp_b.at[bi], vp_b.at[bi], gp_b.at[bi], caug_b.at[bi], m_b.at[bi], t_rows=t_rows)
               for bi in seqs]
    if with_sgu:
        for bi in seqs:
            _sgu_prompt_kernel(su_b.at[bi], sv_b.at[bi], sz_b.at[bi], sg_ref, sw_ref, sbt_ref, yb_b.at[bi],
                               chunks=t_rows // CHUNK_B)

    @pl.when(ci == pl.num_programs(1) - 1)
    def _():
        for bi in seqs:
            _mlstm_final(carries[bi], conv_out_b.at[bi], c_out_b.at[bi], n_out_b.at[bi], m_out_b.at[bi],
                         caug_b.at[bi], m_b.at[bi])


def _mlstm(proj, gates, conv_buf, c0, n0, m0, state_layer, layer, conv_w, conv_b3, gate_bias3, hn_g3, t_rows,
           sgu=None):
    b, seq, _ = proj.shape
    nchunks = seq // t_rows
    hp = SUBLANES
    nb = MLSTM_SEQS_PER_STEP
    assert b % nb == 0
    col = lambda base, width: pl.BlockSpec((nb, t_rows, width), lambda bi, ci: (bi, ci, base))
    param = lambda shape: pl.BlockSpec((None,) + shape, lambda bi, ci: (layer, 0, 0))
    per_b3 = lambda shape: pl.BlockSpec((nb,) + shape, lambda bi, ci: (bi, 0, 0))
    per_b4 = lambda shape: pl.BlockSpec((nb,) + shape, lambda bi, ci: (bi, 0, 0, 0))
    state = lambda shape: pl.BlockSpec((None, nb) + shape,
                                       lambda bi, ci: (state_layer, bi) + (0,) * len(shape))
    sgu_in, sgu_args, sgu_shape, sgu_out = [], [], [], []
    if sgu is not None:
        assert t_rows % CHUNK_B == 0
        sgu_in = [pl.BlockSpec((nb, t_rows, D_B), lambda bi, ci, c=c: (bi, ci, c)) for c in (COL_UB, COL_VB, COL_ZB)]
        sgu_in += [pl.BlockSpec((None, 1, D_B), lambda bi, ci: (layer, 0, 0)),
                   pl.BlockSpec((None, G_B, CHUNK_B, CHUNK_B), lambda bi, ci: (layer, 0, 0, 0)),
                   pl.BlockSpec((None, CHUNK_B, G_B), lambda bi, ci: (layer, 0, 0))]
        sgu_args = [proj, proj, proj, *sgu]
        sgu_shape = [jax.ShapeDtypeStruct((b, seq, D_B), BF16)]
        sgu_out = [pl.BlockSpec((nb, t_rows, D_B), lambda bi, ci: (bi, ci, 0))]
    return pl.pallas_call(
        functools.partial(_mlstm_kernel, t_rows=t_rows, nb=nb, with_sgu=sgu is not None),
        out_shape=(jax.ShapeDtypeStruct((b, seq, D_C), BF16),
                   jax.ShapeDtypeStruct((b, CONV_W - 1, 2 * D_C), F32),
                   jax.ShapeDtypeStruct((b, H_C, DH_C, DH_C), F32),
                   jax.ShapeDtypeStruct((b, hp, DH_C), F32),
                   jax.ShapeDtypeStruct((b, hp, LANES), F32), *sgu_shape),
        grid=(b // nb, nchunks),
        in_specs=[col(COL_QK, 2 * D_C), col(COL_VC, D_C), col(COL_OC, D_C), col(COL_ZC, D_C),
                  pl.BlockSpec((nb, t_rows, LANES), lambda bi, ci: (bi, ci, 0)),
                  state((CONV_W - 1, 2 * D_C)), state((H_C, DH_C, DH_C)), state((H_C, DH_C)),
                  state((hp, LANES)),
                  param((CONV_W, 2 * D_C)), param((1, 2 * D_C)), param((1, LANES)), param((1, D_C)), *sgu_in],
        out_specs=(pl.BlockSpec((nb, t_rows, D_C), lambda bi, ci: (bi, ci, 0)),
                   per_b3((CONV_W - 1, 2 * D_C)), per_b4((H_C, DH_C, DH_C)), per_b3((hp, DH_C)),
                   per_b3((hp, LANES)), *sgu_out),
        scratch_shapes=[pltpu.VMEM((nb, MLSTM_ROW0 + CHUNK_C, 2 * D_C), F32),
                        pltpu.VMEM((nb, CHUNK_C, D_C), F32),
                        pltpu.VMEM((nb, CHUNK_C, LANES), F32),
                        pltpu.VMEM((nb, H_C, DH_C, 2 * DH_C), F32),
                        pltpu.VMEM((nb, hp, LANES), F32)],
        compiler_params=_cparams(("arbitrary", "arbitrary")),
        name="mlstm",
    )(proj, proj, proj, proj, gates, conv_buf, c0, n0, m0, conv_w, conv_b3, gate_bias3, hn_g3, *sgu_args)


def _gate_lanes(i_part, f_part):
    lead = i_part.shape[:-1]
    gap = jnp.zeros(lead + (SUBLANES - H_C,), i_part.dtype)
    tail = jnp.zeros(lead + (LANES - 2 * SUBLANES,), i_part.dtype)
    return jnp.concatenate([i_part, gap, f_part, gap, tail], axis=-1)


def _pad_heads(a):
    pad = [(0, 0)] * a.ndim
    pad[1] = (0, SUBLANES - a.shape[1])
    return jnp.pad(a, pad)


def _mixers(proj, gates, lw, layer, attn_fn, sgu_fn, conv_buf, c0, n0, m0, state_layer, t_rows, fused_sgu=None):
    ya, nk, nv = attn_fn(proj)
    outs = _mlstm(proj, gates, conv_buf, c0, n0, m0, state_layer, layer,
                  lw["conv_w"], lw["conv_b"], lw["gate_bias"], lw["hn_g"], t_rows, fused_sgu)
    yc, nconv, c_new, n_new, m_new = outs[:5]
    sgu_out = outs[5] if fused_sgu is not None else sgu_fn(proj)
    return ya, sgu_out, yc, nk, nv, nconv, c_new, n_new[:, :H_C], m_new[:, :H_C, 0]


def kernel(x_prompt, x_sample, c_prompt, c_sample, cache_k_win, cache_v_win, state_conv, state_C, state_n, state_m, rel_bias, norm_g, ada_w, ada_b, w_in, qn_g, kn_g, sgu_g, sgu_w, sgu_b, conv_w, conv_b, f_bias, i_bias, hn_g, w_out):
    depth = w_in.shape[0]
    bp, seq, _ = x_prompt.shape
    bd, t_new, _ = x_sample.shape
    n_past = cache_k_win.shape[2]
    assert t_new == SUBLANES and n_past % LANES == 0 and seq % 1024 == 0

    w_in_t = _unpack_w_in(w_in)
    w_gate = _gate_lanes(w_in[:, :, D_MAIN:D_MAIN + H_C], w_in[:, :, D_MAIN + H_C:])
    w_out_bf = w_out.astype(BF16)

    rel_t = rel_bias.T
    bias_p = _expand_bias(rel_t, jnp.asarray(_prompt_bucket_table()), 64, LOG2E)
    bias_p = bias_p.reshape(H_A, N_PAT, 2, BLK_A, 2 * BLK_A)
    bias_s = _expand_bias(rel_t, jnp.asarray(_sample_bucket_table(n_past, t_new)), SUBLANES, 1.0)
    bias_s = bias_s.reshape(H_A, N_PAT, t_new, n_past + LANES)

    n_c = bp + bd
    c_all = jnp.pad(jnp.concatenate([c_prompt, c_sample], axis=0), ((0, 2 * SUBLANES - n_c), (0, 0)))
    mod = _ada_mod(c_all, ada_w, ada_b)

    cache_kt = jnp.transpose(cache_k_win, (0, 1, 3, 4, 2))
    cache_vt = jnp.transpose(cache_v_win, (0, 1, 3, 4, 2))
    gq_t = jnp.tile(qn_g, (1, H_A))[:, None, :]
    gk_t = jnp.tile(kn_g, (1, H_A))[:, None, :]
    lw = {"conv_w": conv_w, "conv_b": conv_b[:, None, :], "gate_bias": _gate_lanes(i_bias, f_bias)[:, None, :],
          "hn_g": hn_g[:, None, :]}
    sg3 = sgu_g[:, None, :]
    sgu_bt = jnp.transpose(sgu_b, (0, 2, 1))
    zeros_conv = jnp.zeros((1, bp, CONV_W - 1, 2 * D_C), F32)
    zeros_c = jnp.zeros((1, bp, H_C, DH_C, DH_C), F32)
    zeros_n = jnp.zeros((1, bp, H_C, DH_C), F32)
    zeros_m = jnp.zeros((1, bp, SUBLANES, LANES), F32)
    m0_all = jnp.broadcast_to(_pad_heads(state_m.reshape(depth * bd, H_C))[:, :, None],
                              (depth * bd, SUBLANES, LANES)).reshape(depth, bd, SUBLANES, LANES)
    keep = min(WIN_MAX, seq)
    pk_all = jnp.zeros((depth, bp, D_A, keep), F32)
    pv_all = jnp.zeros((depth, bp, D_A, keep), F32)
    w8 = sgu_w[:, :, :t_new, :t_new] * jnp.tril(jnp.ones((t_new, t_new), F32))
    wexp = jnp.repeat(jnp.transpose(w8, (0, 3, 2, 1)), C_B, axis=3)
    bexp = jnp.repeat(jnp.transpose(sgu_b[:, :, :t_new], (0, 2, 1)), C_B, axis=2)

    xp = x_prompt.reshape(bp * seq, D_MODEL)
    xs = x_sample.reshape(bd * t_new, D_MODEL)
    outs = [[] for _ in range(13)]
    mod4 = jnp.transpose(mod.reshape(depth, mod.shape[1], 3, D_MODEL), (0, 2, 1, 3))
    mod_p = mod4[:, :, :bp, None, :]
    mod_s = jnp.repeat(mod4[:, :, bp:n_c], t_new, axis=2)[:, :, None]
    norm_g3 = norm_g[:, None, :]
    hp = _norm(xp, mod_p, norm_g3, 0, 512)
    hs = _norm(xs, mod_s, norm_g3, 0, bd * t_new)
    for l in range(depth):
        last = l == depth - 1

        proj, gates = _inproj(hp, w_in_t, w_gate, l, 2048)
        proj = proj.reshape(bp, seq, D_MAIN)
        gates = gates.reshape(bp, seq, LANES)
        ya, yb, yc, pk_all, pv_all, ncv, nc_, nn_, nm = _mixers(
            proj, gates, lw, l,
            lambda pr: _attn_prompt(pr, gq_t, gk_t, bias_p, pk_all, pv_all, l),
            None,
            zeros_conv, zeros_c, zeros_n, zeros_m, 0, CHUNK_C, (sg3, sgu_w, sgu_bt))
        xp, hp = _outproj(ya.reshape(bp * seq, D_A), yb.reshape(bp * seq, D_B), yc.reshape(bp * seq, D_C),
                          xp, mod_p, norm_g3, w_out_bf, l, 512, not last)
        for i, a in enumerate((ncv, nc_, nn_, nm)):
            outs[2 + i].append(a)

        proj_s, gates_s = _inproj(hs, w_in_t, w_gate, l, bd * t_new, INPROJ_TN_SAMPLE)
        proj_s = proj_s.reshape(bd, t_new, D_MAIN)
        gates_s = gates_s.reshape(bd, t_new, LANES)
        ya, sgu_out, yc, nk, nv, ncv, nc_, nn_, nm = _mixers(
            proj_s, gates_s, lw, l,
            lambda pr: _attn_sample(pr, cache_kt, cache_vt, gq_t, gk_t, bias_s, l),
            lambda pr: _sgu_sample(pr, sg3, wexp, bexp, l),
            state_conv, state_C, state_n, m0_all, l, t_new)
        yb, vn = sgu_out
        xs, hs = _outproj(ya.reshape(bd * t_new, D_A), yb.reshape(bd * t_new, D_B), yc.reshape(bd * t_new, D_C),
                          xs, mod_s, norm_g3, w_out_bf, l, bd * t_new, not last)
        for i, a in enumerate((nk.reshape(bd, t_new, H_A, HEAD_DIM_A), nv.reshape(bd, t_new, H_A, HEAD_DIM_A),
                               vn, ncv, nc_, nn_, nm)):
            outs[6 + i].append(a)

    stacked = [jnp.stack(o) for o in outs[2:]]
    p_k, p_v = (jnp.transpose(a.reshape(depth, bp, H_A, HEAD_DIM_A, keep), (0, 1, 4, 2, 3)) for a in (pk_all, pv_all))
    return (xp.reshape(bp, seq, D_MODEL), xs.reshape(bd, t_new, D_MODEL), p_k, p_v, *stacked)
```
